```python
import jax
import jax.numpy as jnp
from jax import lax
import numpy as np

D_MODEL = 2048
BATCH = 4
SEQ = 2048
DEPTH = 4

N_MIXERS = 2
ROPE_THETA = 500000.0
LN_EPS = 1e-5
RMS_EPS = 1e-6
DEEPNORM_ALPHA = (2.0 * DEPTH) ** 0.25
DEEPNORM_BETA = (8.0 * DEPTH) ** -0.25
ADA_SCALE = 0.1

MLA_HEADS = D_MODEL // 128
MLA_Q_LORA = D_MODEL // 4
MLA_KV_LORA = D_MODEL // 4
MLA_NOPE = 128
MLA_ROPE = 64
MLA_V = 128
ATTN_Q_BLOCK = 128

MOBA_HEADS = D_MODEL // 128
MOBA_HEAD_DIM = 128
MOBA_ROT_DIM = MOBA_HEAD_DIM // 4
MOBA_BLOCK = 256
MOBA_TOPK = 3
MOBA_Q_CHUNK = 16

D_FF = 5632
N_EXPERTS = 8
TOP_K = 2
MOE_ROW_BLOCK = 128

N_EVEN = (DEPTH + 1) // 2
N_ODD = DEPTH // 2

kernel_name = 'hybrid_mla_moba_deepnorm_adaln_moe'


def layer_norm(x, g, b):
    xf = x.astype(jnp.float32)
    mu = jnp.mean(xf, axis=-1, keepdims=True)
    var = jnp.mean(jnp.square(xf - mu), axis=-1, keepdims=True)
    return ((xf - mu) * lax.rsqrt(var + LN_EPS) * g + b).astype(x.dtype)


def rms_norm(x, g):
    xf = x.astype(jnp.float32)
    ms = jnp.mean(jnp.square(xf), axis=-1, keepdims=True)
    return (xf * lax.rsqrt(ms + RMS_EPS) * g).astype(x.dtype)


def rope_angles(positions, dim):
    inv_freq = ROPE_THETA ** (-jnp.arange(0, dim, 2, dtype=jnp.float32) / dim)
    ang = positions.astype(jnp.float32)[..., None] * inv_freq
    return jnp.cos(ang), jnp.sin(ang)


def apply_rope(x, cos, sin):
    half = x.shape[-1] // 2
    x1, x2 = x[..., :half], x[..., half:]
    cos = cos.astype(x.dtype)
    sin = sin.astype(x.dtype)
    return jnp.concatenate([x1 * cos - x2 * sin, x2 * cos + x1 * sin], axis=-1)


def swiglu(h, w_gate_up, w_down):
    gate, up = jnp.split(h @ w_gate_up, 2, axis=-1)
    return (jax.nn.silu(gate) * up) @ w_down


def mla_attention(q_nope, q_rope, k_nope, k_rope, v):
    B, S, H, _ = q_nope.shape
    scale = (MLA_NOPE + MLA_ROPE) ** -0.5
    key_pos = jnp.arange(S)

    def block(i):
        start = i * ATTN_Q_BLOCK
        qn = lax.dynamic_slice_in_dim(q_nope, start, ATTN_Q_BLOCK, axis=1)
        qr = lax.dynamic_slice_in_dim(q_rope, start, ATTN_Q_BLOCK, axis=1)
        s = (jnp.einsum('bqhd,bkhd->bhqk', qn, k_nope, preferred_element_type=jnp.float32)
             + jnp.einsum('bqhr,bkr->bhqk', qr, k_rope, preferred_element_type=jnp.float32)) * scale
        q_pos = start + jnp.arange(ATTN_Q_BLOCK)
        s = jnp.where(key_pos[None, :] <= q_pos[:, None], s, -jnp.inf)
        p = jax.nn.softmax(s, axis=-1).astype(v.dtype)
        return jnp.einsum('bhqk,bkhd->bqhd', p, v)

    out = lax.map(block, jnp.arange(S // ATTN_Q_BLOCK))
    return out.transpose(1, 0, 2, 3, 4).reshape(B, S, H, MLA_V)


def mla_mixer(u, cos, sin, w_down, q_norm, kv_norm, w_uq, w_ukv, w_o):
    B, S, _ = u.shape
    c_q, c_kv, k_rope = jnp.split(u @ w_down, [MLA_Q_LORA, MLA_Q_LORA + MLA_KV_LORA], axis=-1)
    c_q = rms_norm(c_q, q_norm)
    c_kv = rms_norm(c_kv, kv_norm)
    q = (c_q @ w_uq).reshape(B, S, MLA_HEADS, MLA_NOPE + MLA_ROPE)
    q_nope = q[..., :MLA_NOPE]
    q_rope = apply_rope(q[..., MLA_NOPE:], cos[:, :, None, :], sin[:, :, None, :])
    kv = (c_kv @ w_ukv).reshape(B, S, MLA_HEADS, MLA_NOPE + MLA_V)
    k_nope, v = kv[..., :MLA_NOPE], kv[..., MLA_NOPE:]
    k_rope = apply_rope(k_rope, cos, sin)
    o = mla_attention(q_nope, q_rope, k_nope, k_rope, v)
    return o.reshape(B, S, MLA_HEADS * MLA_V) @ w_o


def moba_attention(q, k, v):
    B, S, H, Dh = q.shape
    nb = -(-S // MOBA_BLOCK)
    pad = nb * MOBA_BLOCK - S
    topk = min(MOBA_TOPK, nb)
    scale = Dh ** -0.5

    def to_blocks(t):
        t = jnp.pad(t, ((0, 0), (0, pad), (0, 0), (0, 0)))
        return t.reshape(B, nb, MOBA_BLOCK, H, Dh).transpose(0, 3, 1, 2, 4)

    kb, vb = to_blocks(k), to_blocks(v)
    k_mean = jnp.mean(kb.astype(jnp.float32), axis=3)
    qh = q.transpose(0, 2, 1, 3)
    blk_ids = jnp.arange(nb)
    b_idx = jnp.arange(B)[:, None, None, None]
    h_idx = jnp.arange(H)[None, :, None, None]

    def chunk(i):
        start = i * MOBA_Q_CHUNK
        qc = lax.dynamic_slice_in_dim(qh, start, MOBA_Q_CHUNK, axis=2)
        q_pos = start + jnp.arange(MOBA_Q_CHUNK)
        own = start // MOBA_BLOCK
        gate = jnp.einsum('bhqd,bhnd->bhqn', qc.astype(jnp.float32), k_mean)
        gate = jnp.where(blk_ids < own, gate, -jnp.inf)
        _, sel = lax.top_k(gate, topk)
        k_sel = kb[b_idx, h_idx, sel]
        v_sel = vb[b_idx, h_idx, sel]
        s_sel = jnp.einsum('bhqd,bhqnkd->bhqnk', qc, k_sel,
                           preferred_element_type=jnp.float32) * scale
        valid = (jnp.arange(topk) < own)[:, None]
        s_sel = jnp.where(valid, s_sel, -jnp.inf).reshape(B, H, MOBA_Q_CHUNK, topk * MOBA_BLOCK)
        k_own = lax.dynamic_index_in_dim(kb, own, axis=2, keepdims=False)
        v_own = lax.dynamic_index_in_dim(vb, own, axis=2, keepdims=False)
        s_own = jnp.einsum('bhqd,bhkd->bhqk', qc, k_own,
                           preferred_element_type=jnp.float32) * scale
        own_pos = own * MOBA_BLOCK + jnp.arange(MOBA_BLOCK)
        s_own = jnp.where(own_pos[None, :] <= q_pos[:, None], s_own, -jnp.inf)
        p = jax.nn.softmax(jnp.concatenate([s_sel, s_own], axis=-1), axis=-1).astype(v.dtype)
        p_sel = p[..., :topk * MOBA_BLOCK].reshape(B, H, MOBA_Q_CHUNK, topk, MOBA_BLOCK)
        p_own = p[..., topk * MOBA_BLOCK:]
        return (jnp.einsum('bhqnk,bhqnkd->bhqd', p_sel, v_sel)
                + jnp.einsum('bhqk,bhkd->bhqd', p_own, v_own))

    out = lax.map(chunk, jnp.arange(S // MOBA_Q_CHUNK))
    return out.transpose(1, 0, 3, 2, 4).reshape(B, S, H, Dh)


def moba_mixer(u, cos, sin, w_qkv, w_o):
    B, S, _ = u.shape
    qkv = (u @ w_qkv).reshape(B, S, 3, MOBA_HEADS, MOBA_HEAD_DIM)
    q, k, v = qkv[:, :, 0], qkv[:, :, 1], qkv[:, :, 2]
    cos4, sin4 = cos[:, :, None, :], sin[:, :, None, :]
    q = jnp.concatenate([apply_rope(q[..., :MOBA_ROT_DIM], cos4, sin4), q[..., MOBA_ROT_DIM:]], axis=-1)
    k = jnp.concatenate([apply_rope(k[..., :MOBA_ROT_DIM], cos4, sin4), k[..., MOBA_ROT_DIM:]], axis=-1)
    o = moba_attention(q, k, v)
    return o.reshape(B, S, MOBA_HEADS * MOBA_HEAD_DIM) @ w_o


def moe_swiglu(u, w_router, e_gate_up, e_down):
    B, S, D = u.shape
    T = B * S
    t = u.reshape(T, D)
    logits = jnp.einsum('td,de->te', t, w_router, preferred_element_type=jnp.float32)
    top_logit, top_idx = lax.top_k(logits, TOP_K)
    top_w = jax.nn.softmax(top_logit, axis=-1)
    A = T * TOP_K
    expert_flat = top_idx.reshape(A)
    token_flat = jnp.repeat(jnp.arange(T, dtype=jnp.int32), TOP_K)
    weight_flat = top_w.reshape(A)
    order = jnp.argsort(expert_flat)
    se, st, sw = expert_flat[order], token_flat[order], weight_flat[order]
    counts = jnp.bincount(expert_flat, length=N_EXPERTS)
    padded = (counts + MOE_ROW_BLOCK - 1) // MOE_ROW_BLOCK * MOE_ROW_BLOCK
    pad_end = jnp.cumsum(padded)
    pad_start = pad_end - padded
    grp_start = jnp.cumsum(counts) - counts
    dest = pad_start[se] + (jnp.arange(A, dtype=jnp.int32) - grp_start[se])
    P = -(-A // MOE_ROW_BLOCK) * MOE_ROW_BLOCK + N_EXPERTS * MOE_ROW_BLOCK
    row_token = jnp.zeros((P,), jnp.int32).at[dest].set(st)
    row_w = jnp.zeros((P,), jnp.float32).at[dest].set(sw)
    n_blk = P // MOE_ROW_BLOCK
    blk_start = jnp.arange(n_blk) * MOE_ROW_BLOCK
    blk_expert = jnp.minimum(jnp.sum(pad_end[None, :] <= blk_start[:, None], axis=1), N_EXPERTS - 1)
    xs = t[row_token].reshape(n_blk, MOE_ROW_BLOCK, D)

    def run(args):
        xb, e = args
        return swiglu(xb, e_gate_up[e], e_down[e])

    ys = lax.map(run, (xs, blk_expert)).reshape(P, D)
    out = jnp.zeros_like(t).at[row_token].add(ys * row_w[:, None].astype(ys.dtype))
    return out.reshape(B, S, D)


def setup_inputs(seed: int = 0) -> dict:
    key = jax.random.key(seed)
    ks = jax.random.split(key, 22)
    D = D_MODEL

    def nrm(k, shape, fan_in, scale=1.0):
        return jax.random.normal(k, shape, jnp.float32) * (scale * fan_in ** -0.5)

    def small(k, shape):
        return 0.01 * jax.random.normal(k, shape, jnp.float32)

    x = jax.random.normal(ks[0], (BATCH, SEQ, D), jnp.float32)
    c = jax.random.normal(ks[1], (BATCH, D), jnp.float32)
    positions = (jax.random.randint(ks[2], (BATCH, 1), 0, 1024, dtype=jnp.int32)
                 + jnp.arange(SEQ, dtype=jnp.int32)[None, :])
    w_ada = nrm(ks[3], (DEPTH, D, 6 * D), D, ADA_SCALE)
    b_ada = small(ks[4], (DEPTH, 6 * D))
    ln_mix_g = 1.0 + small(ks[5], (DEPTH, D))
    ln_mix_b = small(ks[6], (DEPTH, D))
    ln_ffn_g = 1.0 + small(ks[7], (DEPTH, D))
    ln_ffn_b = small(ks[8], (DEPTH, D))
    mla_w_down = nrm(ks[9], (N_EVEN, D, MLA_Q_LORA + MLA_KV_LORA + MLA_ROPE), D)
    mla_q_norm = 1.0 + small(ks[10], (N_EVEN, MLA_Q_LORA))
    mla_kv_norm = 1.0 + small(ks[11], (N_EVEN, MLA_KV_LORA))
    mla_w_uq = nrm(ks[12], (N_EVEN, MLA_Q_LORA, MLA_HEADS * (MLA_NOPE + MLA_ROPE)), MLA_Q_LORA)
    mla_w_ukv = nrm(ks[13], (N_EVEN, MLA_KV_LORA, MLA_HEADS * (MLA_NOPE + MLA_V)), MLA_KV_LORA)
    mla_w_o = nrm(ks[14], (N_EVEN, MLA_HEADS * MLA_V, D), MLA_HEADS * MLA_V, DEEPNORM_BETA)
    moba_w_qkv = nrm(ks[15], (N_ODD, D, 3 * MOBA_HEADS * MOBA_HEAD_DIM), D)
    moba_w_o = nrm(ks[16], (N_ODD, MOBA_HEADS * MOBA_HEAD_DIM, D), MOBA_HEADS * MOBA_HEAD_DIM, DEEPNORM_BETA)
    ffn_w_gate_up = nrm(ks[17], (N_EVEN, D, 2 * D_FF), D)
    ffn_w_down = nrm(ks[18], (N_EVEN, D_FF, D), D_FF, DEEPNORM_BETA)
    moe_w_router = nrm(ks[19], (N_ODD, D, N_EXPERTS), D)
    moe_w_gate_up = nrm(ks[20], (N_ODD, N_EXPERTS, D, 2 * D_FF), D)
    moe_w_down = nrm(ks[21], (N_ODD, N_EXPERTS, D_FF, D), D_FF, DEEPNORM_BETA)
    return {'x': x, 'c': c, 'positions': positions, 'w_ada': w_ada, 'b_ada': b_ada,
            'ln_mix_g': ln_mix_g, 'ln_mix_b': ln_mix_b, 'ln_ffn_g': ln_ffn_g, 'ln_ffn_b': ln_ffn_b,
            'mla_w_down': mla_w_down, 'mla_q_norm': mla_q_norm, 'mla_kv_norm': mla_kv_norm,
            'mla_w_uq': mla_w_uq, 'mla_w_ukv': mla_w_ukv, 'mla_w_o': mla_w_o,
            'moba_w_qkv': moba_w_qkv, 'moba_w_o': moba_w_o,
            'ffn_w_gate_up': ffn_w_gate_up, 'ffn_w_down': ffn_w_down,
            'moe_w_router': moe_w_router, 'moe_w_gate_up': moe_w_gate_up, 'moe_w_down': moe_w_down}


def reference(x, c, positions, w_ada, b_ada, ln_mix_g, ln_mix_b, ln_ffn_g, ln_ffn_b,
              mla_w_down, mla_q_norm, mla_kv_norm, mla_w_uq, mla_w_ukv, mla_w_o,
              moba_w_qkv, moba_w_o, ffn_w_gate_up, ffn_w_down,
              moe_w_router, moe_w_gate_up, moe_w_down):
    cos_mla, sin_mla = rope_angles(positions, MLA_ROPE)
    cos_moba, sin_moba = rope_angles(positions, MOBA_ROT_DIM)
    c_act = jax.nn.silu(c)
    for l in range(DEPTH):
        j = l // 2
        ada = c_act @ w_ada[l] + b_ada[l]
        sh_m, sc_m, g_m, sh_f, sc_f, g_f = jnp.split(ada[:, None, :], 6, axis=-1)
        u = x * (1.0 + sc_m) + sh_m
        if l % N_MIXERS == 0:
            y = mla_mixer(u, cos_mla, sin_mla, mla_w_down[j], mla_q_norm[j], mla_kv_norm[j],
                          mla_w_uq[j], mla_w_ukv[j], mla_w_o[j])
        else:
            y = moba_mixer(u, cos_moba, sin_moba, moba_w_qkv[j], moba_w_o[j])
        x = layer_norm(DEEPNORM_ALPHA * x + (1.0 + g_m) * y, ln_mix_g[l], ln_mix_b[l])
        u = x * (1.0 + sc_f) + sh_f
        if l % 2 == 0:
            y = swiglu(u, ffn_w_gate_up[j], ffn_w_down[j])
        else:
            y = moe_swiglu(u, moe_w_router[j], moe_w_gate_up[j], moe_w_down[j])
        x = layer_norm(DEEPNORM_ALPHA * x + (1.0 + g_f) * y, ln_ffn_g[l], ln_ffn_b[l])
    return x
```

```python
import functools

import jax
import jax.numpy as jnp
from jax import lax
from jax.experimental import pallas as pl
from jax.experimental.pallas import tpu as pltpu

F32 = jnp.float32
BF16 = jnp.bfloat16

ROPE_THETA = 500000.0
LN_EPS = 1e-5
RMS_EPS = 1e-6
MLA_NOPE = 128
MLA_ROPE = 64
MLA_V = 128
MOBA_HEAD_DIM = 128
MOBA_ROT_DIM = 32
MOBA_BLOCK = 256
MOBA_TOPK = 3
TOP_K = 2

LANES = 128
SUBLANES = 8
MLA_HEAD_PAD = 2 * LANES

ADA_BATCH_PAD = SUBLANES
MOE_ROW_BLOCK = 256
MOE_SUPER_BLOCKS = 8
MOE_ITEMS = 16


def _cparams(n_axes, vmem_mb):
    return pltpu.CompilerParams(dimension_semantics=("arbitrary",) * n_axes,
                                vmem_limit_bytes=vmem_mb * 1024 * 1024)


def _split_bf16(x):
    hi = x.astype(BF16)
    return hi, (x - hi.astype(F32)).astype(BF16)


def _ada_row(layer, batch, comp):
    return (layer * ADA_BATCH_PAD + batch) * 6 + comp


def _ada_kernel(c_ref, w_ref, b_ref, o_ref):
    c = c_ref[...]
    ca = (c * jax.nn.sigmoid(c)).astype(BF16)
    o_ref[...] = jnp.dot(ca, w_ref[...].astype(BF16), preferred_element_type=F32) + b_ref[...]


def _ada_all(c, w_ada, b_ada):
    depth, d, n6 = w_ada.shape
    b = c.shape[0]
    c_pad = jnp.zeros((ADA_BATCH_PAD, d), F32).at[:b].set(c)
    tn = 1024
    out = pl.pallas_call(
        _ada_kernel,
        grid=(depth, n6 // tn),
        in_specs=[pl.BlockSpec((ADA_BATCH_PAD, d), lambda l, j: (0, 0)),
                  pl.BlockSpec((None, d, tn), lambda l, j: (l, 0, j)),
                  pl.BlockSpec((None, 1, tn), lambda l, j: (l, 0, j))],
        out_specs=pl.BlockSpec((None, ADA_BATCH_PAD, tn), lambda l, j: (l, 0, j)),
        out_shape=jax.ShapeDtypeStruct((depth, ADA_BATCH_PAD, n6), F32),
        compiler_params=_cparams(2, 40),
        name="ada",
    )(c_pad, w_ada, b_ada.reshape(depth, 1, n6))
    return out.reshape(depth * ADA_BATCH_PAD * 6, 1, d)


def _modulate_kernel(x_ref, sc_ref, sh_ref, u_ref):
    u_ref[...] = (x_ref[...] * (1.0 + sc_ref[0]) + sh_ref[0]).astype(u_ref.dtype)


def _modulate(x, ada_rows, layer, seq, comp_sc, comp_sh, out_dtype):
    t, d = x.shape
    tm = 512
    tpb = seq // tm
    row = lambda comp: pl.BlockSpec((1, 1, d), lambda i: (_ada_row(layer, i // tpb, comp), 0, 0))
    return pl.pallas_call(
        _modulate_kernel,
        grid=(t // tm,),
        in_specs=[pl.BlockSpec((tm, d), lambda i: (i, 0)), row(comp_sc), row(comp_sh)],
        out_specs=pl.BlockSpec((tm, d), lambda i: (i, 0)),
        out_shape=jax.ShapeDtypeStruct((t, d), out_dtype),
        compiler_params=_cparams(1, 40),
        name="modulate",
    )(x, ada_rows, ada_rows)


def _ln_modulate(z, lg, lb, sc_sh):
    mu = jnp.mean(z, axis=-1, keepdims=True)
    zc = z - mu
    var = jnp.mean(zc * zc, axis=-1, keepdims=True)
    xn = zc * lax.rsqrt(var + LN_EPS) * lg + lb
    if sc_sh is None:
        return xn, None
    sc, sh = sc_sh
    return xn, xn * (1.0 + sc) + sh


def _resid_ln_kernel(*refs, alpha, emit_u):
    if emit_u:
        x_ref, y_ref, g_ref, lg_ref, lb_ref, sc_ref, sh_ref, xo_ref, uo_ref = refs
        sc_sh = (sc_ref[0], sh_ref[0])
    else:
        x_ref, y_ref, g_ref, lg_ref, lb_ref, xo_ref = refs
        sc_sh = None
    z = alpha * x_ref[...] + (1.0 + g_ref[0]) * y_ref[...].astype(F32)
    xn, u = _ln_modulate(z, lg_ref[0], lb_ref[0], sc_sh)
    xo_ref[...] = xn
    if emit_u:
        uo_ref[...] = u.astype(uo_ref.dtype)


def _resid_ln(x, y, ada_rows, ln_g, ln_b, *, alpha, seq, layer, comp_g, nxt, u_dtype):
    t, d = x.shape
    tm = 256
    tpb = seq // tm
    row = lambda l, comp: pl.BlockSpec((1, 1, d), lambda i: (_ada_row(l, i // tpb, comp), 0, 0))
    tile = pl.BlockSpec((tm, d), lambda i: (i, 0))
    lnp = pl.BlockSpec((1, 1, d), lambda i: (layer, 0, 0))
    in_specs = [tile, tile, row(layer, comp_g), lnp, lnp]
    args = [x, y, ada_rows, ln_g, ln_b]
    out_specs = [tile]
    out_shape = [jax.ShapeDtypeStruct((t, d), F32)]
    if nxt is not None:
        in_specs += [row(nxt[0], nxt[1]), row(nxt[0], nxt[2])]
        args += [ada_rows, ada_rows]
        out_specs.append(tile)
        out_shape.append(jax.ShapeDtypeStruct((t, d), u_dtype))
    outs = pl.pallas_call(
        functools.partial(_resid_ln_kernel, alpha=alpha, emit_u=nxt is not None),
        grid=(t // tm,),
        in_specs=in_specs, out_specs=out_specs, out_shape=out_shape,
        compiler_params=_cparams(1, 48),
        name="resid_ln",
    )(*args)
    return (outs[0], outs[1]) if nxt is not None else (outs[0], None)


def _mm_kernel(a_ref, w_ref, o_ref, wb_ref):
    @pl.when(pl.program_id(1) == 0)
    def _():
        wb_ref[...] = w_ref[...].astype(BF16)
    o_ref[...] = jnp.dot(a_ref[...], wb_ref[...], preferred_element_type=F32).astype(o_ref.dtype)


def _matmul(a, w_stack, layer, out_dtype, *, tm, tn):
    m, k = a.shape
    n = w_stack.shape[2]
    return pl.pallas_call(
        _mm_kernel,
        grid=(n // tn, m // tm),
        in_specs=[pl.BlockSpec((tm, k), lambda j, i: (i, 0)),
                  pl.BlockSpec((None, k, tn), lambda j, i: (layer, 0, j))],
        out_specs=pl.BlockSpec((tm, tn), lambda j, i: (i, j)),
        out_shape=jax.ShapeDtypeStruct((m, n), out_dtype),
        scratch_shapes=[pltpu.VMEM((k, tn), BF16)],
        compiler_params=_cparams(2, 48),
        name="matmul",
    )(a, w_stack)


def _rms(x, g):
    ms = jnp.mean(x * x, axis=-1, keepdims=True)
    return x * lax.rsqrt(ms + RMS_EPS) * g


def _mla_down_kernel(u_ref, w_ref, qn_ref, kvn_ref, c_ref, s_ref, cq_ref, ckv_ref, kr_ref, *, ql, kvl):
    acc = jnp.dot(u_ref[...], w_ref[...], preferred_element_type=F32)
    cq_ref[...] = _rms(acc[:, :ql], qn_ref[0]).astype(BF16)
    ckv_ref[...] = _rms(acc[:, ql:ql + kvl], kvn_ref[0]).astype(BF16)
    xr = acc[:, ql + kvl:]
    kr_ref[...] = (xr * c_ref[...] + pltpu.roll(xr, LANES // 2, 1) * s_ref[...]).astype(BF16)


def _mla_down(u, w_perm, q_norm, kv_norm, layer, cos_t, sin_t, ql, kvl):
    t, d = u.shape
    n = w_perm.shape[1]
    tm = 512
    nrm = lambda width: pl.BlockSpec((1, 1, width), lambda i: (layer, 0, 0))
    rows = lambda width: pl.BlockSpec((tm, width), lambda i: (i, 0))
    return pl.pallas_call(
        functools.partial(_mla_down_kernel, ql=ql, kvl=kvl),
        grid=(t // tm,),
        in_specs=[rows(d), pl.BlockSpec((d, n), lambda i: (0, 0)), nrm(ql), nrm(kvl), rows(LANES), rows(LANES)],
        out_specs=[rows(ql), rows(kvl), rows(LANES)],
        out_shape=[jax.ShapeDtypeStruct((t, ql), BF16), jax.ShapeDtypeStruct((t, kvl), BF16),
                   jax.ShapeDtypeStruct((t, LANES), BF16)],
        compiler_params=_cparams(1, 48),
        name="mla_down",
    )(u, w_perm, q_norm, kv_norm, cos_t, sin_t)


def _mla_qup_kernel(a_ref, w_ref, c_ref, s_ref, o_ref, *, scale, heads):
    acc = jnp.dot(a_ref[...], w_ref[...], preferred_element_type=F32)
    c = c_ref[...]
    s = s_ref[...]
    for h in range(heads):
        b0 = h * MLA_HEAD_PAD
        o_ref[:, b0:b0 + LANES] = (acc[:, b0:b0 + LANES] * scale).astype(BF16)
        xr = acc[:, b0 + LANES:b0 + MLA_HEAD_PAD]
        o_ref[:, b0 + LANES:b0 + MLA_HEAD_PAD] = (
            (xr * c + pltpu.roll(xr, LANES // 2, 1) * s) * scale).astype(BF16)


def _mla_qup(cq, w_perm, cos_t, sin_t, scale):
    t, k = cq.shape
    n = w_perm.shape[1]
    tm, tn = 1024, 1024
    rows = pl.BlockSpec((tm, LANES), lambda j, i: (i, 0))
    return pl.pallas_call(
        functools.partial(_mla_qup_kernel, scale=scale, heads=tn // MLA_HEAD_PAD),
        grid=(n // tn, t // tm),
        in_specs=[pl.BlockSpec((tm, k), lambda j, i: (i, 0)), pl.BlockSpec((k, tn), lambda j, i: (0, j)),
                  rows, rows],
        out_specs=pl.BlockSpec((tm, tn), lambda j, i: (i, j)),
        out_shape=jax.ShapeDtypeStruct((t, n), BF16),
        compiler_params=_cparams(2, 48),
        name="mla_qup",
    )(cq, w_perm, cos_t, sin_t)


def _moba_qkv_kernel(u_ref, w_ref, c_ref, s1_ref, s2_ref, o_ref, wb_ref, *, scale, tiles_per_sec, heads):
    j = pl.program_id(0)

    @pl.when(pl.program_id(1) == 0)
    def _():
        wb_ref[...] = w_ref[...].astype(BF16)

    acc = jnp.dot(u_ref[...], wb_ref[...], preferred_element_type=F32)
    sec = j // tiles_per_sec

    @pl.when(sec == 2)
    def _():
        o_ref[...] = acc.astype(BF16)

    @pl.when(sec < 2)
    def _():
        c = c_ref[...]
        s1 = s1_ref[...]
        s2 = s2_ref[...]
        mul = jnp.where(sec == 0, scale, 1.0).astype(F32)
        half = MOBA_ROT_DIM // 2
        for h in range(heads):
            x = acc[:, h * LANES:(h + 1) * LANES]
            r = x * c + pltpu.roll(x, half, 1) * s1 + pltpu.roll(x, LANES - half, 1) * s2
            o_ref[:, h * LANES:(h + 1) * LANES] = (r * mul).astype(BF16)


def _moba_qkv(u, w_stack, layer, c_t, s1_t, s2_t, scale):
    t, k = u.shape
    n = w_stack.shape[2]
    tm, tn = 1024, 512
    rows = pl.BlockSpec((tm, LANES), lambda j, i: (i, 0))
    return pl.pallas_call(
        functools.partial(_moba_qkv_kernel, scale=scale, tiles_per_sec=(n // 3) // tn, heads=tn // LANES),
        grid=(n // tn, t // tm),
        in_specs=[pl.BlockSpec((tm, k), lambda j, i: (i, 0)),
                  pl.BlockSpec((None, k, tn), lambda j, i: (layer, 0, j)), rows, rows, rows],
        out_specs=pl.BlockSpec((tm, tn), lambda j, i: (i, j)),
        out_shape=jax.ShapeDtypeStruct((t, n), BF16),
        scratch_shapes=[pltpu.VMEM((k, tn), BF16)],
        compiler_params=_cparams(2, 48),
        name="moba_qkv",
    )(u, w_stack, c_t, s1_t, s2_t)


_NT = (((1,), (1,)), ((), ()))


def _mla_attn_kernel(q_ref, kv_ref, kr_ref, o_ref, kfull_ref, *, tq):
    qi = pl.program_id(2)

    @pl.when(qi == 0)
    def _():
        kfull_ref[:, :LANES] = kv_ref[:, :LANES]
        kfull_ref[:, LANES:] = kr_ref[...]

    q = q_ref[...]

    def scores(off):
        k = kfull_ref[pl.ds(off, tq), :]
        return lax.dot_general(q, k, _NT, preferred_element_type=F32)

    def update(carry, s, off):
        m, l, acc = carry
        m_new = jnp.maximum(m, jnp.max(s, axis=-1, keepdims=True))
        a = jnp.exp(m - m_new)
        p = jnp.exp(s - m_new)
        l = a * l + jnp.sum(p, axis=-1, keepdims=True)
        v = kv_ref[pl.ds(off, tq), LANES:]
        acc = a * acc + jnp.dot(p.astype(BF16), v, preferred_element_type=F32)
        return m_new, l, acc

    def past(cidx, carry):
        off = pl.multiple_of(cidx * tq, tq)
        return update(carry, scores(off), off)

    init = (jnp.full((tq, 1), -jnp.inf, F32), jnp.zeros((tq, 1), F32), jnp.zeros((tq, MLA_V), F32))
    carry = lax.fori_loop(0, qi, past, init)
    off = pl.multiple_of(qi * tq, tq)
    s = scores(off)
    row = lax.broadcasted_iota(jnp.int32, (tq, tq), 0)
    col = lax.broadcasted_iota(jnp.int32, (tq, tq), 1)
    s = jnp.where(col <= row, s, -jnp.inf)
    m, l, acc = update(carry, s, off)
    o_ref[...] = (acc / l).astype(o_ref.dtype)


def _mla_attention(q, kv, kr, batch, seq, heads):
    t = q.shape[0]
    tq = 512
    nq = seq // tq
    return pl.pallas_call(
        functools.partial(_mla_attn_kernel, tq=tq),
        grid=(batch, heads, nq),
        in_specs=[pl.BlockSpec((tq, MLA_HEAD_PAD), lambda b, h, i: (b * nq + i, h)),
                  pl.BlockSpec((seq, MLA_NOPE + MLA_V), lambda b, h, i: (b, h)),
                  pl.BlockSpec((seq, LANES), lambda b, h, i: (b, 0))],
        out_specs=pl.BlockSpec((tq, MLA_V), lambda b, h, i: (b * nq + i, h)),
        out_shape=jax.ShapeDtypeStruct((t, heads * MLA_V), BF16),
        scratch_shapes=[pltpu.VMEM((seq, MLA_HEAD_PAD), BF16)],
        compiler_params=_cparams(3, 40),
        name="mla_attn",
    )(q, kv, kr)


def _moba_attn_kernel(q_ref, k_ref, v_ref, o_ref, km_ref, m_ref, l_ref, acc_ref, *, nb):
    blk = MOBA_BLOCK
    qi = pl.program_id(2)

    @pl.when(qi == 0)
    def _():
        km_ref[...] = jnp.zeros_like(km_ref)
        for j in range(nb):
            km_ref[j:j + 1, :] = jnp.mean(k_ref[j * blk:(j + 1) * blk, :].astype(F32), axis=0, keepdims=True)

    q = q_ref[...]
    kmh, kml = _split_bf16(km_ref[...])
    gate = (lax.dot_general(q, kmh, _NT, preferred_element_type=F32)
            + lax.dot_general(q, kml, _NT, preferred_element_type=F32))
    lane = lax.broadcasted_iota(jnp.int32, (blk, LANES), 1)
    past = lane < qi
    sel = []
    for j in range(nb - 1):
        gj = gate[:, j:j + 1]
        beats = past & ((gate > gj) | ((gate == gj) & (lane < j)))
        n_beats = jnp.sum(beats.astype(jnp.int32), axis=-1, keepdims=True)
        sel.append(n_beats < MOBA_TOPK)

    def update(s, v, first):
        if first:
            m_new = jnp.max(s, axis=-1, keepdims=True)
            p = jnp.exp(s - m_new)
            l_ref[...] = jnp.sum(p, axis=-1, keepdims=True)
            acc_ref[...] = jnp.dot(p.astype(BF16), v, preferred_element_type=F32)
        else:
            m_old = m_ref[...]
            m_new = jnp.maximum(m_old, jnp.max(s, axis=-1, keepdims=True))
            a = jnp.exp(m_old - m_new)
            p = jnp.exp(s - m_new)
            l_ref[...] = a * l_ref[...] + jnp.sum(p, axis=-1, keepdims=True)
            acc_ref[...] = a * acc_ref[...] + jnp.dot(p.astype(BF16), v, preferred_element_type=F32)
        m_ref[...] = m_new

    off = pl.multiple_of(qi * blk, blk)
    s = lax.dot_general(q, k_ref[pl.ds(off, blk), :], _NT, preferred_element_type=F32)
    row = lax.broadcasted_iota(jnp.int32, (blk, blk), 0)
    col = lax.broadcasted_iota(jnp.int32, (blk, blk), 1)
    update(jnp.where(col <= row, s, -jnp.inf), v_ref[pl.ds(off, blk), :], True)

    for j in range(nb - 1):
        @pl.when(j < qi)
        def _(j=j):
            sj = lax.dot_general(q, k_ref[j * blk:(j + 1) * blk, :], _NT, preferred_element_type=F32)
            update(jnp.where(sel[j], sj, -jnp.inf), v_ref[j * blk:(j + 1) * blk, :], False)

    o_ref[...] = (acc_ref[...] / l_ref[...]).astype(o_ref.dtype)


def _moba_attention(qkv, batch, seq, heads):
    t = qkv.shape[0]
    blk = MOBA_BLOCK
    nb = seq // blk
    d = MOBA_HEAD_DIM
    return pl.pallas_call(
        functools.partial(_moba_attn_kernel, nb=nb),
        grid=(batch, heads, nb),
        in_specs=[pl.BlockSpec((blk, d), lambda b, h, i: (b * nb + i, h)),
                  pl.BlockSpec((seq, d), lambda b, h, i: (b, heads + h)),
                  pl.BlockSpec((seq, d), lambda b, h, i: (b, 2 * heads + h))],
        out_specs=pl.BlockSpec((blk, d), lambda b, h, i: (b * nb + i, h)),
        out_shape=jax.ShapeDtypeStruct((t, heads * d), BF16),
        scratch_shapes=[pltpu.VMEM((LANES, d), F32),
                        pltpu.VMEM((blk, 1), F32), pltpu.VMEM((blk, 1), F32), pltpu.VMEM((blk, d), F32)],
        compiler_params=_cparams(3, 40),
        name="moba_attn",
    )(qkv, qkv, qkv)


def _swiglu_partial(x, wg, wu, wd):
    g = jnp.dot(x, wg, preferred_element_type=F32)
    u = jnp.dot(x, wu, preferred_element_type=F32)
    a = (g * jax.nn.sigmoid(g) * u).astype(BF16)
    return jnp.dot(a, wd, preferred_element_type=F32)


def _ffn_kernel(u_ref, wg_ref, wu_ref, wd_ref, o_ref):
    f = pl.program_id(1)
    y = _swiglu_partial(u_ref[...], wg_ref[...].astype(BF16), wu_ref[...].astype(BF16),
                        wd_ref[...].astype(BF16))

    @pl.when(f == 0)
    def _():
        o_ref[...] = y

    @pl.when(f > 0)
    def _():
        o_ref[...] += y


def _dense_ffn(u, w_gate_up, w_down, layer):
    t, d = u.shape
    dff = w_down.shape[1]
    tm, tf = 1024, 256
    nf = dff // tf
    return pl.pallas_call(
        _ffn_kernel,
        grid=(t // tm, nf),
        in_specs=[pl.BlockSpec((tm, d), lambda i, f: (i, 0)),
                  pl.BlockSpec((None, d, tf), lambda i, f: (layer, 0, f)),
                  pl.BlockSpec((None, d, tf), lambda i, f: (layer, 0, nf + f)),
                  pl.BlockSpec((None, tf, d), lambda i, f: (layer, f, 0))],
        out_specs=pl.BlockSpec((tm, d), lambda i, f: (i, 0)),
        out_shape=jax.ShapeDtypeStruct((t, d), F32),
        compiler_params=_cparams(2, 56),
        name="dense_ffn",
    )(u, w_gate_up, w_gate_up, w_down)


def _router_kernel(u_ref, w_ref, idx_ref, wt_ref, *, n_exp):
    uh, ul = _split_bf16(u_ref[...])
    wh, wl = _split_bf16(w_ref[...])
    dot = lambda a, b: jnp.dot(a, b, preferred_element_type=F32)
    logits = dot(uh, wh) + dot(uh, wl) + dot(ul, wh)
    lane = lax.broadcasted_iota(jnp.int32, logits.shape, 1)
    lg = jnp.where(lane < n_exp, logits, -jnp.inf)
    m1 = jnp.max(lg, axis=-1, keepdims=True)
    i1 = jnp.min(jnp.where(lg == m1, lane, LANES), axis=-1, keepdims=True)
    lg2 = jnp.where(lane == i1, -jnp.inf, lg)
    m2 = jnp.max(lg2, axis=-1, keepdims=True)
    i2 = jnp.min(jnp.where(lg2 == m2, lane, LANES), axis=-1, keepdims=True)
    e = jnp.exp(m2 - m1)
    w1 = 1.0 / (1.0 + e)
    w2 = e / (1.0 + e)
    idx_ref[...] = jnp.where(lane == 0, i1, jnp.where(lane == 1, i2, 0))
    wt_ref[...] = jnp.where(lane == 0, w1, jnp.where(lane == 1, w2, 0.0))


def _router(u32, w_router_pad, n_exp):
    t, d = u32.shape
    tm = 512
    tile = pl.BlockSpec((tm, LANES), lambda i: (i, 0))
    return pl.pallas_call(
        functools.partial(_router_kernel, n_exp=n_exp),
        grid=(t // tm,),
        in_specs=[pl.BlockSpec((tm, d), lambda i: (i, 0)), pl.BlockSpec((d, LANES), lambda i: (0, 0))],
        out_specs=[tile, tile],
        out_shape=[jax.ShapeDtypeStruct((t, LANES), jnp.int32), jax.ShapeDtypeStruct((t, LANES), F32)],
        compiler_params=_cparams(1, 40),
        name="router",
    )(u32, w_router_pad)


def _route_metadata(idx2, n_exp):
    rb, st = MOE_ROW_BLOCK, MOE_SUPER_BLOCKS
    t = idx2.shape[0]
    a = t * TOP_K
    e_flat = idx2.reshape(a)
    onehot = (e_flat[:, None] == jnp.arange(n_exp, dtype=jnp.int32)[None, :]).astype(jnp.int32)
    csum = jnp.cumsum(onehot, axis=0)
    rank = jnp.sum((csum - onehot) * onehot, axis=1)
    counts = csum[-1]
    nsub = (counts + rb - 1) // rb
    sub_start = jnp.cumsum(nsub) - nsub
    dest = jnp.sum(onehot * (sub_start * rb)[None, :], axis=1) + rank
    p_rows = (a // rb + n_exp) * rb
    token_flat = jnp.arange(a, dtype=jnp.int32) // TOP_K
    row_token = jnp.zeros((p_rows,), jnp.int32).at[dest].set(token_flat)
    n_it = (nsub + st - 1) // st
    it_end = jnp.cumsum(n_it)
    it_start = it_end - n_it
    w = jnp.arange(MOE_ITEMS, dtype=jnp.int32)
    e_w = jnp.sum((it_end[None, :] <= w[:, None]).astype(jnp.int32), axis=1)
    active = e_w < n_exp
    e_c = jnp.minimum(e_w, n_exp - 1)
    local = w - it_start[e_c]
    item_nsub = jnp.where(active, jnp.clip(nsub[e_c] - local * st, 0, st), 0)
    item_row0 = jnp.where(active, (sub_start[e_c] + local * st) * rb, 0)
    e_last = jnp.max(jnp.where(n_it > 0, jnp.arange(n_exp, dtype=jnp.int32), 0))
    item_e = jnp.where(active, e_c, e_last)
    item_row0 = jnp.concatenate([item_row0, jnp.sum(nsub, keepdims=True) * rb])
    return (dest.astype(jnp.int32), row_token, item_e.astype(jnp.int32), item_row0.astype(jnp.int32),
            item_nsub.astype(jnp.int32), p_rows)


def _row_copy(src_hbm, src_row, dst, dst_row, sem, chunks):
    at = lambda r: r * chunks if isinstance(r, int) else pl.multiple_of(r * chunks, chunks)
    return pltpu.make_async_copy(src_hbm.at[pl.ds(at(src_row), chunks), :],
                                 dst.at[pl.ds(at(dst_row), chunks), :], sem)


def _dispatch_kernel(tok_ref, u_hbm, o_ref, stage_ref, sem, *, rb, chunks):
    base = pl.program_id(0) * rb

    def issue(i, c):
        _row_copy(u_hbm, tok_ref[base + i], stage_ref, i, sem.at[0], chunks).start()
        return c

    lax.fori_loop(0, rb, issue, 0)

    def wait(i, c):
        _row_copy(u_hbm, 0, stage_ref, i, sem.at[0], chunks).wait()
        return c

    lax.fori_loop(0, rb, wait, 0)
    for c in range(chunks):
        o_ref[:, c * LANES:(c + 1) * LANES] = stage_ref[pl.ds(c, rb, stride=chunks), :].astype(o_ref.dtype)


def _dispatch(u_lin, row_token, d):
    p_rows = row_token.shape[0]
    rb = MOE_ROW_BLOCK
    chunks = d // LANES
    return pl.pallas_call(
        functools.partial(_dispatch_kernel, rb=rb, chunks=chunks),
        grid_spec=pltpu.PrefetchScalarGridSpec(
            num_scalar_prefetch=1,
            grid=(p_rows // rb,),
            in_specs=[pl.BlockSpec(memory_space=pl.ANY)],
            out_specs=pl.BlockSpec((rb, d), lambda i, tok: (i, 0)),
            scratch_shapes=[pltpu.VMEM((rb * chunks, LANES), F32), pltpu.SemaphoreType.DMA((1,))]),
        out_shape=jax.ShapeDtypeStruct((p_rows, d), BF16),
        compiler_params=_cparams(1, 40),
        name="moe_dispatch",
    )(row_token, u_lin)


def _moe_ffn_kernel(e_ref, row0_ref, nsub_ref, xs_hbm, wg_ref, wu_ref, wd_ref, ys_hbm,
                    x_ref, acc_ref, wgb_ref, wub_ref, wdb_ref, sem, *, rb, nf, n_items):
    w = pl.program_id(0)
    f = pl.program_id(1)
    nsub = nsub_ref[w]
    row0 = row0_ref[w]

    def rows_of(r):
        return pl.ds(pl.multiple_of(r * rb, rb), rb)

    def load(r):
        return pltpu.make_async_copy(xs_hbm.at[pl.ds(pl.multiple_of(row0 + r * rb, rb), rb), :],
                                     x_ref.at[rows_of(r), :], sem.at[0])

    def store(r):
        return pltpu.make_async_copy(acc_ref.at[rows_of(r), :],
                                     ys_hbm.at[pl.ds(pl.multiple_of(row0 + r * rb, rb), rb), :], sem.at[1])

    def for_blocks(fn):
        def body(r, c):
            fn(r)
            return c
        lax.fori_loop(0, nsub, body, 0)

    @pl.when(f == 0)
    def _():
        for_blocks(lambda r: load(r).start())
        for_blocks(lambda r: load(r).wait())

    @pl.when(nsub > 0)
    def _():
        wgb_ref[...] = wg_ref[...].astype(BF16)
        wub_ref[...] = wu_ref[...].astype(BF16)
        wdb_ref[...] = wd_ref[...].astype(BF16)

        def block(r):
            y = _swiglu_partial(x_ref[rows_of(r), :], wgb_ref[...], wub_ref[...], wdb_ref[...])

            @pl.when(f == 0)
            def _():
                acc_ref[rows_of(r), :] = y

            @pl.when(f > 0)
            def _():
                acc_ref[rows_of(r), :] += y

        for_blocks(block)

    @pl.when(f == nf - 1)
    def _():
        for_blocks(lambda r: store(r).start())
        for_blocks(lambda r: store(r).wait())

    @pl.when((f == nf - 1) & (w == n_items - 1))
    def _():
        used = row0_ref[n_items]
        n_tail = (ys_hbm.shape[0] - used) // rb
        acc_ref[0:rb, :] = jnp.zeros((rb, acc_ref.shape[1]), F32)

        def fill(r):
            return pltpu.make_async_copy(acc_ref.at[0:rb, :],
                                         ys_hbm.at[pl.ds(pl.multiple_of(used + r * rb, rb), rb), :], sem.at[1])

        def tail(fn):
            def body(r, c):
                fn(r)
                return c
            lax.fori_loop(0, n_tail, body, 0)

        tail(lambda r: fill(r).start())
        tail(lambda r: fill(r).wait())


def _moe_ffn(xs, w_gate_up, w_down, layer, item_e, item_row0, item_nsub):
    p_rows, d = xs.shape
    dff = w_down.shape[2]
    rb, st = MOE_ROW_BLOCK, MOE_SUPER_BLOCKS
    tf = 256
    nf = dff // tf

    def f_eff(f, nsub, w):
        return jnp.where(nsub[w] > 0, f, nf - 1)

    return pl.pallas_call(
        functools.partial(_moe_ffn_kernel, rb=rb, nf=nf, n_items=MOE_ITEMS),
        grid_spec=pltpu.PrefetchScalarGridSpec(
            num_scalar_prefetch=3,
            grid=(MOE_ITEMS, nf),
            in_specs=[pl.BlockSpec(memory_space=pl.ANY),
                      pl.BlockSpec((None, None, d, tf), lambda w, f, e, r0, ns: (layer, e[w], 0, f_eff(f, ns, w))),
                      pl.BlockSpec((None, None, d, tf),
                                   lambda w, f, e, r0, ns: (layer, e[w], 0, nf + f_eff(f, ns, w))),
                      pl.BlockSpec((None, None, tf, d), lambda w, f, e, r0, ns: (layer, e[w], f_eff(f, ns, w), 0))],
            out_specs=pl.BlockSpec(memory_space=pl.ANY),
            scratch_shapes=[pltpu.VMEM((st * rb, d), BF16), pltpu.VMEM((st * rb, d), F32),
                            pltpu.VMEM((d, tf), BF16), pltpu.VMEM((d, tf), BF16), pltpu.VMEM((tf, d), BF16),
                            pltpu.SemaphoreType.DMA((2,))]),
        out_shape=jax.ShapeDtypeStruct((p_rows, d), F32),
        compiler_params=_cparams(2, 52),
        name="moe_ffn",
    )(item_e, item_row0, item_nsub, xs, w_gate_up, w_gate_up, w_down)


def _moe_combine_ln_kernel(*refs, alpha, emit_u, tm, chunks):
    if emit_u:
        (pos_ref, ys_hbm, wt_ref, x_ref, g_ref, lg_ref, lb_ref, sc_ref, sh_ref, xo_ref, uo_ref,
         stage_ref, y_ref, sem) = refs
        sc_sh = (sc_ref[0], sh_ref[0])
    else:
        pos_ref, ys_hbm, wt_ref, x_ref, g_ref, lg_ref, lb_ref, xo_ref, stage_ref, y_ref, sem = refs
        sc_sh = None
    base = pl.program_id(0) * tm

    def issue(t, c):
        for k in range(TOP_K):
            _row_copy(ys_hbm, pos_ref[(base + t) * TOP_K + k], stage_ref.at[k], t, sem.at[k], chunks).start()
        return c

    lax.fori_loop(0, tm, issue, 0)

    def wait(t, c):
        for k in range(TOP_K):
            _row_copy(ys_hbm, 0, stage_ref.at[k], t, sem.at[k], chunks).wait()
        return c

    lax.fori_loop(0, tm, wait, 0)
    w1 = wt_ref[:, 0:1]
    w2 = wt_ref[:, 1:2]
    for c in range(chunks):
        y_ref[:, c * LANES:(c + 1) * LANES] = (stage_ref[0, pl.ds(c, tm, stride=chunks), :] * w1
                                               + stage_ref[1, pl.ds(c, tm, stride=chunks), :] * w2)
    z = alpha * x_ref[...] + (1.0 + g_ref[0]) * y_ref[...]
    xn, u = _ln_modulate(z, lg_ref[0], lb_ref[0], sc_sh)
    xo_ref[...] = xn
    if emit_u:
        uo_ref[...] = u.astype(uo_ref.dtype)


def _moe_combine_ln(ys_lin, pos, wt, x, ada_rows, ln_g, ln_b, *, alpha, seq, layer, comp_g, nxt, u_dtype):
    t, d = x.shape
    tm = 256
    tpb = seq // tm
    chunks = d // LANES
    row = lambda l, comp: pl.BlockSpec((1, 1, d), lambda i, p: (_ada_row(l, i // tpb, comp), 0, 0))
    tile = pl.BlockSpec((tm, d), lambda i, p: (i, 0))
    lnp = pl.BlockSpec((1, 1, d), lambda i, p: (layer, 0, 0))
    in_specs = [pl.BlockSpec(memory_space=pl.ANY), pl.BlockSpec((tm, LANES), lambda i, p: (i, 0)), tile,
                row(layer, comp_g), lnp, lnp]
    args = [ys_lin, wt, x, ada_rows, ln_g, ln_b]
    out_specs = [tile]
    out_shape = [jax.ShapeDtypeStruct((t, d), F32)]
    if nxt is not None:
        in_specs += [row(nxt[0], nxt[1]), row(nxt[0], nxt[2])]
        args += [ada_rows, ada_rows]
        out_specs.append(tile)
        out_shape.append(jax.ShapeDtypeStruct((t, d), u_dtype))
    outs = pl.pallas_call(
        functools.partial(_moe_combine_ln_kernel, alpha=alpha, emit_u=nxt is not None, tm=tm, chunks=chunks),
        grid_spec=pltpu.PrefetchScalarGridSpec(
            num_scalar_prefetch=1,
            grid=(t // tm,),
            in_specs=in_specs, out_specs=out_specs,
            scratch_shapes=[pltpu.VMEM((TOP_K, tm * chunks, LANES), F32), pltpu.VMEM((tm, d), F32),
                            pltpu.SemaphoreType.DMA((TOP_K,))]),
        out_shape=out_shape,
        compiler_params=_cparams(1, 48),
        name="moe_combine_ln",
    )(pos, *args)
    return (outs[0], outs[1]) if nxt is not None else (outs[0], None)


def _rope_cos_sin(positions, dim):
    inv_freq = ROPE_THETA ** (-jnp.arange(0, dim, 2, dtype=F32) / dim)
    ang = positions.astype(F32).reshape(-1)[:, None] * inv_freq
    return jnp.cos(ang), jnp.sin(ang)


def _mla_rope_tables(positions):
    cos, sin = _rope_cos_sin(positions, MLA_ROPE)
    z = jnp.zeros_like(cos)
    return jnp.concatenate([cos, z, cos, z], axis=1), jnp.concatenate([-sin, z, sin, z], axis=1)


def _moba_rope_tables(positions):
    cos, sin = _rope_cos_sin(positions, MOBA_ROT_DIM)
    t, half = cos.shape
    rest = LANES - 2 * half
    c = jnp.concatenate([cos, cos, jnp.ones((t, rest), F32)], axis=1)
    s1 = jnp.concatenate([jnp.zeros((t, half), F32), sin, jnp.zeros((t, rest), F32)], axis=1)
    s2 = jnp.concatenate([-sin, jnp.zeros((t, half + rest), F32)], axis=1)
    return c, s1, s2


def _spread_rope_cols(w_rope):
    half = MLA_ROPE // 2
    z = jnp.zeros(w_rope.shape[:-1] + (LANES // 2 - half,), w_rope.dtype)
    return jnp.concatenate([w_rope[..., :half], z, w_rope[..., half:], z], axis=-1)


def _mla_weights(w_down, w_uq, ql, kvl):
    k = w_uq.shape[0]
    heads = w_uq.shape[1] // (MLA_NOPE + MLA_ROPE)
    wd = jnp.concatenate([w_down[:, :ql + kvl], _spread_rope_cols(w_down[:, ql + kvl:])], axis=1).astype(BF16)
    wq = w_uq.reshape(k, heads, MLA_NOPE + MLA_ROPE)
    wq = jnp.concatenate([wq[..., :MLA_NOPE], _spread_rope_cols(wq[..., MLA_NOPE:])], axis=-1)
    return wd, wq.reshape(k, heads * MLA_HEAD_PAD).astype(BF16), heads


def kernel(x, c, positions, w_ada, b_ada, ln_mix_g, ln_mix_b, ln_ffn_g, ln_ffn_b, mla_w_down, mla_q_norm,
           mla_kv_norm, mla_w_uq, mla_w_ukv, mla_w_o, moba_w_qkv, moba_w_o, ffn_w_gate_up, ffn_w_down,
           moe_w_router, moe_w_gate_up, moe_w_down):
    batch, seq, d = x.shape
    depth = w_ada.shape[0]
    t = batch * seq
    alpha = (2.0 * depth) ** 0.25
    ql = mla_q_norm.shape[1]
    kvl = mla_kv_norm.shape[1]
    n_exp = moe_w_router.shape[2]
    chunks = d // LANES

    ada_rows = _ada_all(c, w_ada, b_ada)
    cos_mla, sin_mla = _mla_rope_tables(positions)
    c_moba, s1_moba, s2_moba = _moba_rope_tables(positions)
    ln3 = lambda p: p.reshape(depth, 1, d)
    ln_mix_g, ln_mix_b, ln_ffn_g, ln_ffn_b = ln3(ln_mix_g), ln3(ln_mix_b), ln3(ln_ffn_g), ln3(ln_ffn_b)
    q_norm3 = mla_q_norm.reshape(-1, 1, ql)
    kv_norm3 = mla_kv_norm.reshape(-1, 1, kvl)

    xf = x.reshape(t, d)
    u = _modulate(xf, ada_rows, 0, seq, 1, 0, BF16)
    for l in range(depth):
        j = l // 2
        moe_layer = l % 2 == 1
        if l % 2 == 0:
            wd_p, wq_p, heads = _mla_weights(mla_w_down[j], mla_w_uq[j], ql, kvl)
            cq, ckv, kr = _mla_down(u, wd_p, q_norm3, kv_norm3, j, cos_mla, sin_mla, ql, kvl)
            q = _mla_qup(cq, wq_p, cos_mla, sin_mla, (MLA_NOPE + MLA_ROPE) ** -0.5)
            kv = _matmul(ckv, mla_w_ukv, j, BF16, tm=1024, tn=1024)
            o = _mla_attention(q, kv, kr, batch, seq, heads)
            y = _matmul(o, mla_w_o, j, F32, tm=1024, tn=512)
        else:
            heads = moba_w_qkv.shape[2] // (3 * MOBA_HEAD_DIM)
            qkv = _moba_qkv(u, moba_w_qkv, j, c_moba, s1_moba, s2_moba, MOBA_HEAD_DIM ** -0.5)
            o = _moba_attention(qkv, batch, seq, heads)
            y = _matmul(o, moba_w_o, j, F32, tm=1024, tn=512)
        xf, u = _resid_ln(xf, y, ada_rows, ln_mix_g, ln_mix_b, alpha=alpha, seq=seq, layer=l, comp_g=2,
                          nxt=(l, 4, 3), u_dtype=F32 if moe_layer else BF16)
        nxt = (l + 1, 1, 0) if l + 1 < depth else None
        if not moe_layer:
            y = _dense_ffn(u, ffn_w_gate_up, ffn_w_down, j)
            xf, u = _resid_ln(xf, y, ada_rows, ln_ffn_g, ln_ffn_b, alpha=alpha, seq=seq, layer=l, comp_g=5,
                              nxt=nxt, u_dtype=BF16)
        else:
            w_router_pad = jnp.zeros((d, LANES), F32).at[:, :n_exp].set(moe_w_router[j])
            idx, wt = _router(u, w_router_pad, n_exp)
            pos, row_token, item_e, item_row0, item_nsub, p_rows = _route_metadata(idx[:, :TOP_K], n_exp)
            xs = _dispatch(u.reshape(t * chunks, LANES), row_token, d)
            ys = _moe_ffn(xs, moe_w_gate_up, moe_w_down, j, item_e, item_row0, item_nsub)
            xf, u = _moe_combine_ln(ys.reshape(p_rows * chunks, LANES), pos, wt, xf, ada_rows, ln_ffn_g, ln_ffn_b,
                                    alpha=alpha, seq=seq, layer=l, comp_g=5, nxt=nxt, u_dtype=BF16)
    return xf.reshape(batch, seq, d)
```

```python
import functools

import jax
import jax.numpy as jnp
from jax import lax
from jax.experimental import pallas as pl
from jax.experimental.pallas import tpu as pltpu

F32 = jnp.float32
BF16 = jnp.bfloat16

ROPE_THETA = 500000.0
LN_EPS = 1e-5
RMS_EPS = 1e-6
MLA_NOPE = 128
MLA_ROPE = 64
MLA_V = 128
MOBA_HEAD_DIM = 128
MOBA_ROT_DIM = 32
MOBA_BLOCK = 256
MOBA_TOPK = 3
TOP_K = 2

LANES = 128
SUBLANES = 8
MLA_HEAD_PAD = 2 * LANES

ADA_BATCH_PAD = SUBLANES
MOE_ROW_BLOCK = 256
MOE_SUPER_BLOCKS = 10


def _cparams(n_axes, vmem_mb):
    return pltpu.CompilerParams(dimension_semantics=("arbitrary",) * n_axes,
                                vmem_limit_bytes=vmem_mb * 1024 * 1024)


def _split_bf16(x):
    hi = x.astype(BF16)
    return hi, (x - hi.astype(F32)).astype(BF16)


def _ada_row(layer, batch, comp):
    return (layer * ADA_BATCH_PAD + batch) * 6 + comp


def _ada_kernel(c_ref, w_ref, b_ref, o_ref):
    c = c_ref[...]
    ca = (c * jax.nn.sigmoid(c)).astype(BF16)
    o_ref[...] = jnp.dot(ca, w_ref[...].astype(BF16), preferred_element_type=F32) + b_ref[...]


def _ada_all(c, w_ada, b_ada):
    depth, d, n6 = w_ada.shape
    b = c.shape[0]
    c_pad = jnp.zeros((ADA_BATCH_PAD, d), F32).at[:b].set(c)
    tn = 1024
    out = pl.pallas_call(
        _ada_kernel,
        grid=(depth, n6 // tn),
        in_specs=[pl.BlockSpec((ADA_BATCH_PAD, d), lambda l, j: (0, 0)),
                  pl.BlockSpec((None, d, tn), lambda l, j: (l, 0, j)),
                  pl.BlockSpec((None, 1, tn), lambda l, j: (l, 0, j))],
        out_specs=pl.BlockSpec((None, ADA_BATCH_PAD, tn), lambda l, j: (l, 0, j)),
        out_shape=jax.ShapeDtypeStruct((depth, ADA_BATCH_PAD, n6), F32),
        compiler_params=_cparams(2, 40),
        name="ada",
    )(c_pad, w_ada, b_ada.reshape(depth, 1, n6))
    return out.reshape(depth * ADA_BATCH_PAD * 6, 1, d)


def _modulate_kernel(x_ref, sc_ref, sh_ref, u_ref):
    u_ref[...] = (x_ref[...] * (1.0 + sc_ref[0]) + sh_ref[0]).astype(u_ref.dtype)


def _modulate(x, ada_rows, layer, seq, comp_sc, comp_sh, out_dtype):
    t, d = x.shape
    tm = 512
    tpb = seq // tm
    row = lambda comp: pl.BlockSpec((1, 1, d), lambda i: (_ada_row(layer, i // tpb, comp), 0, 0))
    return pl.pallas_call(
        _modulate_kernel,
        grid=(t // tm,),
        in_specs=[pl.BlockSpec((tm, d), lambda i: (i, 0)), row(comp_sc), row(comp_sh)],
        out_specs=pl.BlockSpec((tm, d), lambda i: (i, 0)),
        out_shape=jax.ShapeDtypeStruct((t, d), out_dtype),
        compiler_params=_cparams(1, 40),
        name="modulate",
    )(x, ada_rows, ada_rows)


def _ln_modulate(z, lg, lb, sc_sh):
    mu = jnp.mean(z, axis=-1, keepdims=True)
    zc = z - mu
    var = jnp.mean(zc * zc, axis=-1, keepdims=True)
    xn = zc * lax.rsqrt(var + LN_EPS) * lg + lb
    if sc_sh is None:
        return xn, None
    sc, sh = sc_sh
    return xn, xn * (1.0 + sc) + sh


def _store_linear(dst_ref, val, chunks):
    rows = val.shape[0]
    for c in range(chunks):
        dst_ref[pl.ds(c, rows, stride=chunks), :] = val[:, c * LANES:(c + 1) * LANES]


def _top2_route(logits, n_exp):
    lane = lax.broadcasted_iota(jnp.int32, logits.shape, 1)
    lg = jnp.where(lane < n_exp, logits, -jnp.inf)
    m1 = jnp.max(lg, axis=-1, keepdims=True)
    i1 = jnp.min(jnp.where(lg == m1, lane, LANES), axis=-1, keepdims=True)
    lg2 = jnp.where(lane == i1, -jnp.inf, lg)
    m2 = jnp.max(lg2, axis=-1, keepdims=True)
    i2 = jnp.min(jnp.where(lg2 == m2, lane, LANES), axis=-1, keepdims=True)
    e = jnp.exp(m2 - m1)
    w1 = 1.0 / (1.0 + e)
    w2 = e / (1.0 + e)
    idx = jnp.where(lane == 0, i1, jnp.where(lane == 1, i2, 0))
    wt = jnp.where(lane == 0, w1, jnp.where(lane == 1, w2, 0.0))
    return idx, wt


def _resid_ln_kernel(*refs, alpha, mode, n_exp, chunks):
    if mode == "last":
        x_ref, y_ref, g_ref, lg_ref, lb_ref, xo_ref = refs
        sc_sh = None
    elif mode == "next":
        x_ref, y_ref, g_ref, lg_ref, lb_ref, sc_ref, sh_ref, xo_ref, uo_ref = refs
        sc_sh = (sc_ref[0], sh_ref[0])
    else:
        x_ref, y_ref, g_ref, lg_ref, lb_ref, sc_ref, sh_ref, wr_ref, xo_ref, uo_ref, idx_ref, wt_ref = refs
        sc_sh = (sc_ref[0], sh_ref[0])
    z = alpha * x_ref[...] + (1.0 + g_ref[0]) * y_ref[...].astype(F32)
    xn, u = _ln_modulate(z, lg_ref[0], lb_ref[0], sc_sh)
    xo_ref[...] = xn
    if mode == "next":
        uo_ref[...] = u.astype(uo_ref.dtype)
    elif mode == "route":
        _store_linear(uo_ref, u, chunks)
        uh, ul = _split_bf16(u)
        wh, wl = _split_bf16(wr_ref[...])
        dot = lambda a, b: jnp.dot(a, b, preferred_element_type=F32)
        idx, wt = _top2_route(dot(uh, wh) + dot(uh, wl) + dot(ul, wh), n_exp)
        idx_ref[...] = idx
        wt_ref[...] = wt


def _resid_ln(x, y, ada_rows, ln_g, ln_b, *, alpha, seq, layer, comp_g, nxt, w_router_pad=None, n_exp=0):
    t, d = x.shape
    tm = 256
    tpb = seq // tm
    chunks = d // LANES
    mode = "last" if nxt is None else ("route" if w_router_pad is not None else "next")
    row = lambda l, comp: pl.BlockSpec((1, 1, d), lambda i: (_ada_row(l, i // tpb, comp), 0, 0))
    tile = pl.BlockSpec((tm, d), lambda i: (i, 0))
    lanes = pl.BlockSpec((tm, LANES), lambda i: (i, 0))
    lnp = pl.BlockSpec((1, 1, d), lambda i: (layer, 0, 0))
    in_specs = [tile, tile, row(layer, comp_g), lnp, lnp]
    args = [x, y, ada_rows, ln_g, ln_b]
    out_specs = [tile]
    out_shape = [jax.ShapeDtypeStruct((t, d), F32)]
    if mode != "last":
        in_specs += [row(nxt[0], nxt[1]), row(nxt[0], nxt[2])]
        args += [ada_rows, ada_rows]
    if mode == "next":
        out_specs.append(tile)
        out_shape.append(jax.ShapeDtypeStruct((t, d), BF16))
    elif mode == "route":
        in_specs.append(pl.BlockSpec((d, LANES), lambda i: (0, 0)))
        args.append(w_router_pad)
        out_specs += [pl.BlockSpec((tm * chunks, LANES), lambda i: (i, 0)), lanes, lanes]
        out_shape += [jax.ShapeDtypeStruct((t * chunks, LANES), F32),
                      jax.ShapeDtypeStruct((t, LANES), jnp.int32), jax.ShapeDtypeStruct((t, LANES), F32)]
    return pl.pallas_call(
        functools.partial(_resid_ln_kernel, alpha=alpha, mode=mode, n_exp=n_exp, chunks=chunks),
        grid=(t // tm,),
        in_specs=in_specs, out_specs=out_specs, out_shape=out_shape,
        compiler_params=_cparams(1, 48),
        name="resid_ln_" + mode,
    )(*args)


def _mm_kernel(a_ref, w_ref, o_ref, wb_ref):
    @pl.when(pl.program_id(1) == 0)
    def _():
        wb_ref[...] = w_ref[...].astype(BF16)
    o_ref[...] = jnp.dot(a_ref[...], wb_ref[...], preferred_element_type=F32).astype(o_ref.dtype)


def _matmul(a, w_stack, layer, out_dtype, *, tm, tn):
    m, k = a.shape
    n = w_stack.shape[2]
    return pl.pallas_call(
        _mm_kernel,
        grid=(n // tn, m // tm),
        in_specs=[pl.BlockSpec((tm, k), lambda j, i: (i, 0)),
                  pl.BlockSpec((None, k, tn), lambda j, i: (layer, 0, j))],
        out_specs=pl.BlockSpec((tm, tn), lambda j, i: (i, j)),
        out_shape=jax.ShapeDtypeStruct((m, n), out_dtype),
        scratch_shapes=[pltpu.VMEM((k, tn), BF16)],
        compiler_params=_cparams(2, 48),
        name="matmul",
    )(a, w_stack)


def _rms(x, g):
    ms = jnp.mean(x * x, axis=-1, keepdims=True)
    return x * lax.rsqrt(ms + RMS_EPS) * g


def _mla_down_kernel(u_ref, w_ref, qn_ref, kvn_ref, c_ref, s_ref, cq_ref, ckv_ref, kr_ref, *, ql, kvl):
    acc = jnp.dot(u_ref[...], w_ref[...], preferred_element_type=F32)
    cq_ref[...] = _rms(acc[:, :ql], qn_ref[0]).astype(BF16)
    ckv_ref[...] = _rms(acc[:, ql:ql + kvl], kvn_ref[0]).astype(BF16)
    xr = acc[:, ql + kvl:]
    kr_ref[...] = (xr * c_ref[...] + pltpu.roll(xr, LANES // 2, 1) * s_ref[...]).astype(BF16)


def _mla_down(u, w_perm, q_norm, kv_norm, layer, cos_t, sin_t, ql, kvl):
    t, d = u.shape
    n = w_perm.shape[1]
    tm = 512
    nrm = lambda width: pl.BlockSpec((1, 1, width), lambda i: (layer, 0, 0))
    rows = lambda width: pl.BlockSpec((tm, width), lambda i: (i, 0))
    return pl.pallas_call(
        functools.partial(_mla_down_kernel, ql=ql, kvl=kvl),
        grid=(t // tm,),
        in_specs=[rows(d), pl.BlockSpec((d, n), lambda i: (0, 0)), nrm(ql), nrm(kvl), rows(LANES), rows(LANES)],
        out_specs=[rows(ql), rows(kvl), rows(LANES)],
        out_shape=[jax.ShapeDtypeStruct((t, ql), BF16), jax.ShapeDtypeStruct((t, kvl), BF16),
                   jax.ShapeDtypeStruct((t, LANES), BF16)],
        compiler_params=_cparams(1, 48),
        name="mla_down",
    )(u, w_perm, q_norm, kv_norm, cos_t, sin_t)


def _mla_qup_kernel(a_ref, w_ref, c_ref, s_ref, o_ref, *, scale, heads):
    acc = jnp.dot(a_ref[...], w_ref[...], preferred_element_type=F32)
    c = c_ref[...]
    s = s_ref[...]
    for h in range(heads):
        b0 = h * MLA_HEAD_PAD
        o_ref[:, b0:b0 + LANES] = (acc[:, b0:b0 + LANES] * scale).astype(BF16)
        xr = acc[:, b0 + LANES:b0 + MLA_HEAD_PAD]
        o_ref[:, b0 + LANES:b0 + MLA_HEAD_PAD] = (
            (xr * c + pltpu.roll(xr, LANES // 2, 1) * s) * scale).astype(BF16)


def _mla_qup(cq, w_perm, cos_t, sin_t, scale):
    t, k = cq.shape
    n = w_perm.shape[1]
    tm, tn = 1024, 1024
    rows = pl.BlockSpec((tm, LANES), lambda j, i: (i, 0))
    return pl.pallas_call(
        functools.partial(_mla_qup_kernel, scale=scale, heads=tn // MLA_HEAD_PAD),
        grid=(n // tn, t // tm),
        in_specs=[pl.BlockSpec((tm, k), lambda j, i: (i, 0)), pl.BlockSpec((k, tn), lambda j, i: (0, j)),
                  rows, rows],
        out_specs=pl.BlockSpec((tm, tn), lambda j, i: (i, j)),
        out_shape=jax.ShapeDtypeStruct((t, n), BF16),
        compiler_params=_cparams(2, 48),
        name="mla_qup",
    )(cq, w_perm, cos_t, sin_t)


def _moba_qkv_kernel(u_ref, w_ref, c_ref, s1_ref, s2_ref, o_ref, wb_ref, *, scale, tiles_per_sec, heads):
    j = pl.program_id(0)

    @pl.when(pl.program_id(1) == 0)
    def _():
        wb_ref[...] = w_ref[...].astype(BF16)

    acc = jnp.dot(u_ref[...], wb_ref[...], preferred_element_type=F32)
    sec = j // tiles_per_sec

    @pl.when(sec == 2)
    def _():
        o_ref[...] = acc.astype(BF16)

    @pl.when(sec < 2)
    def _():
        c = c_ref[...]
        s1 = s1_ref[...]
        s2 = s2_ref[...]
        mul = jnp.where(sec == 0, scale, 1.0).astype(F32)
        half = MOBA_ROT_DIM // 2
        for h in range(heads):
            x = acc[:, h * LANES:(h + 1) * LANES]
            r = x * c + pltpu.roll(x, half, 1) * s1 + pltpu.roll(x, LANES - half, 1) * s2
            o_ref[:, h * LANES:(h + 1) * LANES] = (r * mul).astype(BF16)


def _moba_qkv(u, w_stack, layer, c_t, s1_t, s2_t, scale):
    t, k = u.shape
    n = w_stack.shape[2]
    tm, tn = 1024, 512
    rows = pl.BlockSpec((tm, LANES), lambda j, i: (i, 0))
    return pl.pallas_call(
        functools.partial(_moba_qkv_kernel, scale=scale, tiles_per_sec=(n // 3) // tn, heads=tn // LANES),
        grid=(n // tn, t // tm),
        in_specs=[pl.BlockSpec((tm, k), lambda j, i: (i, 0)),
                  pl.BlockSpec((None, k, tn), lambda j, i: (layer, 0, j)), rows, rows, rows],
        out_specs=pl.BlockSpec((tm, tn), lambda j, i: (i, j)),
        out_shape=jax.ShapeDtypeStruct((t, n), BF16),
        scratch_shapes=[pltpu.VMEM((k, tn), BF16)],
        compiler_params=_cparams(2, 48),
        name="moba_qkv",
    )(u, w_stack, c_t, s1_t, s2_t)


_NT = (((1,), (1,)), ((), ()))


def _softmax_pv(s, v):
    m = jnp.max(s, axis=-1, keepdims=True)
    p = jnp.exp(s - m)
    l = jnp.sum(p, axis=-1, keepdims=True)
    return jnp.dot(p.astype(BF16), v, preferred_element_type=F32) / l


def _mla_attn_kernel(q_ref, kv_ref, kr_ref, o_ref, kfull_ref, *, tq):
    seq = q_ref.shape[0]
    kfull_ref[:, :LANES] = kv_ref[:, :LANES]
    kfull_ref[:, LANES:] = kr_ref[...]
    row = lax.broadcasted_iota(jnp.int32, (tq, tq), 0)
    col = lax.broadcasted_iota(jnp.int32, (tq, tq), 1)
    causal = col <= row
    for n in range(seq // tq):
        q = q_ref[n * tq:(n + 1) * tq, :]
        width = (n + 1) * tq
        s = lax.dot_general(q, kfull_ref[0:width, :], _NT, preferred_element_type=F32)
        diag = jnp.where(causal, s[:, n * tq:], -jnp.inf)
        s = jnp.concatenate([s[:, :n * tq], diag], axis=1) if n else diag
        o_ref[n * tq:(n + 1) * tq, :] = _softmax_pv(s, kv_ref[0:width, LANES:]).astype(o_ref.dtype)


def _mla_attention(q, kv, kr, batch, seq, heads):
    t = q.shape[0]
    return pl.pallas_call(
        functools.partial(_mla_attn_kernel, tq=512),
        grid=(batch, heads),
        in_specs=[pl.BlockSpec((seq, MLA_HEAD_PAD), lambda b, h: (b, h)),
                  pl.BlockSpec((seq, MLA_NOPE + MLA_V), lambda b, h: (b, h)),
                  pl.BlockSpec((seq, LANES), lambda b, h: (b, 0))],
        out_specs=pl.BlockSpec((seq, MLA_V), lambda b, h: (b, h)),
        out_shape=jax.ShapeDtypeStruct((t, heads * MLA_V), BF16),
        scratch_shapes=[pltpu.VMEM((seq, MLA_HEAD_PAD), BF16)],
        compiler_params=_cparams(2, 48),
        name="mla_attn",
    )(q, kv, kr)


def _moba_attn_kernel(q_ref, k_ref, v_ref, o_ref, *, nb):
    blk = MOBA_BLOCK
    seq = k_ref.shape[0]
    r = lax.broadcasted_iota(jnp.int32, (LANES, seq), 0)
    c = lax.broadcasted_iota(jnp.int32, (LANES, seq), 1)
    ind = jnp.where(c // blk == r, 1.0 / blk, 0.0).astype(BF16)
    km = jnp.dot(ind, k_ref[...], preferred_element_type=F32)
    kmh, kml = _split_bf16(km)
    row = lax.broadcasted_iota(jnp.int32, (blk, blk), 0)
    col = lax.broadcasted_iota(jnp.int32, (blk, blk), 1)
    causal = col <= row
    lane = lax.broadcasted_iota(jnp.int32, (blk, LANES), 1)
    for n in range(nb):
        q = q_ref[n * blk:(n + 1) * blk, :]
        width = (n + 1) * blk
        s = lax.dot_general(q, k_ref[0:width, :], _NT, preferred_element_type=F32)
        parts = []
        if n > MOBA_TOPK:
            gate = (lax.dot_general(q, kmh, _NT, preferred_element_type=F32)
                    + lax.dot_general(q, kml, _NT, preferred_element_type=F32))
            past = lane < n
        for j in range(n):
            sj = s[:, j * blk:(j + 1) * blk]
            if n > MOBA_TOPK:
                gj = gate[:, j:j + 1]
                beats = past & ((gate > gj) | ((gate == gj) & (lane < j)))
                n_beats = jnp.sum(beats.astype(F32), axis=-1, keepdims=True)
                sj = jnp.where(n_beats < MOBA_TOPK, sj, -jnp.inf)
            parts.append(sj)
        parts.append(jnp.where(causal, s[:, n * blk:], -jnp.inf))
        s = jnp.concatenate(parts, axis=1) if n else parts[0]
        o_ref[n * blk:(n + 1) * blk, :] = _softmax_pv(s, v_ref[0:width, :]).astype(o_ref.dtype)


def _moba_attention(qkv, batch, seq, heads):
    t = qkv.shape[0]
    d = MOBA_HEAD_DIM
    return pl.pallas_call(
        functools.partial(_moba_attn_kernel, nb=seq // MOBA_BLOCK),
        grid=(batch, heads),
        in_specs=[pl.BlockSpec((seq, d), lambda b, h: (b, h)),
                  pl.BlockSpec((seq, d), lambda b, h: (b, heads + h)),
                  pl.BlockSpec((seq, d), lambda b, h: (b, 2 * heads + h))],
        out_specs=pl.BlockSpec((seq, d), lambda b, h: (b, h)),
        out_shape=jax.ShapeDtypeStruct((t, heads * d), BF16),
        compiler_params=_cparams(2, 48),
        name="moba_attn",
    )(qkv, qkv, qkv)


def _swiglu_partial(x, wg, wu, wd):
    g = jnp.dot(x, wg, preferred_element_type=F32)
    u = jnp.dot(x, wu, preferred_element_type=F32)
    a = (g * jax.nn.sigmoid(g) * u).astype(BF16)
    return jnp.dot(a, wd, preferred_element_type=F32)


def _ffn_kernel(u_ref, wg_ref, wu_ref, wd_ref, o_ref):
    f = pl.program_id(1)
    y = _swiglu_partial(u_ref[...], wg_ref[...].astype(BF16), wu_ref[...].astype(BF16),
                        wd_ref[...].astype(BF16))

    @pl.when(f == 0)
    def _():
        o_ref[...] = y

    @pl.when(f > 0)
    def _():
        o_ref[...] += y


def _dense_ffn(u, w_gate_up, w_down, layer):
    t, d = u.shape
    dff = w_down.shape[1]
    tm, tf = 1024, 256
    nf = dff // tf
    return pl.pallas_call(
        _ffn_kernel,
        grid=(t // tm, nf),
        in_specs=[pl.BlockSpec((tm, d), lambda i, f: (i, 0)),
                  pl.BlockSpec((None, d, tf), lambda i, f: (layer, 0, f)),
                  pl.BlockSpec((None, d, tf), lambda i, f: (layer, 0, nf + f)),
                  pl.BlockSpec((None, tf, d), lambda i, f: (layer, f, 0))],
        out_specs=pl.BlockSpec((tm, d), lambda i, f: (i, 0)),
        out_shape=jax.ShapeDtypeStruct((t, d), F32),
        compiler_params=_cparams(2, 56),
        name="dense_ffn",
    )(u, w_gate_up, w_gate_up, w_down)


def _route_metadata(idx2, n_exp):
    rb, st = MOE_ROW_BLOCK, MOE_SUPER_BLOCKS
    t = idx2.shape[0]
    a = t * TOP_K
    n_items = (a // rb + n_exp - 1 + n_exp * (st - 1)) // st
    e_flat = idx2.reshape(a)
    onehot = (e_flat[:, None] == jnp.arange(n_exp, dtype=jnp.int32)[None, :]).astype(jnp.int32)
    csum = jnp.cumsum(onehot, axis=0)
    rank = jnp.sum((csum - onehot) * onehot, axis=1)
    counts = csum[-1]
    nsub = (counts + rb - 1) // rb
    sub_start = jnp.cumsum(nsub) - nsub
    dest = jnp.sum(onehot * (sub_start * rb)[None, :], axis=1) + rank
    p_rows = (a // rb + n_exp) * rb
    token_flat = jnp.arange(a, dtype=jnp.int32) // TOP_K
    row_token = jnp.zeros((p_rows,), jnp.int32).at[dest].set(token_flat)
    n_it = (nsub + st - 1) // st
    it_end = jnp.cumsum(n_it)
    it_start = it_end - n_it
    w = jnp.arange(n_items, dtype=jnp.int32)
    e_w = jnp.sum((it_end[None, :] <= w[:, None]).astype(jnp.int32), axis=1)
    active = e_w < n_exp
    e_c = jnp.minimum(e_w, n_exp - 1)
    local = w - it_start[e_c]
    item_nsub = jnp.where(active, jnp.clip(nsub[e_c] - local * st, 0, st), 0)
    item_row0 = jnp.where(active, (sub_start[e_c] + local * st) * rb, 0)
    e_last = jnp.max(jnp.where(n_it > 0, jnp.arange(n_exp, dtype=jnp.int32), 0))
    item_e = jnp.where(active, e_c, e_last)
    item_row0 = jnp.concatenate([item_row0, jnp.sum(nsub, keepdims=True) * rb])
    return (dest.astype(jnp.int32), row_token, item_e.astype(jnp.int32), item_row0.astype(jnp.int32),
            item_nsub.astype(jnp.int32), p_rows)


def _row_copy(src_hbm, src_row, dst, dst_row, sem, chunks):
    at = lambda r: r * chunks if isinstance(r, int) else pl.multiple_of(r * chunks, chunks)
    return pltpu.make_async_copy(src_hbm.at[pl.ds(at(src_row), chunks), :],
                                 dst.at[pl.ds(at(dst_row), chunks), :], sem)


def _dispatch_kernel(tok_ref, u_hbm, o_ref, stage_ref, sem, *, rb, chunks):
    i = pl.program_id(0)

    def issue(step, slot):
        def body(r, c):
            _row_copy(u_hbm, tok_ref[step * rb + r], stage_ref.at[slot], r, sem.at[slot], chunks).start()
            return c
        lax.fori_loop(0, rb, body, 0, unroll=8)

    @pl.when(i == 0)
    def _():
        issue(0, 0)

    @pl.when(i + 1 < pl.num_programs(0))
    def _():
        issue(i + 1, (i + 1) % 2)

    slot = i % 2
    pltpu.make_async_copy(u_hbm.at[pl.ds(0, rb * chunks), :], stage_ref.at[slot], sem.at[slot]).wait()
    for c in range(chunks):
        o_ref[:, c * LANES:(c + 1) * LANES] = stage_ref[slot, pl.ds(c, rb, stride=chunks), :].astype(o_ref.dtype)


def _dispatch(u_lin, row_token, d):
    p_rows = row_token.shape[0]
    rb = MOE_ROW_BLOCK
    chunks = d // LANES
    return pl.pallas_call(
        functools.partial(_dispatch_kernel, rb=rb, chunks=chunks),
        grid_spec=pltpu.PrefetchScalarGridSpec(
            num_scalar_prefetch=1,
            grid=(p_rows // rb,),
            in_specs=[pl.BlockSpec(memory_space=pl.ANY)],
            out_specs=pl.BlockSpec((rb, d), lambda i, tok: (i, 0)),
            scratch_shapes=[pltpu.VMEM((2, rb * chunks, LANES), F32), pltpu.SemaphoreType.DMA((2,))]),
        out_shape=jax.ShapeDtypeStruct((p_rows, d), BF16),
        compiler_params=_cparams(1, 40),
        name="moe_dispatch",
    )(row_token, u_lin)


def _moe_ffn_kernel(e_ref, row0_ref, nsub_ref, xs_hbm, wg_ref, wu_ref, wd_ref, ys_hbm,
                    x_ref, acc_ref, wgb_ref, wub_ref, wdb_ref, stage_ref, sem, *, rb, nf, n_items, chunks):
    w = pl.program_id(0)
    f = pl.program_id(1)
    nsub = nsub_ref[w]
    row0 = row0_ref[w]

    def for_range(n, fn):
        def body(r, c):
            fn(r)
            return c
        lax.fori_loop(0, n, body, 0)

    def load(r):
        return pltpu.make_async_copy(xs_hbm.at[pl.ds(pl.multiple_of(row0 + r * rb, rb), rb), :],
                                     x_ref.at[pl.ds(pl.multiple_of(r * rb, rb), rb), :], sem.at[0])

    def store(row, slot):
        return pltpu.make_async_copy(
            stage_ref.at[slot], ys_hbm.at[pl.ds(pl.multiple_of(row * chunks, rb * chunks), rb * chunks), :],
            sem.at[1 + slot])

    @pl.when(f == 0)
    def _():
        for_range(nsub, lambda r: load(r).start())
        for_range(nsub, lambda r: load(r).wait())

    @pl.when(nsub > 0)
    def _():
        wgb_ref[...] = wg_ref[...].astype(BF16)
        wub_ref[...] = wu_ref[...].astype(BF16)
        wdb_ref[...] = wd_ref[...].astype(BF16)

        def chunk(start, size):
            rows = pl.ds(pl.multiple_of(start, rb), size)
            y = _swiglu_partial(x_ref[rows, :], wgb_ref[...], wub_ref[...], wdb_ref[...])

            @pl.when(f == 0)
            def _():
                acc_ref[rows, :] = y

            @pl.when(f > 0)
            def _():
                acc_ref[rows, :] += y

        for_range(nsub // 2, lambda p: chunk(p * (2 * rb), 2 * rb))

        @pl.when(nsub % 2 == 1)
        def _():
            chunk((nsub - 1) * rb, rb)

    @pl.when(f == nf - 1)
    def _():
        def emit(r):
            slot = r % 2

            @pl.when(r >= 2)
            def _():
                store(row0 + (r - 2) * rb, slot).wait()

            _store_linear(stage_ref.at[slot], acc_ref[pl.ds(pl.multiple_of(r * rb, rb), rb), :], chunks)
            store(row0 + r * rb, slot).start()

        for_range(nsub, emit)

        @pl.when(nsub >= 2)
        def _():
            store(row0, nsub % 2).wait()

        @pl.when(nsub >= 1)
        def _():
            store(row0, (nsub - 1) % 2).wait()

    @pl.when((f == nf - 1) & (w == n_items - 1))
    def _():
        used = row0_ref[n_items]
        n_tail = (ys_hbm.shape[0] // chunks - used) // rb
        stage_ref[0] = jnp.zeros(stage_ref.shape[1:], F32)
        for_range(n_tail, lambda r: store(used + r * rb, 0).start())
        for_range(n_tail, lambda r: store(used + r * rb, 0).wait())


def _moe_ffn(xs, w_gate_up, w_down, layer, item_e, item_row0, item_nsub):
    p_rows, d = xs.shape
    dff = w_down.shape[2]
    rb, st = MOE_ROW_BLOCK, MOE_SUPER_BLOCKS
    n_items = item_e.shape[0]
    chunks = d // LANES
    tf = 256
    nf = dff // tf

    def f_eff(f, nsub, w):
        return jnp.where(nsub[w] > 0, f, nf - 1)

    return pl.pallas_call(
        functools.partial(_moe_ffn_kernel, rb=rb, nf=nf, n_items=n_items, chunks=chunks),
        grid_spec=pltpu.PrefetchScalarGridSpec(
            num_scalar_prefetch=3,
            grid=(n_items, nf),
            in_specs=[pl.BlockSpec(memory_space=pl.ANY),
                      pl.BlockSpec((None, None, d, tf), lambda w, f, e, r0, ns: (layer, e[w], 0, f_eff(f, ns, w))),
                      pl.BlockSpec((None, None, d, tf),
                                   lambda w, f, e, r0, ns: (layer, e[w], 0, nf + f_eff(f, ns, w))),
                      pl.BlockSpec((None, None, tf, d), lambda w, f, e, r0, ns: (layer, e[w], f_eff(f, ns, w), 0))],
            out_specs=pl.BlockSpec(memory_space=pl.ANY),
            scratch_shapes=[pltpu.VMEM((st * rb, d), BF16), pltpu.VMEM((st * rb, d), F32),
                            pltpu.VMEM((d, tf), BF16), pltpu.VMEM((d, tf), BF16), pltpu.VMEM((tf, d), BF16),
                            pltpu.VMEM((2, rb * chunks, LANES), F32), pltpu.SemaphoreType.DMA((3,))]),
        out_shape=jax.ShapeDtypeStruct((p_rows * chunks, LANES), F32),
        compiler_params=_cparams(2, 58),
        name="moe_ffn",
    )(item_e, item_row0, item_nsub, xs, w_gate_up, w_gate_up, w_down)


def _moe_combine_ln_kernel(*refs, alpha, emit_u, tm, chunks):
    if emit_u:
        (pos_ref, ys_hbm, wt_ref, x_ref, g_ref, lg_ref, lb_ref, sc_ref, sh_ref, xo_ref, uo_ref,
         stage_ref, y_ref, sem) = refs
        sc_sh = (sc_ref[0], sh_ref[0])
    else:
        pos_ref, ys_hbm, wt_ref, x_ref, g_ref, lg_ref, lb_ref, xo_ref, stage_ref, y_ref, sem = refs
        sc_sh = None
    i = pl.program_id(0)

    def issue(step, slot):
        def body(t, c):
            for k in range(TOP_K):
                _row_copy(ys_hbm, pos_ref[(step * tm + t) * TOP_K + k], stage_ref.at[slot, k], t,
                          sem.at[slot, k], chunks).start()
            return c
        lax.fori_loop(0, tm, body, 0, unroll=4)

    @pl.when(i == 0)
    def _():
        issue(0, 0)

    @pl.when(i + 1 < pl.num_programs(0))
    def _():
        issue(i + 1, (i + 1) % 2)

    slot = i % 2
    for k in range(TOP_K):
        pltpu.make_async_copy(ys_hbm.at[pl.ds(0, tm * chunks), :], stage_ref.at[slot, k], sem.at[slot, k]).wait()
    w1 = wt_ref[:, 0:1]
    w2 = wt_ref[:, 1:2]
    for c in range(chunks):
        y_ref[:, c * LANES:(c + 1) * LANES] = (stage_ref[slot, 0, pl.ds(c, tm, stride=chunks), :] * w1
                                               + stage_ref[slot, 1, pl.ds(c, tm, stride=chunks), :] * w2)
    z = alpha * x_ref[...] + (1.0 + g_ref[0]) * y_ref[...]
    xn, u = _ln_modulate(z, lg_ref[0], lb_ref[0], sc_sh)
    xo_ref[...] = xn
    if emit_u:
        uo_ref[...] = u.astype(uo_ref.dtype)


def _moe_combine_ln(ys_lin, pos, wt, x, ada_rows, ln_g, ln_b, *, alpha, seq, layer, comp_g, nxt):
    t, d = x.shape
    tm = 256
    tpb = seq // tm
    chunks = d // LANES
    row = lambda l, comp: pl.BlockSpec((1, 1, d), lambda i, p: (_ada_row(l, i // tpb, comp), 0, 0))
    tile = pl.BlockSpec((tm, d), lambda i, p: (i, 0))
    lnp = pl.BlockSpec((1, 1, d), lambda i, p: (layer, 0, 0))
    in_specs = [pl.BlockSpec(memory_space=pl.ANY), pl.BlockSpec((tm, LANES), lambda i, p: (i, 0)), tile,
                row(layer, comp_g), lnp, lnp]
    args = [ys_lin, wt, x, ada_rows, ln_g, ln_b]
    out_specs = [tile]
    out_shape = [jax.ShapeDtypeStruct((t, d), F32)]
    if nxt is not None:
        in_specs += [row(nxt[0], nxt[1]), row(nxt[0], nxt[2])]
        args += [ada_rows, ada_rows]
        out_specs.append(tile)
        out_shape.append(jax.ShapeDtypeStruct((t, d), BF16))
    outs = pl.pallas_call(
        functools.partial(_moe_combine_ln_kernel, alpha=alpha, emit_u=nxt is not None, tm=tm, chunks=chunks),
        grid_spec=pltpu.PrefetchScalarGridSpec(
            num_scalar_prefetch=1,
            grid=(t // tm,),
            in_specs=in_specs, out_specs=out_specs,
            scratch_shapes=[pltpu.VMEM((2, TOP_K, tm * chunks, LANES), F32), pltpu.VMEM((tm, d), F32),
                            pltpu.SemaphoreType.DMA((2, TOP_K))]),
        out_shape=out_shape,
        compiler_params=_cparams(1, 48),
        name="moe_combine_ln",
    )(pos, *args)
    return (outs[0], outs[1]) if nxt is not None else (outs[0], None)


def _rope_cos_sin(positions, dim):
    inv_freq = ROPE_THETA ** (-jnp.arange(0, dim, 2, dtype=F32) / dim)
    ang = positions.astype(F32).reshape(-1)[:, None] * inv_freq
    return jnp.cos(ang), jnp.sin(ang)


def _mla_rope_tables(positions):
    cos, sin = _rope_cos_sin(positions, MLA_ROPE)
    z = jnp.zeros_like(cos)
    return jnp.concatenate([cos, z, cos, z], axis=1), jnp.concatenate([-sin, z, sin, z], axis=1)


def _moba_rope_tables(positions):
    cos, sin = _rope_cos_sin(positions, MOBA_ROT_DIM)
    t, half = cos.shape
    rest = LANES - 2 * half
    c = jnp.concatenate([cos, cos, jnp.ones((t, rest), F32)], axis=1)
    s1 = jnp.concatenate([jnp.zeros((t, half), F32), sin, jnp.zeros((t, rest), F32)], axis=1)
    s2 = jnp.concatenate([-sin, jnp.zeros((t, half + rest), F32)], axis=1)
    return c, s1, s2


def _spread_rope_cols(w_rope):
    half = MLA_ROPE // 2
    z = jnp.zeros(w_rope.shape[:-1] + (LANES // 2 - half,), w_rope.dtype)
    return jnp.concatenate([w_rope[..., :half], z, w_rope[..., half:], z], axis=-1)


def _mla_weights(w_down, w_uq, ql, kvl):
    k = w_uq.shape[0]
    heads = w_uq.shape[1] // (MLA_NOPE + MLA_ROPE)
    wd = jnp.concatenate([w_down[:, :ql + kvl], _spread_rope_cols(w_down[:, ql + kvl:])], axis=1).astype(BF16)
    wq = w_uq.reshape(k, heads, MLA_NOPE + MLA_ROPE)
    wq = jnp.concatenate([wq[..., :MLA_NOPE], _spread_rope_cols(wq[..., MLA_NOPE:])], axis=-1)
    return wd, wq.reshape(k, heads * MLA_HEAD_PAD).astype(BF16), heads


def kernel(x, c, positions, w_ada, b_ada, ln_mix_g, ln_mix_b, ln_ffn_g, ln_ffn_b, mla_w_down, mla_q_norm,
           mla_kv_norm, mla_w_uq, mla_w_ukv, mla_w_o, moba_w_qkv, moba_w_o, ffn_w_gate_up, ffn_w_down,
           moe_w_router, moe_w_gate_up, moe_w_down):
    batch, seq, d = x.shape
    depth = w_ada.shape[0]
    t = batch * seq
    alpha = (2.0 * depth) ** 0.25
    ql = mla_q_norm.shape[1]
    kvl = mla_kv_norm.shape[1]
    n_exp = moe_w_router.shape[2]

    ada_rows = _ada_all(c, w_ada, b_ada)
    cos_mla, sin_mla = _mla_rope_tables(positions)
    c_moba, s1_moba, s2_moba = _moba_rope_tables(positions)
    ln3 = lambda p: p.reshape(depth, 1, d)
    ln_mix_g, ln_mix_b, ln_ffn_g, ln_ffn_b = ln3(ln_mix_g), ln3(ln_mix_b), ln3(ln_ffn_g), ln3(ln_ffn_b)
    q_norm3 = mla_q_norm.reshape(-1, 1, ql)
    kv_norm3 = mla_kv_norm.reshape(-1, 1, kvl)

    xf = x.reshape(t, d)
    u = _modulate(xf, ada_rows, 0, seq, 1, 0, BF16)
    for l in range(depth):
        j = l // 2
        moe_layer = l % 2 == 1
        if l % 2 == 0:
            wd_p, wq_p, heads = _mla_weights(mla_w_down[j], mla_w_uq[j], ql, kvl)
            cq, ckv, kr = _mla_down(u, wd_p, q_norm3, kv_norm3, j, cos_mla, sin_mla, ql, kvl)
            q = _mla_qup(cq, wq_p, cos_mla, sin_mla, (MLA_NOPE + MLA_ROPE) ** -0.5)
            kv = _matmul(ckv, mla_w_ukv, j, BF16, tm=1024, tn=1024)
            o = _mla_attention(q, kv, kr, batch, seq, heads)
            y = _matmul(o, mla_w_o, j, F32, tm=1024, tn=512)
        else:
            heads = moba_w_qkv.shape[2] // (3 * MOBA_HEAD_DIM)
            qkv = _moba_qkv(u, moba_w_qkv, j, c_moba, s1_moba, s2_moba, MOBA_HEAD_DIM ** -0.5)
            o = _moba_attention(qkv, batch, seq, heads)
            y = _matmul(o, moba_w_o, j, F32, tm=1024, tn=512)
        ln_args = dict(alpha=alpha, seq=seq, layer=l)
        nxt = (l + 1, 1, 0) if l + 1 < depth else None
        if not moe_layer:
            xf, u = _resid_ln(xf, y, ada_rows, ln_mix_g, ln_mix_b, comp_g=2, nxt=(l, 4, 3), **ln_args)
            y = _dense_ffn(u, ffn_w_gate_up, ffn_w_down, j)
            outs = _resid_ln(xf, y, ada_rows, ln_ffn_g, ln_ffn_b, comp_g=5, nxt=nxt, **ln_args)
            xf, u = outs[0], (outs[1] if nxt is not None else None)
        else:
            w_router_pad = jnp.zeros((d, LANES), F32).at[:, :n_exp].set(moe_w_router[j])
            xf, u_lin, idx, wt = _resid_ln(xf, y, ada_rows, ln_mix_g, ln_mix_b, comp_g=2, nxt=(l, 4, 3),
                                           w_router_pad=w_router_pad, n_exp=n_exp, **ln_args)
            pos, row_token, item_e, item_row0, item_nsub, _ = _route_metadata(idx[:, :TOP_K], n_exp)
            xs = _dispatch(u_lin, row_token, d)
            ys_lin = _moe_ffn(xs, moe_w_gate_up, moe_w_down, j, item_e, item_row0, item_nsub)
            xf, u = _moe_combine_ln(ys_lin, pos, wt, xf, ada_rows, ln_ffn_g, ln_ffn_b, comp_g=5, nxt=nxt, **ln_args)
    return xf.reshape(batch, seq, d)
```

```python
import functools

import jax
import jax.numpy as jnp
from jax import lax
from jax.experimental import pallas as pl
from jax.experimental.pallas import tpu as pltpu

F32 = jnp.float32
BF16 = jnp.bfloat16

ROPE_THETA = 500000.0
LN_EPS = 1e-5
RMS_EPS = 1e-6
MLA_NOPE = 128
MLA_ROPE = 64
MLA_V = 128
MOBA_HEAD_DIM = 128
MOBA_ROT_DIM = 32
MOBA_BLOCK = 256
MOBA_TOPK = 3
TOP_K = 2

LANES = 128
SUBLANES = 8
MLA_HEAD_PAD = 2 * LANES

ADA_BATCH_PAD = SUBLANES
MOE_ROW_BLOCK = 256
MOE_SUPER_BLOCKS = 10


def _cparams(n_axes, vmem_mb):
    return pltpu.CompilerParams(dimension_semantics=("arbitrary",) * n_axes,
                                vmem_limit_bytes=vmem_mb * 1024 * 1024)


def _split_bf16(x):
    hi = x.astype(BF16)
    return hi, (x - hi.astype(F32)).astype(BF16)


def _ada_row(layer, batch, comp):
    return (layer * ADA_BATCH_PAD + batch) * 6 + comp


def _ada_kernel(c_ref, w_ref, b_ref, o_ref):
    c = c_ref[...]
    ca = (c * jax.nn.sigmoid(c)).astype(BF16)
    o_ref[...] = jnp.dot(ca, w_ref[...].astype(BF16), preferred_element_type=F32) + b_ref[...]


def _ada_all(c, w_ada, b_ada):
    depth, d, n6 = w_ada.shape
    b = c.shape[0]
    c_pad = jnp.zeros((ADA_BATCH_PAD, d), F32).at[:b].set(c)
    tn = 1024
    out = pl.pallas_call(
        _ada_kernel,
        grid=(depth, n6 // tn),
        in_specs=[pl.BlockSpec((ADA_BATCH_PAD, d), lambda l, j: (0, 0)),
                  pl.BlockSpec((None, d, tn), lambda l, j: (l, 0, j)),
                  pl.BlockSpec((None, 1, tn), lambda l, j: (l, 0, j))],
        out_specs=pl.BlockSpec((None, ADA_BATCH_PAD, tn), lambda l, j: (l, 0, j)),
        out_shape=jax.ShapeDtypeStruct((depth, ADA_BATCH_PAD, n6), F32),
        compiler_params=_cparams(2, 40),
        name="ada",
    )(c_pad, w_ada, b_ada.reshape(depth, 1, n6))
    return out.reshape(depth * ADA_BATCH_PAD * 6, 1, d)


def _modulate_kernel(x_ref, sc_ref, sh_ref, u_ref):
    u_ref[...] = (x_ref[...] * (1.0 + sc_ref[0]) + sh_ref[0]).astype(u_ref.dtype)


def _modulate(x, ada_rows, layer, seq, comp_sc, comp_sh, out_dtype):
    t, d = x.shape
    tm = 512
    tpb = seq // tm
    row = lambda comp: pl.BlockSpec((1, 1, d), lambda i: (_ada_row(layer, i // tpb, comp), 0, 0))
    return pl.pallas_call(
        _modulate_kernel,
        grid=(t // tm,),
        in_specs=[pl.BlockSpec((tm, d), lambda i: (i, 0)), row(comp_sc), row(comp_sh)],
        out_specs=pl.BlockSpec((tm, d), lambda i: (i, 0)),
        out_shape=jax.ShapeDtypeStruct((t, d), out_dtype),
        compiler_params=_cparams(1, 40),
        name="modulate",
    )(x, ada_rows, ada_rows)


def _ln_modulate(z, lg, lb, sc_sh):
    mu = jnp.mean(z, axis=-1, keepdims=True)
    zc = z - mu
    var = jnp.mean(zc * zc, axis=-1, keepdims=True)
    xn = zc * lax.rsqrt(var + LN_EPS) * lg + lb
    if sc_sh is None:
        return xn, None
    sc, sh = sc_sh
    return xn, xn * (1.0 + sc) + sh


def _row_pitch(chunks):
    return chunks + 1


def _store_linear(dst_ref, val, chunks):
    rows = val.shape[0]
    pitch = _row_pitch(chunks)
    for c in range(chunks):
        dst_ref[pl.ds(c, rows, stride=pitch), :] = val[:, c * LANES:(c + 1) * LANES]
    dst_ref[pl.ds(chunks, rows, stride=pitch), :] = jnp.zeros((rows, LANES), val.dtype)


def _top2_route(logits, n_exp):
    lane = lax.broadcasted_iota(jnp.int32, logits.shape, 1)
    lg = jnp.where(lane < n_exp, logits, -jnp.inf)
    m1 = jnp.max(lg, axis=-1, keepdims=True)
    i1 = jnp.min(jnp.where(lg == m1, lane, LANES), axis=-1, keepdims=True)
    lg2 = jnp.where(lane == i1, -jnp.inf, lg)
    m2 = jnp.max(lg2, axis=-1, keepdims=True)
    i2 = jnp.min(jnp.where(lg2 == m2, lane, LANES), axis=-1, keepdims=True)
    e = jnp.exp(m2 - m1)
    w1 = 1.0 / (1.0 + e)
    w2 = e / (1.0 + e)
    idx = jnp.where(lane == 0, i1, jnp.where(lane == 1, i2, 0))
    wt = jnp.where(lane == 0, w1, jnp.where(lane == 1, w2, 0.0))
    return idx, wt


def _resid_ln_kernel(*refs, alpha, mode, n_exp, chunks, proj):
    if proj:
        x_ref, a_ref, w_ref, g_ref, lg_ref, lb_ref = refs[:6]
        y = jnp.dot(a_ref[...], w_ref[...], preferred_element_type=F32)
        rest = refs[6:]
    else:
        x_ref, y_ref, g_ref, lg_ref, lb_ref = refs[:5]
        y = y_ref[...].astype(F32)
        rest = refs[5:]
    if mode == "last":
        (xo_ref,) = rest
        sc_sh = None
    elif mode == "next":
        sc_ref, sh_ref, xo_ref, uo_ref = rest
        sc_sh = (sc_ref[0], sh_ref[0])
    else:
        sc_ref, sh_ref, wr_ref, xo_ref, uo_ref, idx_ref, wt_ref = rest
        sc_sh = (sc_ref[0], sh_ref[0])
    z = alpha * x_ref[...] + (1.0 + g_ref[0]) * y
    xn, u = _ln_modulate(z, lg_ref[0], lb_ref[0], sc_sh)
    xo_ref[...] = xn
    if mode == "next":
        uo_ref[...] = u.astype(uo_ref.dtype)
    elif mode == "route":
        _store_linear(uo_ref, u, chunks)
        uh, ul = _split_bf16(u)
        wh, wl = _split_bf16(wr_ref[...])
        dot = lambda a, b: jnp.dot(a, b, preferred_element_type=F32)
        idx, wt = _top2_route(dot(uh, wh) + dot(uh, wl) + dot(ul, wh), n_exp)
        idx_ref[...] = idx
        wt_ref[...] = wt


def _resid_ln(x, y, ada_rows, ln_g, ln_b, *, alpha, seq, layer, comp_g, nxt, w_router_pad=None, n_exp=0):
    t, d = x.shape
    tm = 256
    tpb = seq // tm
    chunks = d // LANES
    proj = isinstance(y, tuple)
    mode = "last" if nxt is None else ("route" if w_router_pad is not None else "next")
    row = lambda l, comp: pl.BlockSpec((1, 1, d), lambda i: (_ada_row(l, i // tpb, comp), 0, 0))
    tile = pl.BlockSpec((tm, d), lambda i: (i, 0))
    lanes = pl.BlockSpec((tm, LANES), lambda i: (i, 0))
    lnp = pl.BlockSpec((1, 1, d), lambda i: (layer, 0, 0))
    if proj:
        a, w_stack, j = y
        k = a.shape[1]
        in_specs = [tile, pl.BlockSpec((tm, k), lambda i: (i, 0)), pl.BlockSpec((None, k, d), lambda i: (j, 0, 0))]
        args = [x, a, w_stack]
    else:
        in_specs = [tile, tile]
        args = [x, y]
    in_specs += [row(layer, comp_g), lnp, lnp]
    args += [ada_rows, ln_g, ln_b]
    out_specs = [tile]
    out_shape = [jax.ShapeDtypeStruct((t, d), F32)]
    if mode != "last":
        in_specs += [row(nxt[0], nxt[1]), row(nxt[0], nxt[2])]
        args += [ada_rows, ada_rows]
    if mode == "next":
        out_specs.append(tile)
        out_shape.append(jax.ShapeDtypeStruct((t, d), BF16))
    elif mode == "route":
        pitch = _row_pitch(chunks)
        in_specs.append(pl.BlockSpec((d, LANES), lambda i: (0, 0)))
        args.append(w_router_pad)
        out_specs += [pl.BlockSpec((tm * pitch, LANES), lambda i: (i, 0)), lanes, lanes]
        out_shape += [jax.ShapeDtypeStruct((t * pitch, LANES), F32),
                      jax.ShapeDtypeStruct((t, LANES), jnp.int32), jax.ShapeDtypeStruct((t, LANES), F32)]
    return pl.pallas_call(
        functools.partial(_resid_ln_kernel, alpha=alpha, mode=mode, n_exp=n_exp, chunks=chunks, proj=proj),
        grid=(t // tm,),
        in_specs=in_specs, out_specs=out_specs, out_shape=out_shape,
        compiler_params=_cparams(1, 56 if proj else 48),
        name=("proj_ln_" if proj else "resid_ln_") + mode,
    )(*args)


def _mm_kernel(a_ref, w_ref, o_ref, wb_ref):
    @pl.when(pl.program_id(1) == 0)
    def _():
        wb_ref[...] = w_ref[...].astype(BF16)
    o_ref[...] = jnp.dot(a_ref[...], wb_ref[...], preferred_element_type=F32).astype(o_ref.dtype)


def _matmul(a, w_stack, layer, out_dtype, *, tm, tn):
    m, k = a.shape
    n = w_stack.shape[2]
    return pl.pallas_call(
        _mm_kernel,
        grid=(n // tn, m // tm),
        in_specs=[pl.BlockSpec((tm, k), lambda j, i: (i, 0)),
                  pl.BlockSpec((None, k, tn), lambda j, i: (layer, 0, j))],
        out_specs=pl.BlockSpec((tm, tn), lambda j, i: (i, j)),
        out_shape=jax.ShapeDtypeStruct((m, n), out_dtype),
        scratch_shapes=[pltpu.VMEM((k, tn), BF16)],
        compiler_params=_cparams(2, 48),
        name="matmul",
    )(a, w_stack)


def _rms(x, g):
    ms = jnp.mean(x * x, axis=-1, keepdims=True)
    return x * lax.rsqrt(ms + RMS_EPS) * g


def _mla_down_kernel(u_ref, w_ref, qn_ref, kvn_ref, c_ref, s_ref, cq_ref, ckv_ref, kr_ref, *, ql, kvl):
    acc = jnp.dot(u_ref[...], w_ref[...], preferred_element_type=F32)
    cq_ref[...] = _rms(acc[:, :ql], qn_ref[0]).astype(BF16)
    ckv_ref[...] = _rms(acc[:, ql:ql + kvl], kvn_ref[0]).astype(BF16)
    xr = acc[:, ql + kvl:]
    kr_ref[...] = (xr * c_ref[...] + pltpu.roll(xr, LANES // 2, 1) * s_ref[...]).astype(BF16)


def _mla_down(u, w_perm, q_norm, kv_norm, layer, cos_t, sin_t, ql, kvl):
    t, d = u.shape
    n = w_perm.shape[1]
    tm = 512
    nrm = lambda width: pl.BlockSpec((1, 1, width), lambda i: (layer, 0, 0))
    rows = lambda width: pl.BlockSpec((tm, width), lambda i: (i, 0))
    return pl.pallas_call(
        functools.partial(_mla_down_kernel, ql=ql, kvl=kvl),
        grid=(t // tm,),
        in_specs=[rows(d), pl.BlockSpec((d, n), lambda i: (0, 0)), nrm(ql), nrm(kvl), rows(LANES), rows(LANES)],
        out_specs=[rows(ql), rows(kvl), rows(LANES)],
        out_shape=[jax.ShapeDtypeStruct((t, ql), BF16), jax.ShapeDtypeStruct((t, kvl), BF16),
                   jax.ShapeDtypeStruct((t, LANES), BF16)],
        compiler_params=_cparams(1, 48),
        name="mla_down",
    )(u, w_perm, q_norm, kv_norm, cos_t, sin_t)


def _mla_qup_kernel(a_ref, w_ref, c_ref, s_ref, o_ref, *, scale, heads):
    acc = jnp.dot(a_ref[...], w_ref[...], preferred_element_type=F32)
    c = c_ref[...]
    s = s_ref[...]
    for h in range(heads):
        b0 = h * MLA_HEAD_PAD
        o_ref[:, b0:b0 + LANES] = (acc[:, b0:b0 + LANES] * scale).astype(BF16)
        xr = acc[:, b0 + LANES:b0 + MLA_HEAD_PAD]
        o_ref[:, b0 + LANES:b0 + MLA_HEAD_PAD] = (
            (xr * c + pltpu.roll(xr, LANES // 2, 1) * s) * scale).astype(BF16)


def _mla_qup(cq, w_perm, cos_t, sin_t, scale):
    t, k = cq.shape
    n = w_perm.shape[1]
    tm, tn = 1024, 1024
    rows = pl.BlockSpec((tm, LANES), lambda j, i: (i, 0))
    return pl.pallas_call(
        functools.partial(_mla_qup_kernel, scale=scale, heads=tn // MLA_HEAD_PAD),
        grid=(n // tn, t // tm),
        in_specs=[pl.BlockSpec((tm, k), lambda j, i: (i, 0)), pl.BlockSpec((k, tn), lambda j, i: (0, j)),
                  rows, rows],
        out_specs=pl.BlockSpec((tm, tn), lambda j, i: (i, j)),
        out_shape=jax.ShapeDtypeStruct((t, n), BF16),
        compiler_params=_cparams(2, 48),
        name="mla_qup",
    )(cq, w_perm, cos_t, sin_t)


def _moba_qkv_kernel(u_ref, w_ref, tab_ref, o_ref, wb_ref, *, heads):
    @pl.when(pl.program_id(1) == 0)
    def _():
        wb_ref[...] = w_ref[...].astype(BF16)

    acc = jnp.dot(u_ref[...], wb_ref[...], preferred_element_type=F32)
    c = tab_ref[:, :LANES]
    s1 = tab_ref[:, LANES:2 * LANES]
    s2 = tab_ref[:, 2 * LANES:]
    half = MOBA_ROT_DIM // 2
    for h in range(heads):
        x = acc[:, h * LANES:(h + 1) * LANES]
        r = x * c + pltpu.roll(x, half, 1) * s1 + pltpu.roll(x, LANES - half, 1) * s2
        o_ref[:, h * LANES:(h + 1) * LANES] = r.astype(BF16)


def _moba_qkv(u, w_stack, layer, tables):
    t, k = u.shape
    n = w_stack.shape[2]
    tm, tn = 1024, 512
    tiles_per_sec = (n // 3) // tn
    return pl.pallas_call(
        functools.partial(_moba_qkv_kernel, heads=tn // LANES),
        grid=(n // tn, t // tm),
        in_specs=[pl.BlockSpec((tm, k), lambda j, i: (i, 0)),
                  pl.BlockSpec((None, k, tn), lambda j, i: (layer, 0, j)),
                  pl.BlockSpec((None, tm, 3 * LANES), lambda j, i: (j // tiles_per_sec, i, 0))],
        out_specs=pl.BlockSpec((tm, tn), lambda j, i: (i, j)),
        out_shape=jax.ShapeDtypeStruct((t, n), BF16),
        scratch_shapes=[pltpu.VMEM((k, tn), BF16)],
        compiler_params=_cparams(2, 48),
        name="moba_qkv",
    )(u, w_stack, tables)


_NT = (((1,), (1,)), ((), ()))


def _softmax_pv(s, v):
    m = jnp.max(s, axis=-1, keepdims=True)
    p = jnp.exp(s - m)
    l = jnp.sum(p, axis=-1, keepdims=True)
    return jnp.dot(p.astype(BF16), v, preferred_element_type=F32) / l


def _mla_attn_kernel(q_ref, kv_ref, kr_ref, o_ref, kfull_ref, *, tq):
    seq = q_ref.shape[0]
    kfull_ref[:, :LANES] = kv_ref[:, :LANES]
    kfull_ref[:, LANES:] = kr_ref[...]
    row = lax.broadcasted_iota(jnp.int32, (tq, tq), 0)
    col = lax.broadcasted_iota(jnp.int32, (tq, tq), 1)
    causal = col <= row
    for n in range(seq // tq):
        q = q_ref[n * tq:(n + 1) * tq, :]
        width = (n + 1) * tq
        s = lax.dot_general(q, kfull_ref[0:width, :], _NT, preferred_element_type=F32)
        diag = jnp.where(causal, s[:, n * tq:], -jnp.inf)
        s = jnp.concatenate([s[:, :n * tq], diag], axis=1) if n else diag
        o_ref[n * tq:(n + 1) * tq, :] = _softmax_pv(s, kv_ref[0:width, LANES:]).astype(o_ref.dtype)


def _mla_attention(q, kv, kr, batch, seq, heads):
    t = q.shape[0]
    return pl.pallas_call(
        functools.partial(_mla_attn_kernel, tq=512),
        grid=(batch, heads),
        in_specs=[pl.BlockSpec((seq, MLA_HEAD_PAD), lambda b, h: (b, h)),
                  pl.BlockSpec((seq, MLA_NOPE + MLA_V), lambda b, h: (b, h)),
                  pl.BlockSpec((seq, LANES), lambda b, h: (b, 0))],
        out_specs=pl.BlockSpec((seq, MLA_V), lambda b, h: (b, h)),
        out_shape=jax.ShapeDtypeStruct((t, heads * MLA_V), BF16),
        scratch_shapes=[pltpu.VMEM((seq, MLA_HEAD_PAD), BF16)],
        compiler_params=_cparams(2, 48),
        name="mla_attn",
    )(q, kv, kr)


def _moba_attn_kernel(q_ref, k_ref, v_ref, o_ref, *, nb):
    blk = MOBA_BLOCK
    seq = k_ref.shape[0]
    r = lax.broadcasted_iota(jnp.int32, (LANES, seq), 0)
    c = lax.broadcasted_iota(jnp.int32, (LANES, seq), 1)
    ind = jnp.where(c // blk == r, 1.0 / blk, 0.0).astype(BF16)
    km = jnp.dot(ind, k_ref[...], preferred_element_type=F32)
    kmh, kml = _split_bf16(km)
    row = lax.broadcasted_iota(jnp.int32, (blk, blk), 0)
    col = lax.broadcasted_iota(jnp.int32, (blk, blk), 1)
    causal = col <= row
    lane = lax.broadcasted_iota(jnp.int32, (blk, LANES), 1)
    for n in range(nb):
        q = q_ref[n * blk:(n + 1) * blk, :]
        width = (n + 1) * blk
        s = lax.dot_general(q, k_ref[0:width, :], _NT, preferred_element_type=F32)
        parts = []
        if n > MOBA_TOPK:
            gate = (lax.dot_general(q, kmh, _NT, preferred_element_type=F32)
                    + lax.dot_general(q, kml, _NT, preferred_element_type=F32))
            past = lane < n
        for j in range(n):
            sj = s[:, j * blk:(j + 1) * blk]
            if n > MOBA_TOPK:
                gj = gate[:, j:j + 1]
                beats = past & ((gate > gj) | ((gate == gj) & (lane < j)))
                n_beats = jnp.sum(beats.astype(F32), axis=-1, keepdims=True)
                sj = jnp.where(n_beats < MOBA_TOPK, sj, -jnp.inf)
            parts.append(sj)
        parts.append(jnp.where(causal, s[:, n * blk:], -jnp.inf))
        s = jnp.concatenate(parts, axis=1) if n else parts[0]
        o_ref[n * blk:(n + 1) * blk, :] = _softmax_pv(s, v_ref[0:width, :]).astype(o_ref.dtype)


def _moba_attention(qkv, batch, seq, heads):
    t = qkv.shape[0]
    d = MOBA_HEAD_DIM
    return pl.pallas_call(
        functools.partial(_moba_attn_kernel, nb=seq // MOBA_BLOCK),
        grid=(batch, heads),
        in_specs=[pl.BlockSpec((seq, d), lambda b, h: (b, h)),
                  pl.BlockSpec((seq, d), lambda b, h: (b, heads + h)),
                  pl.BlockSpec((seq, d), lambda b, h: (b, 2 * heads + h))],
        out_specs=pl.BlockSpec((seq, d), lambda b, h: (b, h)),
        out_shape=jax.ShapeDtypeStruct((t, heads * d), BF16),
        compiler_params=_cparams(2, 48),
        name="moba_attn",
    )(qkv, qkv, qkv)


def _swiglu_partial(x, wg, wu, wd):
    g = jnp.dot(x, wg, preferred_element_type=F32)
    u = jnp.dot(x, wu, preferred_element_type=F32)
    a = (g * jax.nn.sigmoid(g) * u).astype(BF16)
    return jnp.dot(a, wd, preferred_element_type=F32)


def _ffn_kernel(u_ref, wg_ref, wu_ref, wd_ref, o_ref):
    f = pl.program_id(1)
    @pl.when(f == 0)
    def _():
        o_ref[...] = jnp.zeros_like(o_ref)

    o_ref[...] += _swiglu_partial(u_ref[...], wg_ref[...].astype(BF16), wu_ref[...].astype(BF16),
                                  wd_ref[...].astype(BF16))


def _dense_ffn(u, w_gate_up, w_down, layer):
    t, d = u.shape
    dff = w_down.shape[1]
    tm, tf = 1024, 256
    nf = dff // tf
    return pl.pallas_call(
        _ffn_kernel,
        grid=(t // tm, nf),
        in_specs=[pl.BlockSpec((tm, d), lambda i, f: (i, 0)),
                  pl.BlockSpec((None, d, tf), lambda i, f: (layer, 0, f)),
                  pl.BlockSpec((None, d, tf), lambda i, f: (layer, 0, nf + f)),
                  pl.BlockSpec((None, tf, d), lambda i, f: (layer, f, 0))],
        out_specs=pl.BlockSpec((tm, d), lambda i, f: (i, 0)),
        out_shape=jax.ShapeDtypeStruct((t, d), F32),
        compiler_params=_cparams(2, 56),
        name="dense_ffn",
    )(u, w_gate_up, w_gate_up, w_down)


def _route_metadata(idx2, n_exp):
    rb, st = MOE_ROW_BLOCK, MOE_SUPER_BLOCKS
    t = idx2.shape[0]
    a = t * TOP_K
    n_items = (a // rb + n_exp - 1 + n_exp * (st - 1)) // st
    e_flat = idx2.reshape(a)
    onehot = (e_flat[:, None] == jnp.arange(n_exp, dtype=jnp.int32)[None, :]).astype(jnp.int32)
    csum = jnp.cumsum(onehot, axis=0)
    rank = jnp.sum((csum - onehot) * onehot, axis=1)
    counts = csum[-1]
    nsub = (counts + rb - 1) // rb
    sub_start = jnp.cumsum(nsub) - nsub
    dest = jnp.sum(onehot * (sub_start * rb)[None, :], axis=1) + rank
    p_rows = (a // rb + n_exp) * rb
    token_flat = jnp.arange(a, dtype=jnp.int32) // TOP_K
    row_token = jnp.zeros((p_rows,), jnp.int32).at[dest].set(token_flat)
    n_it = (nsub + st - 1) // st
    it_end = jnp.cumsum(n_it)
    it_start = it_end - n_it
    w = jnp.arange(n_items, dtype=jnp.int32)
    e_w = jnp.sum((it_end[None, :] <= w[:, None]).astype(jnp.int32), axis=1)
    active = e_w < n_exp
    e_c = jnp.minimum(e_w, n_exp - 1)
    local = w - it_start[e_c]
    item_nsub = jnp.where(active, jnp.clip(nsub[e_c] - local * st, 0, st), 0)
    item_row0 = jnp.where(active, (sub_start[e_c] + local * st) * rb, 0)
    e_last = jnp.max(jnp.where(n_it > 0, jnp.arange(n_exp, dtype=jnp.int32), 0))
    item_e = jnp.where(active, e_c, e_last)
    item_row0 = jnp.concatenate([item_row0, jnp.sum(nsub, keepdims=True) * rb])
    return (dest.astype(jnp.int32), row_token, item_e.astype(jnp.int32), item_row0.astype(jnp.int32),
            item_nsub.astype(jnp.int32), p_rows)


ROW_DMA_UNROLL = 8


def _row_copy(src_hbm, src_row, dst, dst_row, sem, chunks):
    pitch = _row_pitch(chunks)
    return pltpu.make_async_copy(src_hbm.at[pl.ds(src_row * pitch, chunks), :],
                                 dst.at[pl.ds(dst_row * pitch, chunks), :], sem)


def _rows_wait(src_hbm, dst, sem, rows, chunks):
    n = rows * chunks
    pltpu.make_async_copy(src_hbm.at[pl.ds(0, n), :], dst.at[pl.ds(0, n), :], sem).wait()


def _dispatch_kernel(tok_ref, u_hbm, o_ref, stage_ref, sem, *, rb, chunks):
    i = pl.program_id(0)
    pitch = _row_pitch(chunks)

    def issue(step, slot):
        def body(g, c):
            for k in range(ROW_DMA_UNROLL):
                r = g * ROW_DMA_UNROLL + k
                _row_copy(u_hbm, tok_ref[step * rb + r], stage_ref.at[slot], r, sem.at[slot],
                          chunks).start(priority=k % 2)
            return c
        lax.fori_loop(0, rb // ROW_DMA_UNROLL, body, 0)

    @pl.when(i == 0)
    def _():
        issue(0, 0)

    @pl.when(i + 1 < pl.num_programs(0))
    def _():
        issue(i + 1, (i + 1) % 2)

    slot = i % 2
    _rows_wait(u_hbm, stage_ref.at[slot], sem.at[slot], rb, chunks)
    for c in range(chunks):
        o_ref[:, c * LANES:(c + 1) * LANES] = stage_ref[slot, pl.ds(c, rb, stride=pitch), :].astype(o_ref.dtype)


def _dispatch(u_lin, row_token, d):
    p_rows = row_token.shape[0]
    rb = MOE_ROW_BLOCK
    chunks = d // LANES
    return pl.pallas_call(
        functools.partial(_dispatch_kernel, rb=rb, chunks=chunks),
        grid_spec=pltpu.PrefetchScalarGridSpec(
            num_scalar_prefetch=1,
            grid=(p_rows // rb,),
            in_specs=[pl.BlockSpec(memory_space=pl.ANY)],
            out_specs=pl.BlockSpec((rb, d), lambda i, tok: (i, 0)),
            scratch_shapes=[pltpu.VMEM((2, rb * _row_pitch(chunks), LANES), F32),
                            pltpu.SemaphoreType.DMA((2,))]),
        out_shape=jax.ShapeDtypeStruct((p_rows, d), BF16),
        compiler_params=_cparams(1, 40),
        name="moe_dispatch",
    )(row_token, u_lin)


def _moe_ffn_kernel(e_ref, row0_ref, nsub_ref, xs_hbm, wg_ref, wu_ref, wd_ref, ys_hbm,
                    x_ref, acc_ref, wgb_ref, wub_ref, wdb_ref, stage_ref, sem, *, rb, nf, n_items, chunks):
    w = pl.program_id(0)
    f = pl.program_id(1)
    nsub = nsub_ref[w]
    row0 = row0_ref[w]

    def for_range(n, fn):
        def body(r, c):
            fn(r)
            return c
        lax.fori_loop(0, n, body, 0)

    def load(r):
        return pltpu.make_async_copy(xs_hbm.at[pl.ds(pl.multiple_of(row0 + r * rb, rb), rb), :],
                                     x_ref.at[pl.ds(pl.multiple_of(r * rb, rb), rb), :], sem.at[0])

    pitch = _row_pitch(chunks)

    def store(row, slot):
        return pltpu.make_async_copy(
            stage_ref.at[slot], ys_hbm.at[pl.ds(pl.multiple_of(row * pitch, rb * pitch), rb * pitch), :],
            sem.at[1 + slot])

    @pl.when(f == 0)
    def _():
        for_range(nsub, lambda r: load(r).start())
        acc_ref[...] = jnp.zeros_like(acc_ref)
        for_range(nsub, lambda r: load(r).wait())

    @pl.when(nsub > 0)
    def _():
        wgb_ref[...] = wg_ref[...].astype(BF16)
        wub_ref[...] = wu_ref[...].astype(BF16)
        wdb_ref[...] = wd_ref[...].astype(BF16)

        def chunk(start, size):
            rows = pl.ds(pl.multiple_of(start, rb), size)
            acc_ref[rows, :] += _swiglu_partial(x_ref[rows, :], wgb_ref[...], wub_ref[...], wdb_ref[...])

        for_range(nsub // 2, lambda p: chunk(p * (2 * rb), 2 * rb))

        @pl.when(nsub % 2 == 1)
        def _():
            chunk((nsub - 1) * rb, rb)

    @pl.when(f == nf - 1)
    def _():
        def emit(r):
            slot = r % 2

            @pl.when(r >= 2)
            def _():
                store(row0 + (r - 2) * rb, slot).wait()

            _store_linear(stage_ref.at[slot], acc_ref[pl.ds(pl.multiple_of(r * rb, rb), rb), :], chunks)
            store(row0 + r * rb, slot).start()

        for_range(nsub, emit)

        @pl.when(nsub >= 2)
        def _():
            store(row0, nsub % 2).wait()

        @pl.when(nsub >= 1)
        def _():
            store(row0, (nsub - 1) % 2).wait()

    @pl.when((f == nf - 1) & (w == n_items - 1))
    def _():
        used = row0_ref[n_items]
        n_tail = (ys_hbm.shape[0] // pitch - used) // rb
        stage_ref[0] = jnp.zeros(stage_ref.shape[1:], F32)
        for_range(n_tail, lambda r: store(used + r * rb, 0).start())
        for_range(n_tail, lambda r: store(used + r * rb, 0).wait())


def _moe_ffn(xs, w_gate_up, w_down, layer, item_e, item_row0, item_nsub):
    p_rows, d = xs.shape
    dff = w_down.shape[2]
    rb, st = MOE_ROW_BLOCK, MOE_SUPER_BLOCKS
    n_items = item_e.shape[0]
    chunks = d // LANES
    tf = 256
    nf = dff // tf

    def f_eff(f, nsub, w):
        return jnp.where(nsub[w] > 0, f, nf - 1)

    return pl.pallas_call(
        functools.partial(_moe_ffn_kernel, rb=rb, nf=nf, n_items=n_items, chunks=chunks),
        grid_spec=pltpu.PrefetchScalarGridSpec(
            num_scalar_prefetch=3,
            grid=(n_items, nf),
            in_specs=[pl.BlockSpec(memory_space=pl.ANY),
                      pl.BlockSpec((None, None, d, tf), lambda w, f, e, r0, ns: (layer, e[w], 0, f_eff(f, ns, w))),
                      pl.BlockSpec((None, None, d, tf),
                                   lambda w, f, e, r0, ns: (layer, e[w], 0, nf + f_eff(f, ns, w))),
                      pl.BlockSpec((None, None, tf, d), lambda w, f, e, r0, ns: (layer, e[w], f_eff(f, ns, w), 0))],
            out_specs=pl.BlockSpec(memory_space=pl.ANY),
            scratch_shapes=[pltpu.VMEM((st * rb, d), BF16), pltpu.VMEM((st * rb, d), F32),
                            pltpu.VMEM((d, tf), BF16), pltpu.VMEM((d, tf), BF16), pltpu.VMEM((tf, d), BF16),
                            pltpu.VMEM((2, rb * _row_pitch(chunks), LANES), F32), pltpu.SemaphoreType.DMA((3,))]),
        out_shape=jax.ShapeDtypeStruct((p_rows * _row_pitch(chunks), LANES), F32),
        compiler_params=_cparams(2, 58),
        name="moe_ffn",
    )(item_e, item_row0, item_nsub, xs, w_gate_up, w_gate_up, w_down)


def _moe_combine_ln_kernel(*refs, alpha, emit_u, tm, chunks):
    if emit_u:
        (pos_ref, ys_hbm, wt_ref, x_ref, g_ref, lg_ref, lb_ref, sc_ref, sh_ref, xo_ref, uo_ref,
         stage_ref, y_ref, sem) = refs
        sc_sh = (sc_ref[0], sh_ref[0])
    else:
        pos_ref, ys_hbm, wt_ref, x_ref, g_ref, lg_ref, lb_ref, xo_ref, stage_ref, y_ref, sem = refs
        sc_sh = None
    i = pl.program_id(0)
    pitch = _row_pitch(chunks)

    def issue(step, slot):
        def body(g, c):
            for j in range(ROW_DMA_UNROLL // TOP_K):
                t = g * (ROW_DMA_UNROLL // TOP_K) + j
                for k in range(TOP_K):
                    _row_copy(ys_hbm, pos_ref[(step * tm + t) * TOP_K + k], stage_ref.at[slot, k], t,
                              sem.at[slot, k], chunks).start(priority=k % 2)
            return c
        lax.fori_loop(0, tm * TOP_K // ROW_DMA_UNROLL, body, 0)

    @pl.when(i == 0)
    def _():
        issue(0, 0)

    @pl.when(i + 1 < pl.num_programs(0))
    def _():
        issue(i + 1, (i + 1) % 2)

    slot = i % 2
    for k in range(TOP_K):
        _rows_wait(ys_hbm, stage_ref.at[slot, k], sem.at[slot, k], tm, chunks)
    w1 = wt_ref[:, 0:1]
    w2 = wt_ref[:, 1:2]
    for c in range(chunks):
        y_ref[:, c * LANES:(c + 1) * LANES] = (stage_ref[slot, 0, pl.ds(c, tm, stride=pitch), :] * w1
                                               + stage_ref[slot, 1, pl.ds(c, tm, stride=pitch), :] * w2)
    z = alpha * x_ref[...] + (1.0 + g_ref[0]) * y_ref[...]
    xn, u = _ln_modulate(z, lg_ref[0], lb_ref[0], sc_sh)
    xo_ref[...] = xn
    if emit_u:
        uo_ref[...] = u.astype(uo_ref.dtype)


def _moe_combine_ln(ys_lin, pos, wt, x, ada_rows, ln_g, ln_b, *, alpha, seq, layer, comp_g, nxt):
    t, d = x.shape
    tm = 256
    tpb = seq // tm
    chunks = d // LANES
    row = lambda l, comp: pl.BlockSpec((1, 1, d), lambda i, p: (_ada_row(l, i // tpb, comp), 0, 0))
    tile = pl.BlockSpec((tm, d), lambda i, p: (i, 0))
    lnp = pl.BlockSpec((1, 1, d), lambda i, p: (layer, 0, 0))
    in_specs = [pl.BlockSpec(memory_space=pl.ANY), pl.BlockSpec((tm, LANES), lambda i, p: (i, 0)), tile,
                row(layer, comp_g), lnp, lnp]
    args = [ys_lin, wt, x, ada_rows, ln_g, ln_b]
    out_specs = [tile]
    out_shape = [jax.ShapeDtypeStruct((t, d), F32)]
    if nxt is not None:
        in_specs += [row(nxt[0], nxt[1]), row(nxt[0], nxt[2])]
        args += [ada_rows, ada_rows]
        out_specs.append(tile)
        out_shape.append(jax.ShapeDtypeStruct((t, d), BF16))
    outs = pl.pallas_call(
        functools.partial(_moe_combine_ln_kernel, alpha=alpha, emit_u=nxt is not None, tm=tm, chunks=chunks),
        grid_spec=pltpu.PrefetchScalarGridSpec(
            num_scalar_prefetch=1,
            grid=(t // tm,),
            in_specs=in_specs, out_specs=out_specs,
            scratch_shapes=[pltpu.VMEM((2, TOP_K, tm * _row_pitch(chunks), LANES), F32), pltpu.VMEM((tm, d), F32),
                            pltpu.SemaphoreType.DMA((2, TOP_K))]),
        out_shape=out_shape,
        compiler_params=_cparams(1, 48),
        name="moe_combine_ln",
    )(pos, *args)
    return (outs[0], outs[1]) if nxt is not None else (outs[0], None)


def _rope_cos_sin(positions, dim):
    inv_freq = ROPE_THETA ** (-jnp.arange(0, dim, 2, dtype=F32) / dim)
    ang = positions.astype(F32).reshape(-1)[:, None] * inv_freq
    return jnp.cos(ang), jnp.sin(ang)


def _mla_rope_tables(positions):
    cos, sin = _rope_cos_sin(positions, MLA_ROPE)
    z = jnp.zeros_like(cos)
    return jnp.concatenate([cos, z, cos, z], axis=1), jnp.concatenate([-sin, z, sin, z], axis=1)


def _moba_rope_tables(positions, scale):
    cos, sin = _rope_cos_sin(positions, MOBA_ROT_DIM)
    t, half = cos.shape
    rest = LANES - 2 * half
    c = jnp.concatenate([cos, cos, jnp.ones((t, rest), F32)], axis=1)
    s1 = jnp.concatenate([jnp.zeros((t, half), F32), sin, jnp.zeros((t, rest), F32)], axis=1)
    s2 = jnp.concatenate([-sin, jnp.zeros((t, half + rest), F32)], axis=1)
    rot = jnp.concatenate([c, s1, s2], axis=1)
    ident = jnp.concatenate([jnp.ones((t, LANES), F32), jnp.zeros((t, 2 * LANES), F32)], axis=1)
    return jnp.stack([rot * scale, rot, ident])


def _spread_rope_cols(w_rope):
    half = MLA_ROPE // 2
    z = jnp.zeros(w_rope.shape[:-1] + (LANES // 2 - half,), w_rope.dtype)
    return jnp.concatenate([w_rope[..., :half], z, w_rope[..., half:], z], axis=-1)


def _mla_weights(w_down, w_uq, ql, kvl):
    k = w_uq.shape[0]
    heads = w_uq.shape[1] // (MLA_NOPE + MLA_ROPE)
    wd = jnp.concatenate([w_down[:, :ql + kvl], _spread_rope_cols(w_down[:, ql + kvl:])], axis=1).astype(BF16)
    wq = w_uq.reshape(k, heads, MLA_NOPE + MLA_ROPE)
    wq = jnp.concatenate([wq[..., :MLA_NOPE], _spread_rope_cols(wq[..., MLA_NOPE:])], axis=-1)
    return wd, wq.reshape(k, heads * MLA_HEAD_PAD).astype(BF16), heads


def kernel(x, c, positions, w_ada, b_ada, ln_mix_g, ln_mix_b, ln_ffn_g, ln_ffn_b, mla_w_down, mla_q_norm,
           mla_kv_norm, mla_w_uq, mla_w_ukv, mla_w_o, moba_w_qkv, moba_w_o, ffn_w_gate_up, ffn_w_down,
           moe_w_router, moe_w_gate_up, moe_w_down):
    batch, seq, d = x.shape
    depth = w_ada.shape[0]
    t = batch * seq
    alpha = (2.0 * depth) ** 0.25
    ql = mla_q_norm.shape[1]
    kvl = mla_kv_norm.shape[1]
    n_exp = moe_w_router.shape[2]

    ada_rows = _ada_all(c, w_ada, b_ada)
    cos_mla, sin_mla = _mla_rope_tables(positions)
    moba_tables = _moba_rope_tables(positions, MOBA_HEAD_DIM ** -0.5)
    mla_w_o_b = mla_w_o.astype(BF16)
    moba_w_o_b = moba_w_o.astype(BF16)
    ln3 = lambda p: p.reshape(depth, 1, d)
    ln_mix_g, ln_mix_b, ln_ffn_g, ln_ffn_b = ln3(ln_mix_g), ln3(ln_mix_b), ln3(ln_ffn_g), ln3(ln_ffn_b)
    q_norm3 = mla_q_norm.reshape(-1, 1, ql)
    kv_norm3 = mla_kv_norm.reshape(-1, 1, kvl)

    xf = x.reshape(t, d)
    u = _modulate(xf, ada_rows, 0, seq, 1, 0, BF16)
    for l in range(depth):
        j = l // 2
        moe_layer = l % 2 == 1
        if l % 2 == 0:
            wd_p, wq_p, heads = _mla_weights(mla_w_down[j], mla_w_uq[j], ql, kvl)
            cq, ckv, kr = _mla_down(u, wd_p, q_norm3, kv_norm3, j, cos_mla, sin_mla, ql, kvl)
            q = _mla_qup(cq, wq_p, cos_mla, sin_mla, (MLA_NOPE + MLA_ROPE) ** -0.5)
            kv = _matmul(ckv, mla_w_ukv, j, BF16, tm=1024, tn=1024)
            o = _mla_attention(q, kv, kr, batch, seq, heads)
            y = (o, mla_w_o_b, j)
        else:
            heads = moba_w_qkv.shape[2] // (3 * MOBA_HEAD_DIM)
            qkv = _moba_qkv(u, moba_w_qkv, j, moba_tables)
            o = _moba_attention(qkv, batch, seq, heads)
            y = (o, moba_w_o_b, j)
        ln_args = dict(alpha=alpha, seq=seq, layer=l)
        nxt = (l + 1, 1, 0) if l + 1 < depth else None
        if not moe_layer:
            xf, u = _resid_ln(xf, y, ada_rows, ln_mix_g, ln_mix_b, comp_g=2, nxt=(l, 4, 3), **ln_args)
            y = _dense_ffn(u, ffn_w_gate_up, ffn_w_down, j)
            outs = _resid_ln(xf, y, ada_rows, ln_ffn_g, ln_ffn_b, comp_g=5, nxt=nxt, **ln_args)
            xf, u = outs[0], (outs[1] if nxt is not None else None)
        else:
            w_router_pad = jnp.zeros((d, LANES), F32).at[:, :n_exp].set(moe_w_router[j])
            xf, u_lin, idx, wt = _resid_ln(xf, y, ada_rows, ln_mix_g, ln_mix_b, comp_g=2, nxt=(l, 4, 3),
                                           w_router_pad=w_router_pad, n_exp=n_exp, **ln_args)
            pos, row_token, item_e, item_row0, item_nsub, _ = _route_metadata(idx[:, :TOP_K], n_exp)
            xs = _dispatch(u_lin, row_token, d)
            ys_lin = _moe_ffn(xs, moe_w_gate_up, moe_w_down, j, item_e, item_row0, item_nsub)
            xf, u = _moe_combine_ln(ys_lin, pos, wt, xf, ada_rows, ln_ffn_g, ln_ffn_b, comp_g=5, nxt=nxt, **ln_args)
    return xf.reshape(batch, seq, d)
```

```python
import functools

import jax
import jax.numpy as jnp
from jax import lax
from jax.experimental import pallas as pl
from jax.experimental.pallas import tpu as pltpu

F32 = jnp.float32
BF16 = jnp.bfloat16

ROPE_THETA = 500000.0
LN_EPS = 1e-5
RMS_EPS = 1e-6
MLA_NOPE = 128
MLA_ROPE = 64
MLA_V = 128
MOBA_HEAD_DIM = 128
MOBA_ROT_DIM = 32
MOBA_BLOCK = 256
MOBA_TOPK = 3
TOP_K = 2

LANES = 128
SUBLANES = 8
MLA_HEAD_PAD = 2 * LANES

ADA_BATCH_PAD = SUBLANES
MOE_ROW_BLOCK = 256
MOE_SUPER_BLOCKS = 10
EPILOGUE_ROWS = 256
PROJ_LN_ROWS = 128


def _cparams(n_axes, vmem_mb):
    return pltpu.CompilerParams(dimension_semantics=("arbitrary",) * n_axes,
                                vmem_limit_bytes=vmem_mb * 1024 * 1024)


def _split_bf16(x):
    hi = x.astype(BF16)
    return hi, (x - hi.astype(F32)).astype(BF16)


def _ada_row(layer, batch, comp):
    return (layer * ADA_BATCH_PAD + batch) * 6 + comp


def _ada_kernel(c_ref, w_ref, b_ref, o_ref):
    c = c_ref[...]
    ca = (c * jax.nn.sigmoid(c)).astype(BF16)
    o_ref[...] = jnp.dot(ca, w_ref[...].astype(BF16), preferred_element_type=F32) + b_ref[...]


def _ada_all(c, w_ada, b_ada):
    depth, d, n6 = w_ada.shape
    b = c.shape[0]
    c_pad = jnp.zeros((ADA_BATCH_PAD, d), F32).at[:b].set(c)
    tn = 1024
    out = pl.pallas_call(
        _ada_kernel,
        grid=(depth, n6 // tn),
        in_specs=[pl.BlockSpec((ADA_BATCH_PAD, d), lambda l, j: (0, 0)),
                  pl.BlockSpec((None, d, tn), lambda l, j: (l, 0, j)),
                  pl.BlockSpec((None, 1, tn), lambda l, j: (l, 0, j))],
        out_specs=pl.BlockSpec((None, ADA_BATCH_PAD, tn), lambda l, j: (l, 0, j)),
        out_shape=jax.ShapeDtypeStruct((depth, ADA_BATCH_PAD, n6), F32),
        compiler_params=_cparams(2, 40),
        name="ada",
    )(c_pad, w_ada, b_ada.reshape(depth, 1, n6))
    return out.reshape(depth * ADA_BATCH_PAD * 6, 1, d)


def _modulate_kernel(x_ref, sc_ref, sh_ref, u_ref):
    u_ref[...] = (x_ref[...] * (1.0 + sc_ref[0]) + sh_ref[0]).astype(u_ref.dtype)


def _modulate(x, ada_rows, layer, seq, comp_sc, comp_sh, out_dtype):
    t, d = x.shape
    tm = 512
    tpb = seq // tm
    row = lambda comp: pl.BlockSpec((1, 1, d), lambda i: (_ada_row(layer, i // tpb, comp), 0, 0))
    return pl.pallas_call(
        _modulate_kernel,
        grid=(t // tm,),
        in_specs=[pl.BlockSpec((tm, d), lambda i: (i, 0)), row(comp_sc), row(comp_sh)],
        out_specs=pl.BlockSpec((tm, d), lambda i: (i, 0)),
        out_shape=jax.ShapeDtypeStruct((t, d), out_dtype),
        compiler_params=_cparams(1, 40),
        name="modulate",
    )(x, ada_rows, ada_rows)


def _ln_modulate(z, lg, lb, sc_sh):
    mu = jnp.mean(z, axis=-1, keepdims=True)
    zc = z - mu
    var = jnp.mean(zc * zc, axis=-1, keepdims=True)
    xn = zc * lax.rsqrt(var + LN_EPS) * lg + lb
    if sc_sh is None:
        return xn, None
    sc, sh = sc_sh
    return xn, xn * (1.0 + sc) + sh


def _row_pitch(chunks):
    return chunks + 1


def _store_linear(dst_ref, val, chunks):
    rows = val.shape[0]
    pitch = _row_pitch(chunks)
    for c in range(chunks):
        dst_ref[pl.ds(c, rows, stride=pitch), :] = val[:, c * LANES:(c + 1) * LANES]
    dst_ref[pl.ds(chunks, rows, stride=pitch), :] = jnp.zeros((rows, LANES), val.dtype)


def _top2_route(logits, n_exp):
    lane = lax.broadcasted_iota(jnp.int32, logits.shape, 1)
    lg = jnp.where(lane < n_exp, logits, -jnp.inf)
    m1 = jnp.max(lg, axis=-1, keepdims=True)
    i1 = jnp.min(jnp.where(lg == m1, lane, LANES), axis=-1, keepdims=True)
    lg2 = jnp.where(lane == i1, -jnp.inf, lg)
    m2 = jnp.max(lg2, axis=-1, keepdims=True)
    i2 = jnp.min(jnp.where(lg2 == m2, lane, LANES), axis=-1, keepdims=True)
    e = jnp.exp(m2 - m1)
    w1 = 1.0 / (1.0 + e)
    w2 = e / (1.0 + e)
    idx = jnp.where(lane == 0, i1, jnp.where(lane == 1, i2, 0))
    wt = jnp.where(lane == 0, w1, jnp.where(lane == 1, w2, 0.0))
    return idx, wt


def _resid_ln_kernel(*refs, alpha, mode, n_exp, chunks, proj):
    if proj:
        x_ref, a_ref, w_ref, g_ref, lg_ref, lb_ref = refs[:6]
        rest = refs[6:]
    else:
        x_ref, y_ref, g_ref, lg_ref, lb_ref = refs[:5]
        rest = refs[5:]
    if mode == "last":
        (xo_ref,) = rest
        sc_sh = None
    elif mode == "next":
        sc_ref, sh_ref, xo_ref, uo_ref = rest
        sc_sh = (sc_ref[0], sh_ref[0])
    else:
        sc_ref, sh_ref, wr_ref, xo_ref, uo_ref, idx_ref, wt_ref = rest
        sc_sh = (sc_ref[0], sh_ref[0])
    tm = x_ref.shape[0]
    rc = PROJ_LN_ROWS if proj else tm

    def project(t):
        return jnp.dot(a_ref[t * rc:(t + 1) * rc, :], w_ref[...], preferred_element_type=F32)

    def finish(t, y):
        rows = slice(t * rc, (t + 1) * rc)
        z = alpha * x_ref[rows, :] + (1.0 + g_ref[0]) * y
        xn, u = _ln_modulate(z, lg_ref[0], lb_ref[0], sc_sh)
        xo_ref[rows, :] = xn
        if mode == "next":
            uo_ref[rows, :] = u.astype(uo_ref.dtype)
        elif mode == "route":
            pitch = _row_pitch(chunks)
            _store_linear(uo_ref.at[t * rc * pitch:(t + 1) * rc * pitch, :], u, chunks)
            uh, ul = _split_bf16(u)
            wh, wl = _split_bf16(wr_ref[...])
            dot = lambda a, b: jnp.dot(a, b, preferred_element_type=F32)
            idx, wt = _top2_route(dot(uh, wh) + dot(uh, wl) + dot(ul, wh), n_exp)
            idx_ref[rows, :] = idx
            wt_ref[rows, :] = wt

    if proj:
        _staged(tm // rc, project, finish)
    else:
        finish(0, y_ref[...].astype(F32))


def _resid_ln(x, y, ada_rows, ln_g, ln_b, *, alpha, seq, layer, comp_g, nxt, w_router_pad=None, n_exp=0):
    t, d = x.shape
    tm = 256
    tpb = seq // tm
    chunks = d // LANES
    proj = isinstance(y, tuple)
    mode = "last" if nxt is None else ("route" if w_router_pad is not None else "next")
    row = lambda l, comp: pl.BlockSpec((1, 1, d), lambda i: (_ada_row(l, i // tpb, comp), 0, 0))
    tile = pl.BlockSpec((tm, d), lambda i: (i, 0))
    lanes = pl.BlockSpec((tm, LANES), lambda i: (i, 0))
    lnp = pl.BlockSpec((1, 1, d), lambda i: (layer, 0, 0))
    if proj:
        a, w_stack, j = y
        k = a.shape[1]
        in_specs = [tile, pl.BlockSpec((tm, k), lambda i: (i, 0)), pl.BlockSpec((None, k, d), lambda i: (j, 0, 0))]
        args = [x, a, w_stack]
    else:
        in_specs = [tile, tile]
        args = [x, y]
    in_specs += [row(layer, comp_g), lnp, lnp]
    args += [ada_rows, ln_g, ln_b]
    out_specs = [tile]
    out_shape = [jax.ShapeDtypeStruct((t, d), F32)]
    if mode != "last":
        in_specs += [row(nxt[0], nxt[1]), row(nxt[0], nxt[2])]
        args += [ada_rows, ada_rows]
    if mode == "next":
        out_specs.append(tile)
        out_shape.append(jax.ShapeDtypeStruct((t, d), BF16))
    elif mode == "route":
        pitch = _row_pitch(chunks)
        in_specs.append(pl.BlockSpec((d, LANES), lambda i: (0, 0)))
        args.append(w_router_pad)
        out_specs += [pl.BlockSpec((tm * pitch, LANES), lambda i: (i, 0)), lanes, lanes]
        out_shape += [jax.ShapeDtypeStruct((t * pitch, LANES), F32),
                      jax.ShapeDtypeStruct((t, LANES), jnp.int32), jax.ShapeDtypeStruct((t, LANES), F32)]
    return pl.pallas_call(
        functools.partial(_resid_ln_kernel, alpha=alpha, mode=mode, n_exp=n_exp, chunks=chunks, proj=proj),
        grid=(t // tm,),
        in_specs=in_specs, out_specs=out_specs, out_shape=out_shape,
        compiler_params=_cparams(1, 56 if proj else 48),
        name=("proj_ln_" if proj else "resid_ln_") + mode,
    )(*args)


def _mm_kernel(a_ref, w_ref, o_ref, wb_ref):
    @pl.when(pl.program_id(1) == 0)
    def _():
        wb_ref[...] = w_ref[...].astype(BF16)
    o_ref[...] = jnp.dot(a_ref[...], wb_ref[...], preferred_element_type=F32).astype(o_ref.dtype)


def _matmul(a, w_stack, layer, out_dtype, *, tm, tn):
    m, k = a.shape
    n = w_stack.shape[2]
    return pl.pallas_call(
        _mm_kernel,
        grid=(n // tn, m // tm),
        in_specs=[pl.BlockSpec((tm, k), lambda j, i: (i, 0)),
                  pl.BlockSpec((None, k, tn), lambda j, i: (layer, 0, j))],
        out_specs=pl.BlockSpec((tm, tn), lambda j, i: (i, j)),
        out_shape=jax.ShapeDtypeStruct((m, n), out_dtype),
        scratch_shapes=[pltpu.VMEM((k, tn), BF16)],
        compiler_params=_cparams(2, 48),
        name="matmul",
    )(a, w_stack)


def _rms(x, g):
    ms = jnp.mean(x * x, axis=-1, keepdims=True)
    return x * lax.rsqrt(ms + RMS_EPS) * g


def _mla_down_kernel(u_ref, w_ref, qn_ref, kvn_ref, c_ref, s_ref, cq_ref, ckv_ref, kr_ref, *, ql, kvl):
    rc = EPILOGUE_ROWS

    def matmul(t):
        return jnp.dot(u_ref[t * rc:(t + 1) * rc, :], w_ref[...], preferred_element_type=F32)

    def norm_rotate(t, acc):
        rows = slice(t * rc, (t + 1) * rc)
        cq_ref[rows, :] = _rms(acc[:, :ql], qn_ref[0]).astype(BF16)
        ckv_ref[rows, :] = _rms(acc[:, ql:ql + kvl], kvn_ref[0]).astype(BF16)
        xr = acc[:, ql + kvl:]
        kr_ref[rows, :] = (xr * c_ref[rows, :] + pltpu.roll(xr, LANES // 2, 1) * s_ref[rows, :]).astype(BF16)

    _staged(u_ref.shape[0] // rc, matmul, norm_rotate)


def _mla_down(u, w_perm, q_norm, kv_norm, layer, cos_t, sin_t, ql, kvl):
    t, d = u.shape
    n = w_perm.shape[1]
    tm = 512
    nrm = lambda width: pl.BlockSpec((1, 1, width), lambda i: (layer, 0, 0))
    rows = lambda width: pl.BlockSpec((tm, width), lambda i: (i, 0))
    return pl.pallas_call(
        functools.partial(_mla_down_kernel, ql=ql, kvl=kvl),
        grid=(t // tm,),
        in_specs=[rows(d), pl.BlockSpec((d, n), lambda i: (0, 0)), nrm(ql), nrm(kvl), rows(LANES), rows(LANES)],
        out_specs=[rows(ql), rows(kvl), rows(LANES)],
        out_shape=[jax.ShapeDtypeStruct((t, ql), BF16), jax.ShapeDtypeStruct((t, kvl), BF16),
                   jax.ShapeDtypeStruct((t, LANES), BF16)],
        compiler_params=_cparams(1, 48),
        name="mla_down",
    )(u, w_perm, q_norm, kv_norm, cos_t, sin_t)


def _mla_qup_kernel(a_ref, w_ref, c_ref, s_ref, o_ref, *, scale, heads):
    rc = EPILOGUE_ROWS

    def matmul(t):
        return jnp.dot(a_ref[t * rc:(t + 1) * rc, :], w_ref[...], preferred_element_type=F32)

    def rotate(t, acc):
        rows = slice(t * rc, (t + 1) * rc)
        c = c_ref[rows, :]
        s = s_ref[rows, :]
        for h in range(heads):
            b0 = h * MLA_HEAD_PAD
            o_ref[rows, b0:b0 + LANES] = (acc[:, b0:b0 + LANES] * scale).astype(BF16)
            xr = acc[:, b0 + LANES:b0 + MLA_HEAD_PAD]
            o_ref[rows, b0 + LANES:b0 + MLA_HEAD_PAD] = (
                (xr * c + pltpu.roll(xr, LANES // 2, 1) * s) * scale).astype(BF16)

    _staged(a_ref.shape[0] // rc, matmul, rotate)


def _mla_qup(cq, w_perm, cos_t, sin_t, scale):
    t, k = cq.shape
    n = w_perm.shape[1]
    tm, tn = 1024, 1024
    rows = pl.BlockSpec((tm, LANES), lambda j, i: (i, 0))
    return pl.pallas_call(
        functools.partial(_mla_qup_kernel, scale=scale, heads=tn // MLA_HEAD_PAD),
        grid=(n // tn, t // tm),
        in_specs=[pl.BlockSpec((tm, k), lambda j, i: (i, 0)), pl.BlockSpec((k, tn), lambda j, i: (0, j)),
                  rows, rows],
        out_specs=pl.BlockSpec((tm, tn), lambda j, i: (i, j)),
        out_shape=jax.ShapeDtypeStruct((t, n), BF16),
        compiler_params=_cparams(2, 48),
        name="mla_qup",
    )(cq, w_perm, cos_t, sin_t)


def _moba_qkv_kernel(u_ref, w_ref, tab_ref, o_ref, wb_ref, *, heads):
    @pl.when(pl.program_id(1) == 0)
    def _():
        wb_ref[...] = w_ref[...].astype(BF16)

    half = MOBA_ROT_DIM // 2
    rc = EPILOGUE_ROWS

    def matmul(t):
        return jnp.dot(u_ref[t * rc:(t + 1) * rc, :], wb_ref[...], preferred_element_type=F32)

    def rotate(t, acc):
        rows = slice(t * rc, (t + 1) * rc)
        c = tab_ref[rows, :LANES]
        s1 = tab_ref[rows, LANES:2 * LANES]
        s2 = tab_ref[rows, 2 * LANES:]
        for h in range(heads):
            x = acc[:, h * LANES:(h + 1) * LANES]
            r = x * c + pltpu.roll(x, half, 1) * s1 + pltpu.roll(x, LANES - half, 1) * s2
            o_ref[rows, h * LANES:(h + 1) * LANES] = r.astype(BF16)

    _staged(u_ref.shape[0] // rc, matmul, rotate)


def _moba_qkv(u, w_stack, layer, tables):
    t, k = u.shape
    n = w_stack.shape[2]
    tm, tn = 1024, 512
    tiles_per_sec = (n // 3) // tn
    return pl.pallas_call(
        functools.partial(_moba_qkv_kernel, heads=tn // LANES),
        grid=(n // tn, t // tm),
        in_specs=[pl.BlockSpec((tm, k), lambda j, i: (i, 0)),
                  pl.BlockSpec((None, k, tn), lambda j, i: (layer, 0, j)),
                  pl.BlockSpec((None, tm, 3 * LANES), lambda j, i: (j // tiles_per_sec, i, 0))],
        out_specs=pl.BlockSpec((tm, tn), lambda j, i: (i, j)),
        out_shape=jax.ShapeDtypeStruct((t, n), BF16),
        scratch_shapes=[pltpu.VMEM((k, tn), BF16)],
        compiler_params=_cparams(2, 48),
        name="moba_qkv",
    )(u, w_stack, tables)


_NT = (((1,), (1,)), ((), ()))


def _softmax_numer(s):
    m = jnp.max(s, axis=-1, keepdims=True)
    return jnp.exp(s - m).astype(BF16)


def _fill_values_ones(vext_ref, v):
    vext_ref[:, :LANES] = v
    vext_ref[:, LANES:] = jnp.ones_like(v)


def _normalised_pv(p, vext):
    o = jnp.dot(p, vext, preferred_element_type=F32)
    return o[:, :LANES] / o[:, LANES:LANES + 1]


def _staged(n_tiles, *stages, reverse=False):
    vals = {}
    for t in range(n_tiles + len(stages) - 1):
        for k, stage in enumerate(stages):
            if 0 <= t - k < n_tiles:
                tile = n_tiles - 1 - (t - k) if reverse else t - k
                vals[tile] = stage(tile) if k == 0 else stage(tile, vals[tile])


def _mla_attn_kernel(q_ref, kv_ref, kr_ref, o_ref, kfull_ref, vext_ref, *, tq):
    seq = q_ref.shape[0]
    kfull_ref[:, :LANES] = kv_ref[:, :LANES]
    kfull_ref[:, LANES:] = kr_ref[...]
    _fill_values_ones(vext_ref, kv_ref[:, LANES:])
    row = lax.broadcasted_iota(jnp.int32, (tq, tq), 0)
    col = lax.broadcasted_iota(jnp.int32, (tq, tq), 1)
    causal = col <= row

    def scores(n):
        q = q_ref[n * tq:(n + 1) * tq, :]
        return lax.dot_general(q, kfull_ref[0:(n + 1) * tq, :], _NT, preferred_element_type=F32)

    def probs(n, s):
        diag = jnp.where(causal, s[:, n * tq:], -jnp.inf)
        return _softmax_numer(jnp.concatenate([s[:, :n * tq], diag], axis=1) if n else diag)

    def output(n, p):
        o_ref[n * tq:(n + 1) * tq, :] = _normalised_pv(p, vext_ref[0:(n + 1) * tq, :]).astype(o_ref.dtype)

    _staged(seq // tq, scores, probs, output, reverse=True)


def _mla_attention(q, kv, kr, batch, seq, heads):
    t = q.shape[0]
    return pl.pallas_call(
        functools.partial(_mla_attn_kernel, tq=512),
        grid=(batch, heads),
        in_specs=[pl.BlockSpec((seq, MLA_HEAD_PAD), lambda b, h: (b, h)),
                  pl.BlockSpec((seq, MLA_NOPE + MLA_V), lambda b, h: (b, h)),
                  pl.BlockSpec((seq, LANES), lambda b, h: (b, 0))],
        out_specs=pl.BlockSpec((seq, MLA_V), lambda b, h: (b, h)),
        out_shape=jax.ShapeDtypeStruct((t, heads * MLA_V), BF16),
        scratch_shapes=[pltpu.VMEM((seq, MLA_HEAD_PAD), BF16), pltpu.VMEM((seq, 2 * LANES), BF16)],
        compiler_params=_cparams(2, 48),
        name="mla_attn",
    )(q, kv, kr)


def _moba_attn_kernel(q_ref, k_ref, v_ref, o_ref, vext_ref, *, nb):
    blk = MOBA_BLOCK
    seq = k_ref.shape[0]
    _fill_values_ones(vext_ref, v_ref[...])
    r = lax.broadcasted_iota(jnp.int32, (LANES, seq), 0)
    c = lax.broadcasted_iota(jnp.int32, (LANES, seq), 1)
    ind = jnp.where(c // blk == r, 1.0 / blk, 0.0).astype(BF16)
    km = jnp.dot(ind, k_ref[...], preferred_element_type=F32)
    kmh, kml = _split_bf16(km)
    row = lax.broadcasted_iota(jnp.int32, (blk, blk), 0)
    col = lax.broadcasted_iota(jnp.int32, (blk, blk), 1)
    causal = col <= row
    lane = lax.broadcasted_iota(jnp.int32, (blk, LANES), 1)

    def scores(n):
        q = q_ref[n * blk:(n + 1) * blk, :]
        s = lax.dot_general(q, k_ref[0:(n + 1) * blk, :], _NT, preferred_element_type=F32)
        if n <= MOBA_TOPK:
            return s, None
        gate = (lax.dot_general(q, kmh, _NT, preferred_element_type=F32)
                + lax.dot_general(q, kml, _NT, preferred_element_type=F32))
        return s, gate

    def probs(n, s_gate):
        s, gate = s_gate
        parts = []
        for j in range(n):
            sj = s[:, j * blk:(j + 1) * blk]
            if gate is not None:
                gj = gate[:, j:j + 1]
                beats = (lane < n) & ((gate > gj) | ((gate == gj) & (lane < j)))
                n_beats = jnp.sum(beats.astype(F32), axis=-1, keepdims=True)
                sj = jnp.where(n_beats < MOBA_TOPK, sj, -jnp.inf)
            parts.append(sj)
        parts.append(jnp.where(causal, s[:, n * blk:], -jnp.inf))
        return _softmax_numer(jnp.concatenate(parts, axis=1) if n else parts[0])

    def output(n, p):
        o_ref[n * blk:(n + 1) * blk, :] = _normalised_pv(p, vext_ref[0:(n + 1) * blk, :]).astype(o_ref.dtype)

    _staged(nb, scores, probs, output, reverse=True)


def _moba_attention(qkv, batch, seq, heads):
    t = qkv.shape[0]
    d = MOBA_HEAD_DIM
    return pl.pallas_call(
        functools.partial(_moba_attn_kernel, nb=seq // MOBA_BLOCK),
        grid=(batch, heads),
        in_specs=[pl.BlockSpec((seq, d), lambda b, h: (b, h)),
                  pl.BlockSpec((seq, d), lambda b, h: (b, heads + h)),
                  pl.BlockSpec((seq, d), lambda b, h: (b, 2 * heads + h))],
        out_specs=pl.BlockSpec((seq, d), lambda b, h: (b, h)),
        out_shape=jax.ShapeDtypeStruct((t, heads * d), BF16),
        scratch_shapes=[pltpu.VMEM((seq, 2 * LANES), BF16)],
        compiler_params=_cparams(2, 48),
        name="moba_attn",
    )(qkv, qkv, qkv)


def _swiglu_partial(x, wg, wu, wd):
    g = jnp.dot(x, wg, preferred_element_type=F32)
    u = jnp.dot(x, wu, preferred_element_type=F32)
    a = (g * jax.nn.sigmoid(g) * u).astype(BF16)
    return jnp.dot(a, wd, preferred_element_type=F32)


def _ffn_kernel(u_ref, wg_ref, wu_ref, wd_ref, o_ref):
    f = pl.program_id(1)
    @pl.when(f == 0)
    def _():
        o_ref[...] = jnp.zeros_like(o_ref)

    o_ref[...] += _swiglu_partial(u_ref[...], wg_ref[...].astype(BF16), wu_ref[...].astype(BF16),
                                  wd_ref[...].astype(BF16))


def _dense_ffn(u, w_gate_up, w_down, layer):
    t, d = u.shape
    dff = w_down.shape[1]
    tm, tf = 1024, 256
    nf = dff // tf
    return pl.pallas_call(
        _ffn_kernel,
        grid=(t // tm, nf),
        in_specs=[pl.BlockSpec((tm, d), lambda i, f: (i, 0)),
                  pl.BlockSpec((None, d, tf), lambda i, f: (layer, 0, f)),
                  pl.BlockSpec((None, d, tf), lambda i, f: (layer, 0, nf + f)),
                  pl.BlockSpec((None, tf, d), lambda i, f: (layer, f, 0))],
        out_specs=pl.BlockSpec((tm, d), lambda i, f: (i, 0)),
        out_shape=jax.ShapeDtypeStruct((t, d), F32),
        compiler_params=_cparams(2, 56),
        name="dense_ffn",
    )(u, w_gate_up, w_gate_up, w_down)


def _route_metadata(idx2, n_exp):
    rb, st = MOE_ROW_BLOCK, MOE_SUPER_BLOCKS
    t = idx2.shape[0]
    a = t * TOP_K
    n_items = (a // rb + n_exp - 1 + n_exp * (st - 1)) // st
    e_flat = idx2.reshape(a)
    onehot = (e_flat[:, None] == jnp.arange(n_exp, dtype=jnp.int32)[None, :]).astype(jnp.int32)
    csum = jnp.cumsum(onehot, axis=0)
    rank = jnp.sum((csum - onehot) * onehot, axis=1)
    counts = csum[-1]
    nsub = (counts + rb - 1) // rb
    sub_start = jnp.cumsum(nsub) - nsub
    dest = jnp.sum(onehot * (sub_start * rb)[None, :], axis=1) + rank
    p_rows = (a // rb + n_exp) * rb
    token_flat = jnp.arange(a, dtype=jnp.int32) // TOP_K
    row_token = jnp.zeros((p_rows,), jnp.int32).at[dest].set(token_flat)
    n_it = (nsub + st - 1) // st
    it_end = jnp.cumsum(n_it)
    it_start = it_end - n_it
    w = jnp.arange(n_items, dtype=jnp.int32)
    e_w = jnp.sum((it_end[None, :] <= w[:, None]).astype(jnp.int32), axis=1)
    active = e_w < n_exp
    e_c = jnp.minimum(e_w, n_exp - 1)
    local = w - it_start[e_c]
    item_nsub = jnp.where(active, jnp.clip(nsub[e_c] - local * st, 0, st), 0)
    item_row0 = jnp.where(active, (sub_start[e_c] + local * st) * rb, 0)
    e_last = jnp.max(jnp.where(n_it > 0, jnp.arange(n_exp, dtype=jnp.int32), 0))
    item_e = jnp.where(active, e_c, e_last)
    item_row0 = jnp.concatenate([item_row0, jnp.sum(nsub, keepdims=True) * rb])
    return (dest.astype(jnp.int32), row_token, item_e.astype(jnp.int32), item_row0.astype(jnp.int32),
            item_nsub.astype(jnp.int32), p_rows)


ROW_DMA_UNROLL = 8


def _row_copy(src_hbm, src_row, dst, dst_row, sem, chunks):
    pitch = _row_pitch(chunks)
    return pltpu.make_async_copy(src_hbm.at[pl.ds(src_row * pitch, chunks), :],
                                 dst.at[pl.ds(dst_row * pitch, chunks), :], sem)


def _rows_wait(src_hbm, dst, sem, rows, chunks):
    n = rows * chunks
    pltpu.make_async_copy(src_hbm.at[pl.ds(0, n), :], dst.at[pl.ds(0, n), :], sem).wait()


def _dispatch_kernel(tok_ref, u_hbm, o_ref, stage_ref, sem, *, rb, chunks):
    i = pl.program_id(0)
    pitch = _row_pitch(chunks)

    def issue(step, slot):
        def body(g, c):
            for k in range(ROW_DMA_UNROLL):
                r = g * ROW_DMA_UNROLL + k
                _row_copy(u_hbm, tok_ref[step * rb + r], stage_ref.at[slot], r, sem.at[slot],
                          chunks).start(priority=k % 2)
            return c
        lax.fori_loop(0, rb // ROW_DMA_UNROLL, body, 0)

    @pl.when(i == 0)
    def _():
        issue(0, 0)

    @pl.when(i + 1 < pl.num_programs(0))
    def _():
        issue(i + 1, (i + 1) % 2)

    slot = i % 2
    _rows_wait(u_hbm, stage_ref.at[slot], sem.at[slot], rb, chunks)
    for c in range(chunks):
        o_ref[:, c * LANES:(c + 1) * LANES] = stage_ref[slot, pl.ds(c, rb, stride=pitch), :].astype(o_ref.dtype)


def _dispatch(u_lin, row_token, d):
    p_rows = row_token.shape[0]
    rb = MOE_ROW_BLOCK
    chunks = d // LANES
    return pl.pallas_call(
        functools.partial(_dispatch_kernel, rb=rb, chunks=chunks),
        grid_spec=pltpu.PrefetchScalarGridSpec(
            num_scalar_prefetch=1,
            grid=(p_rows // rb,),
            in_specs=[pl.BlockSpec(memory_space=pl.ANY)],
            out_specs=pl.BlockSpec((rb, d), lambda i, tok: (i, 0)),
            scratch_shapes=[pltpu.VMEM((2, rb * _row_pitch(chunks), LANES), F32),
                            pltpu.SemaphoreType.DMA((2,))]),
        out_shape=jax.ShapeDtypeStruct((p_rows, d), BF16),
        compiler_params=_cparams(1, 40),
        name="moe_dispatch",
    )(row_token, u_lin)


def _moe_ffn_kernel(e_ref, row0_ref, nsub_ref, xs_hbm, wg_ref, wu_ref, wd_ref, ys_hbm,
                    x_ref, acc_ref, wgb_ref, wub_ref, wdb_ref, stage_ref, sem, *, rb, nf, n_items, chunks):
    w = pl.program_id(0)
    f = pl.program_id(1)
    nsub = nsub_ref[w]
    row0 = row0_ref[w]

    def for_range(n, fn):
        def body(r, c):
            fn(r)
            return c
        lax.fori_loop(0, n, body, 0)

    def load(r):
        return pltpu.make_async_copy(xs_hbm.at[pl.ds(pl.multiple_of(row0 + r * rb, rb), rb), :],
                                     x_ref.at[pl.ds(pl.multiple_of(r * rb, rb), rb), :], sem.at[0])

    pitch = _row_pitch(chunks)

    def store(row, slot):
        return pltpu.make_async_copy(
            stage_ref.at[slot], ys_hbm.at[pl.ds(pl.multiple_of(row * pitch, rb * pitch), rb * pitch), :],
            sem.at[1 + slot])

    @pl.when(f == 0)
    def _():
        for_range(nsub, lambda r: load(r).start())
        acc_ref[...] = jnp.zeros_like(acc_ref)
        for_range(nsub, lambda r: load(r).wait())

    @pl.when(nsub > 0)
    def _():
        wgb_ref[...] = wg_ref[...].astype(BF16)
        wub_ref[...] = wu_ref[...].astype(BF16)
        wdb_ref[...] = wd_ref[...].astype(BF16)

        def chunk(start, size):
            rows = pl.ds(pl.multiple_of(start, rb), size)
            acc_ref[rows, :] += _swiglu_partial(x_ref[rows, :], wgb_ref[...], wub_ref[...], wdb_ref[...])

        for_range(nsub // 2, lambda p: chunk(p * (2 * rb), 2 * rb))

        @pl.when(nsub % 2 == 1)
        def _():
            chunk((nsub - 1) * rb, rb)

    @pl.when(f == nf - 1)
    def _():
        def emit(r):
            slot = r % 2

            @pl.when(r >= 2)
            def _():
                store(row0 + (r - 2) * rb, slot).wait()

            _store_linear(stage_ref.at[slot], acc_ref[pl.ds(pl.multiple_of(r * rb, rb), rb), :], chunks)
            store(row0 + r * rb, slot).start()

        for_range(nsub, emit)

        @pl.when(nsub >= 2)
        def _():
            store(row0, nsub % 2).wait()

        @pl.when(nsub >= 1)
        def _():
            store(row0, (nsub - 1) % 2).wait()

    @pl.when((f == nf - 1) & (w == n_items - 1))
    def _():
        used = row0_ref[n_items]
        n_tail = (ys_hbm.shape[0] // pitch - used) // rb
        stage_ref[0] = jnp.zeros(stage_ref.shape[1:], F32)
        for_range(n_tail, lambda r: store(used + r * rb, 0).start())
        for_range(n_tail, lambda r: store(used + r * rb, 0).wait())


def _moe_ffn(xs, w_gate_up, w_down, layer, item_e, item_row0, item_nsub):
    p_rows, d = xs.shape
    dff = w_down.shape[2]
    rb, st = MOE_ROW_BLOCK, MOE_SUPER_BLOCKS
    n_items = item_e.shape[0]
    chunks = d // LANES
    tf = 256
    nf = dff // tf

    def f_eff(f, nsub, w):
        return jnp.where(nsub[w] > 0, f, nf - 1)

    return pl.pallas_call(
        functools.partial(_moe_ffn_kernel, rb=rb, nf=nf, n_items=n_items, chunks=chunks),
        grid_spec=pltpu.PrefetchScalarGridSpec(
            num_scalar_prefetch=3,
            grid=(n_items, nf),
            in_specs=[pl.BlockSpec(memory_space=pl.ANY),
                      pl.BlockSpec((None, None, d, tf), lambda w, f, e, r0, ns: (layer, e[w], 0, f_eff(f, ns, w))),
                      pl.BlockSpec((None, None, d, tf),
                                   lambda w, f, e, r0, ns: (layer, e[w], 0, nf + f_eff(f, ns, w))),
                      pl.BlockSpec((None, None, tf, d), lambda w, f, e, r0, ns: (layer, e[w], f_eff(f, ns, w), 0))],
            out_specs=pl.BlockSpec(memory_space=pl.ANY),
            scratch_shapes=[pltpu.VMEM((st * rb, d), BF16), pltpu.VMEM((st * rb, d), F32),
                            pltpu.VMEM((d, tf), BF16), pltpu.VMEM((d, tf), BF16), pltpu.VMEM((tf, d), BF16),
                            pltpu.VMEM((2, rb * _row_pitch(chunks), LANES), F32), pltpu.SemaphoreType.DMA((3,))]),
        out_shape=jax.ShapeDtypeStruct((p_rows * _row_pitch(chunks), LANES), F32),
        compiler_params=_cparams(2, 58),
        name="moe_ffn",
    )(item_e, item_row0, item_nsub, xs, w_gate_up, w_gate_up, w_down)


def _moe_combine_ln_kernel(*refs, alpha, emit_u, tm, chunks):
    if emit_u:
        (pos_ref, ys_hbm, wt_ref, x_ref, g_ref, lg_ref, lb_ref, sc_ref, sh_ref, xo_ref, uo_ref,
         stage_ref, y_ref, sem) = refs
        sc_sh = (sc_ref[0], sh_ref[0])
    else:
        pos_ref, ys_hbm, wt_ref, x_ref, g_ref, lg_ref, lb_ref, xo_ref, stage_ref, y_ref, sem = refs
        sc_sh = None
    i = pl.program_id(0)
    pitch = _row_pitch(chunks)

    def issue(step, slot):
        def body(g, c):
            for j in range(ROW_DMA_UNROLL // TOP_K):
                t = g * (ROW_DMA_UNROLL // TOP_K) + j
                for k in range(TOP_K):
                    _row_copy(ys_hbm, pos_ref[(step * tm + t) * TOP_K + k], stage_ref.at[slot, k], t,
                              sem.at[slot, k], chunks).start(priority=k % 2)
            return c
        lax.fori_loop(0, tm * TOP_K // ROW_DMA_UNROLL, body, 0)

    @pl.when(i == 0)
    def _():
        issue(0, 0)

    @pl.when(i + 1 < pl.num_programs(0))
    def _():
        issue(i + 1, (i + 1) % 2)

    slot = i % 2
    for k in range(TOP_K):
        _rows_wait(ys_hbm, stage_ref.at[slot, k], sem.at[slot, k], tm, chunks)
    w1 = wt_ref[:, 0:1]
    w2 = wt_ref[:, 1:2]
    for c in range(chunks):
        y_ref[:, c * LANES:(c + 1) * LANES] = (stage_ref[slot, 0, pl.ds(c, tm, stride=pitch), :] * w1
                                               + stage_ref[slot, 1, pl.ds(c, tm, stride=pitch), :] * w2)
    z = alpha * x_ref[...] + (1.0 + g_ref[0]) * y_ref[...]
    xn, u = _ln_modulate(z, lg_ref[0], lb_ref[0], sc_sh)
    xo_ref[...] = xn
    if emit_u:
        uo_ref[...] = u.astype(uo_ref.dtype)


def _moe_combine_ln(ys_lin, pos, wt, x, ada_rows, ln_g, ln_b, *, alpha, seq, layer, comp_g, nxt):
    t, d = x.shape
    tm = 256
    tpb = seq // tm
    chunks = d // LANES
    row = lambda l, comp: pl.BlockSpec((1, 1, d), lambda i, p: (_ada_row(l, i // tpb, comp), 0, 0))
    tile = pl.BlockSpec((tm, d), lambda i, p: (i, 0))
    lnp = pl.BlockSpec((1, 1, d), lambda i, p: (layer, 0, 0))
    in_specs = [pl.BlockSpec(memory_space=pl.ANY), pl.BlockSpec((tm, LANES), lambda i, p: (i, 0)), tile,
                row(layer, comp_g), lnp, lnp]
    args = [ys_lin, wt, x, ada_rows, ln_g, ln_b]
    out_specs = [tile]
    out_shape = [jax.ShapeDtypeStruct((t, d), F32)]
    if nxt is not None:
        in_specs += [row(nxt[0], nxt[1]), row(nxt[0], nxt[2])]
        args += [ada_rows, ada_rows]
        out_specs.append(tile)
        out_shape.append(jax.ShapeDtypeStruct((t, d), BF16))
    outs = pl.pallas_call(
        functools.partial(_moe_combine_ln_kernel, alpha=alpha, emit_u=nxt is not None, tm=tm, chunks=chunks),
        grid_spec=pltpu.PrefetchScalarGridSpec(
            num_scalar_prefetch=1,
            grid=(t // tm,),
            in_specs=in_specs, out_specs=out_specs,
            scratch_shapes=[pltpu.VMEM((2, TOP_K, tm * _row_pitch(chunks), LANES), F32), pltpu.VMEM((tm, d), F32),
                            pltpu.SemaphoreType.DMA((2, TOP_K))]),
        out_shape=out_shape,
        compiler_params=_cparams(1, 48),
        name="moe_combine_ln",
    )(pos, *args)
    return (outs[0], outs[1]) if nxt is not None else (outs[0], None)


def _rope_cos_sin(positions, dim):
    inv_freq = ROPE_THETA ** (-jnp.arange(0, dim, 2, dtype=F32) / dim)
    ang = positions.astype(F32).reshape(-1)[:, None] * inv_freq
    return jnp.cos(ang), jnp.sin(ang)


def _mla_rope_tables(positions):
    cos, sin = _rope_cos_sin(positions, MLA_ROPE)
    z = jnp.zeros_like(cos)
    return jnp.concatenate([cos, z, cos, z], axis=1), jnp.concatenate([-sin, z, sin, z], axis=1)


def _moba_rope_tables(positions, scale):
    cos, sin = _rope_cos_sin(positions, MOBA_ROT_DIM)
    t, half = cos.shape
    rest = LANES - 2 * half
    c = jnp.concatenate([cos, cos, jnp.ones((t, rest), F32)], axis=1)
    s1 = jnp.concatenate([jnp.zeros((t, half), F32), sin, jnp.zeros((t, rest), F32)], axis=1)
    s2 = jnp.concatenate([-sin, jnp.zeros((t, half + rest), F32)], axis=1)
    rot = jnp.concatenate([c, s1, s2], axis=1)
    ident = jnp.concatenate([jnp.ones((t, LANES), F32), jnp.zeros((t, 2 * LANES), F32)], axis=1)
    return jnp.stack([rot * scale, rot, ident])


def _spread_rope_cols(w_rope):
    half = MLA_ROPE // 2
    z = jnp.zeros(w_rope.shape[:-1] + (LANES // 2 - half,), w_rope.dtype)
    return jnp.concatenate([w_rope[..., :half], z, w_rope[..., half:], z], axis=-1)


def _mla_weights(w_down, w_uq, ql, kvl):
    k = w_uq.shape[0]
    heads = w_uq.shape[1] // (MLA_NOPE + MLA_ROPE)
    wd = jnp.concatenate([w_down[:, :ql + kvl], _spread_rope_cols(w_down[:, ql + kvl:])], axis=1).astype(BF16)
    wq = w_uq.reshape(k, heads, MLA_NOPE + MLA_ROPE)
    wq = jnp.concatenate([wq[..., :MLA_NOPE], _spread_rope_cols(wq[..., MLA_NOPE:])], axis=-1)
    return wd, wq.reshape(k, heads * MLA_HEAD_PAD).astype(BF16), heads


def kernel(x, c, positions, w_ada, b_ada, ln_mix_g, ln_mix_b, ln_ffn_g, ln_ffn_b, mla_w_down, mla_q_norm,
           mla_kv_norm, mla_w_uq, mla_w_ukv, mla_w_o, moba_w_qkv, moba_w_o, ffn_w_gate_up, ffn_w_down,
           moe_w_router, moe_w_gate_up, moe_w_down):
    batch, seq, d = x.shape
    depth = w_ada.shape[0]
    t = batch * seq
    alpha = (2.0 * depth) ** 0.25
    ql = mla_q_norm.shape[1]
    kvl = mla_kv_norm.shape[1]
    n_exp = moe_w_router.shape[2]

    ada_rows = _ada_all(c, w_ada, b_ada)
    cos_mla, sin_mla = _mla_rope_tables(positions)
    moba_tables = _moba_rope_tables(positions, MOBA_HEAD_DIM ** -0.5)
    mla_w_o_b = mla_w_o.astype(BF16)
    moba_w_o_b = moba_w_o.astype(BF16)
    ln3 = lambda p: p.reshape(depth, 1, d)
    ln_mix_g, ln_mix_b, ln_ffn_g, ln_ffn_b = ln3(ln_mix_g), ln3(ln_mix_b), ln3(ln_ffn_g), ln3(ln_ffn_b)
    q_norm3 = mla_q_norm.reshape(-1, 1, ql)
    kv_norm3 = mla_kv_norm.reshape(-1, 1, kvl)

    xf = x.reshape(t, d)
    u = _modulate(xf, ada_rows, 0, seq, 1, 0, BF16)
    for l in range(depth):
        j = l // 2
        moe_layer = l % 2 == 1
        if l % 2 == 0:
            wd_p, wq_p, heads = _mla_weights(mla_w_down[j], mla_w_uq[j], ql, kvl)
            cq, ckv, kr = _mla_down(u, wd_p, q_norm3, kv_norm3, j, cos_mla, sin_mla, ql, kvl)
            q = _mla_qup(cq, wq_p, cos_mla, sin_mla, (MLA_NOPE + MLA_ROPE) ** -0.5)
            kv = _matmul(ckv, mla_w_ukv, j, BF16, tm=1024, tn=1024)
            o = _mla_attention(q, kv, kr, batch, seq, heads)
            y = (o, mla_w_o_b, j)
        else:
            heads = moba_w_qkv.shape[2] // (3 * MOBA_HEAD_DIM)
            qkv = _moba_qkv(u, moba_w_qkv, j, moba_tables)
            o = _moba_attention(qkv, batch, seq, heads)
            y = (o, moba_w_o_b, j)
        ln_args = dict(alpha=alpha, seq=seq, layer=l)
        nxt = (l + 1, 1, 0) if l + 1 < depth else None
        if not moe_layer:
            xf, u = _resid_ln(xf, y, ada_rows, ln_mix_g, ln_mix_b, comp_g=2, nxt=(l, 4, 3), **ln_args)
            y = _dense_ffn(u, ffn_w_gate_up, ffn_w_down, j)
            outs = _resid_ln(xf, y, ada_rows, ln_ffn_g, ln_ffn_b, comp_g=5, nxt=nxt, **ln_args)
            xf, u = outs[0], (outs[1] if nxt is not None else None)
        else:
            w_router_pad = jnp.zeros((d, LANES), F32).at[:, :n_exp].set(moe_w_router[j])
            xf, u_lin, idx, wt = _resid_ln(xf, y, ada_rows, ln_mix_g, ln_mix_b, comp_g=2, nxt=(l, 4, 3),
                                           w_router_pad=w_router_pad, n_exp=n_exp, **ln_args)
            pos, row_token, item_e, item_row0, item_nsub, _ = _route_metadata(idx[:, :TOP_K], n_exp)
            xs = _dispatch(u_lin, row_token, d)
            ys_lin = _moe_ffn(xs, moe_w_gate_up, moe_w_down, j, item_e, item_row0, item_nsub)
            xf, u = _moe_combine_ln(ys_lin, pos, wt, xf, ada_rows, ln_ffn_g, ln_ffn_b, comp_g=5, nxt=nxt, **ln_args)
    return xf.reshape(batch, seq, d)
```

```python
import functools

import jax
import jax.numpy as jnp
from jax import lax
from jax.experimental import pallas as pl
from jax.experimental.pallas import tpu as pltpu

F32 = jnp.float32
BF16 = jnp.bfloat16

ROPE_THETA = 500000.0
LN_EPS = 1e-5
RMS_EPS = 1e-6
MLA_NOPE = 128
MLA_ROPE = 64
MLA_V = 128
MOBA_HEAD_DIM = 128
MOBA_ROT_DIM = 32
MOBA_BLOCK = 256
MOBA_TOPK = 3
TOP_K = 2

LANES = 128
SUBLANES = 8
MLA_HEAD_PAD = 2 * LANES

ADA_BATCH_PAD = SUBLANES
MOE_ROW_BLOCK = 256
MOE_SUPER_BLOCKS = 10
EPILOGUE_ROWS = 256
PROJ_LN_ROWS = 128


def _cparams(n_axes, vmem_mb):
    return pltpu.CompilerParams(dimension_semantics=("arbitrary",) * n_axes,
                                vmem_limit_bytes=vmem_mb * 1024 * 1024)


def _split_bf16(x):
    hi = x.astype(BF16)
    return hi, (x - hi.astype(F32)).astype(BF16)


def _ada_row(layer, batch, comp):
    return (layer * ADA_BATCH_PAD + batch) * 6 + comp


def _ada_kernel(c_ref, w_ref, b_ref, o_ref):
    c = c_ref[...]
    ca = (c * jax.nn.sigmoid(c)).astype(BF16)
    o_ref[...] = jnp.dot(ca, w_ref[...].astype(BF16), preferred_element_type=F32) + b_ref[...]


def _ada_all(c, w_ada, b_ada):
    depth, d, n6 = w_ada.shape
    b = c.shape[0]
    c_pad = jnp.zeros((ADA_BATCH_PAD, d), F32).at[:b].set(c)
    tn = 1024
    out = pl.pallas_call(
        _ada_kernel,
        grid=(depth, n6 // tn),
        in_specs=[pl.BlockSpec((ADA_BATCH_PAD, d), lambda l, j: (0, 0)),
                  pl.BlockSpec((None, d, tn), lambda l, j: (l, 0, j)),
                  pl.BlockSpec((None, 1, tn), lambda l, j: (l, 0, j))],
        out_specs=pl.BlockSpec((None, ADA_BATCH_PAD, tn), lambda l, j: (l, 0, j)),
        out_shape=jax.ShapeDtypeStruct((depth, ADA_BATCH_PAD, n6), F32),
        compiler_params=_cparams(2, 40),
        name="ada",
    )(c_pad, w_ada, b_ada.reshape(depth, 1, n6))
    return out.reshape(depth * ADA_BATCH_PAD * 6, 1, d)


def _ln_modulate(z, lg, lb, sc_sh):
    mu = jnp.mean(z, axis=-1, keepdims=True)
    zc = z - mu
    var = jnp.mean(zc * zc, axis=-1, keepdims=True)
    xn = zc * lax.rsqrt(var + LN_EPS) * lg + lb
    if sc_sh is None:
        return xn, None
    sc, sh = sc_sh
    return xn, xn * (1.0 + sc) + sh


def _row_pitch(chunks):
    return chunks + 1


def _store_linear(dst_ref, val, chunks):
    rows = val.shape[0]
    pitch = _row_pitch(chunks)
    for c in range(chunks):
        dst_ref[pl.ds(c, rows, stride=pitch), :] = val[:, c * LANES:(c + 1) * LANES]
    dst_ref[pl.ds(chunks, rows, stride=pitch), :] = jnp.zeros((rows, LANES), val.dtype)


def _top2_route(logits, n_exp):
    lane = lax.broadcasted_iota(jnp.int32, logits.shape, 1)
    lg = jnp.where(lane < n_exp, logits, -jnp.inf)
    m1 = jnp.max(lg, axis=-1, keepdims=True)
    i1 = jnp.min(jnp.where(lg == m1, lane, LANES), axis=-1, keepdims=True)
    lg2 = jnp.where(lane == i1, -jnp.inf, lg)
    m2 = jnp.max(lg2, axis=-1, keepdims=True)
    i2 = jnp.min(jnp.where(lg2 == m2, lane, LANES), axis=-1, keepdims=True)
    e = jnp.exp(m2 - m1)
    w1 = 1.0 / (1.0 + e)
    w2 = e / (1.0 + e)
    idx = jnp.where(lane == 0, i1, jnp.where(lane == 1, i2, 0))
    wt = jnp.where(lane == 0, w1, jnp.where(lane == 1, w2, 0.0))
    return idx, wt


def _resid_ln_kernel(*refs, alpha, mode, n_exp, chunks, proj):
    if proj:
        x_ref, a_ref, w_ref, g_ref, lg_ref, lb_ref = refs[:6]
        rest = refs[6:]
    else:
        x_ref, y_ref, g_ref, lg_ref, lb_ref = refs[:5]
        rest = refs[5:]
    if mode == "last":
        (xo_ref,) = rest
        sc_sh = None
    elif mode == "next":
        sc_ref, sh_ref, xo_ref, uo_ref = rest
        sc_sh = (sc_ref[0], sh_ref[0])
    else:
        sc_ref, sh_ref, wr_ref, xo_ref, uo_ref, idx_ref, wt_ref = rest
        sc_sh = (sc_ref[0], sh_ref[0])
    tm = x_ref.shape[0]
    rc = PROJ_LN_ROWS if proj else tm
    if mode == "route":
        w_cat = jnp.concatenate(_split_bf16(wr_ref[...]), axis=1)

    def project(t):
        return jnp.dot(a_ref[t * rc:(t + 1) * rc, :], w_ref[...], preferred_element_type=F32)

    def finish(t, y):
        rows = slice(t * rc, (t + 1) * rc)
        z = alpha * x_ref[rows, :] + (1.0 + g_ref[0]) * y
        xn, u = _ln_modulate(z, lg_ref[0], lb_ref[0], sc_sh)
        xo_ref[rows, :] = xn
        if mode == "next":
            uo_ref[rows, :] = u.astype(uo_ref.dtype)
        elif mode == "route":
            pitch = _row_pitch(chunks)
            _store_linear(uo_ref.at[t * rc * pitch:(t + 1) * rc * pitch, :], u, chunks)
            uh, ul = _split_bf16(u)
            r = (jnp.dot(uh, w_cat, preferred_element_type=F32) + jnp.dot(ul, w_cat, preferred_element_type=F32))
            idx, wt = _top2_route(r[:, :LANES] + r[:, LANES:], n_exp)
            idx_ref[rows, :] = idx
            wt_ref[rows, :] = wt

    if proj:
        _staged(tm // rc, project, finish)
    else:
        finish(0, y_ref[...].astype(F32))


def _resid_ln(x, y, ada_rows, ln_g, ln_b, *, alpha, seq, layer, comp_g, nxt, w_router_pad=None, n_exp=0):
    t, d = x.shape
    tm = 256
    tpb = seq // tm
    chunks = d // LANES
    proj = isinstance(y, tuple)
    mode = "last" if nxt is None else ("route" if w_router_pad is not None else "next")
    row = lambda l, comp: pl.BlockSpec((1, 1, d), lambda i: (_ada_row(l, i // tpb, comp), 0, 0))
    tile = pl.BlockSpec((tm, d), lambda i: (i, 0))
    lanes = pl.BlockSpec((tm, LANES), lambda i: (i, 0))
    lnp = pl.BlockSpec((1, 1, d), lambda i: (layer, 0, 0))
    if proj:
        a, w_stack, j = y
        k = a.shape[1]
        in_specs = [tile, pl.BlockSpec((tm, k), lambda i: (i, 0)), pl.BlockSpec((None, k, d), lambda i: (j, 0, 0))]
        args = [x, a, w_stack]
    else:
        in_specs = [tile, tile]
        args = [x, y]
    in_specs += [row(layer, comp_g), lnp, lnp]
    args += [ada_rows, ln_g, ln_b]
    out_specs = [tile]
    out_shape = [jax.ShapeDtypeStruct((t, d), F32)]
    if mode != "last":
        in_specs += [row(nxt[0], nxt[1]), row(nxt[0], nxt[2])]
        args += [ada_rows, ada_rows]
    if mode == "next":
        out_specs.append(tile)
        out_shape.append(jax.ShapeDtypeStruct((t, d), BF16))
    elif mode == "route":
        pitch = _row_pitch(chunks)
        in_specs.append(pl.BlockSpec((d, LANES), lambda i: (0, 0)))
        args.append(w_router_pad)
        out_specs += [pl.BlockSpec((tm * pitch, LANES), lambda i: (i, 0)), lanes, lanes]
        out_shape += [jax.ShapeDtypeStruct((t * pitch, LANES), F32),
                      jax.ShapeDtypeStruct((t, LANES), jnp.int32), jax.ShapeDtypeStruct((t, LANES), F32)]
    return pl.pallas_call(
        functools.partial(_resid_ln_kernel, alpha=alpha, mode=mode, n_exp=n_exp, chunks=chunks, proj=proj),
        grid=(t // tm,),
        in_specs=in_specs, out_specs=out_specs, out_shape=out_shape,
        compiler_params=_cparams(1, 56 if proj else 48),
        name=("proj_ln_" if proj else "resid_ln_") + mode,
    )(*args)


def _rms(x, g):
    ms = jnp.mean(x * x, axis=-1, keepdims=True)
    return x * lax.rsqrt(ms + RMS_EPS) * g


def _mla_down_kernel(*refs, ql, kvl, modulate):
    if modulate:
        u_ref, sc_ref, sh_ref, w_ref, qn_ref, kvn_ref, c_ref, s_ref, cq_ref, ckv_ref, kr_ref = refs
    else:
        u_ref, w_ref, qn_ref, kvn_ref, c_ref, s_ref, cq_ref, ckv_ref, kr_ref = refs
    rc = EPILOGUE_ROWS

    def matmul(t):
        u = u_ref[t * rc:(t + 1) * rc, :]
        if modulate:
            u = (u * (1.0 + sc_ref[0]) + sh_ref[0]).astype(BF16)
        return jnp.dot(u, w_ref[...], preferred_element_type=F32)

    def norm_rotate(t, acc):
        rows = slice(t * rc, (t + 1) * rc)
        cq_ref[rows, :] = _rms(acc[:, :ql], qn_ref[0]).astype(BF16)
        ckv_ref[rows, :] = _rms(acc[:, ql:ql + kvl], kvn_ref[0]).astype(BF16)
        xr = acc[:, ql + kvl:]
        kr_ref[rows, :] = (xr * c_ref[rows, :] + pltpu.roll(xr, LANES // 2, 1) * s_ref[rows, :]).astype(BF16)

    _staged(u_ref.shape[0] // rc, matmul, norm_rotate)


def _mla_down(u, w_perm, q_norm, kv_norm, layer, cos_t, sin_t, ql, kvl, mod=None):
    t, d = u.shape
    n = w_perm.shape[1]
    tm = 512
    nrm = lambda width: pl.BlockSpec((1, 1, width), lambda i: (layer, 0, 0))
    rows = lambda width: pl.BlockSpec((tm, width), lambda i: (i, 0))
    in_specs = [rows(d)]
    args = [u]
    if mod is not None:
        ada_rows, seq, ada_layer, comp_sc, comp_sh = mod
        tpb = seq // tm
        ada = lambda comp: pl.BlockSpec((1, 1, d), lambda i: (_ada_row(ada_layer, i // tpb, comp), 0, 0))
        in_specs += [ada(comp_sc), ada(comp_sh)]
        args += [ada_rows, ada_rows]
    in_specs += [pl.BlockSpec((d, n), lambda i: (0, 0)), nrm(ql), nrm(kvl), rows(LANES), rows(LANES)]
    args += [w_perm, q_norm, kv_norm, cos_t, sin_t]
    return pl.pallas_call(
        functools.partial(_mla_down_kernel, ql=ql, kvl=kvl, modulate=mod is not None),
        grid=(t // tm,),
        in_specs=in_specs,
        out_specs=[rows(ql), rows(kvl), rows(LANES)],
        out_shape=[jax.ShapeDtypeStruct((t, ql), BF16), jax.ShapeDtypeStruct((t, kvl), BF16),
                   jax.ShapeDtypeStruct((t, LANES), BF16)],
        compiler_params=_cparams(1, 48),
        name="mla_down",
    )(*args)


def _mla_up_kernel(cq_ref, ckv_ref, wq_ref, wkv_ref, c_ref, s_ref, q_ref, kv_ref, *, scale, heads):
    rc = EPILOGUE_ROWS

    def matmul(t):
        rows = slice(t * rc, (t + 1) * rc)
        return (jnp.dot(cq_ref[rows, :], wq_ref[...], preferred_element_type=F32),
                jnp.dot(ckv_ref[rows, :], wkv_ref[...], preferred_element_type=F32))

    def rotate(t, accs):
        acc, acc_kv = accs
        rows = slice(t * rc, (t + 1) * rc)
        kv_ref[rows, :] = acc_kv.astype(BF16)
        c = c_ref[rows, :]
        s = s_ref[rows, :]
        for h in range(heads):
            b0 = h * MLA_HEAD_PAD
            q_ref[rows, b0:b0 + LANES] = (acc[:, b0:b0 + LANES] * scale).astype(BF16)
            xr = acc[:, b0 + LANES:b0 + MLA_HEAD_PAD]
            q_ref[rows, b0 + LANES:b0 + MLA_HEAD_PAD] = (
                (xr * c + pltpu.roll(xr, LANES // 2, 1) * s) * scale).astype(BF16)

    _staged(cq_ref.shape[0] // rc, matmul, rotate)


def _mla_up(cq, ckv, wq_perm, wkv_stack, layer, cos_t, sin_t, scale):
    t, k = cq.shape
    n = wq_perm.shape[1]
    assert wkv_stack.shape[1:] == (ckv.shape[1], n)
    tm, tn = 1024, 1024
    rows = pl.BlockSpec((tm, LANES), lambda j, i: (i, 0))
    lat = pl.BlockSpec((tm, k), lambda j, i: (i, 0))
    out = pl.BlockSpec((tm, tn), lambda j, i: (i, j))
    return pl.pallas_call(
        functools.partial(_mla_up_kernel, scale=scale, heads=tn // MLA_HEAD_PAD),
        grid=(n // tn, t // tm),
        in_specs=[lat, lat, pl.BlockSpec((k, tn), lambda j, i: (0, j)),
                  pl.BlockSpec((None, k, tn), lambda j, i: (layer, 0, j)), rows, rows],
        out_specs=[out, out],
        out_shape=[jax.ShapeDtypeStruct((t, n), BF16), jax.ShapeDtypeStruct((t, n), BF16)],
        compiler_params=_cparams(2, 48),
        name="mla_up",
    )(cq, ckv, wq_perm, wkv_stack, cos_t, sin_t)


def _moba_qkv_kernel(u_ref, w_ref, tab_ref, o_ref, wb_ref, *, heads):
    @pl.when(pl.program_id(1) == 0)
    def _():
        wb_ref[...] = w_ref[...].astype(BF16)

    half = MOBA_ROT_DIM // 2
    rc = EPILOGUE_ROWS

    def matmul(t):
        return jnp.dot(u_ref[t * rc:(t + 1) * rc, :], wb_ref[...], preferred_element_type=F32)

    def rotate(t, acc):
        rows = slice(t * rc, (t + 1) * rc)
        c = tab_ref[rows, :LANES]
        s1 = tab_ref[rows, LANES:2 * LANES]
        s2 = tab_ref[rows, 2 * LANES:]
        for h in range(heads):
            x = acc[:, h * LANES:(h + 1) * LANES]
            r = x * c + pltpu.roll(x, half, 1) * s1 + pltpu.roll(x, LANES - half, 1) * s2
            o_ref[rows, h * LANES:(h + 1) * LANES] = r.astype(BF16)

    _staged(u_ref.shape[0] // rc, matmul, rotate)


def _moba_qkv(u, w_stack, layer, tables):
    t, k = u.shape
    n = w_stack.shape[2]
    tm, tn = 1024, 512
    tiles_per_sec = (n // 3) // tn
    return pl.pallas_call(
        functools.partial(_moba_qkv_kernel, heads=tn // LANES),
        grid=(n // tn, t // tm),
        in_specs=[pl.BlockSpec((tm, k), lambda j, i: (i, 0)),
                  pl.BlockSpec((None, k, tn), lambda j, i: (layer, 0, j)),
                  pl.BlockSpec((None, tm, 3 * LANES), lambda j, i: (j // tiles_per_sec, i, 0))],
        out_specs=pl.BlockSpec((tm, tn), lambda j, i: (i, j)),
        out_shape=jax.ShapeDtypeStruct((t, n), BF16),
        scratch_shapes=[pltpu.VMEM((k, tn), BF16)],
        compiler_params=_cparams(2, 48),
        name="moba_qkv",
    )(u, w_stack, tables)


_NT = (((1,), (1,)), ((), ()))


def _softmax_numer(s):
    m = jnp.max(s, axis=-1, keepdims=True)
    return jnp.exp(s - m).astype(BF16)


def _fill_values_ones(vext_ref, v):
    vext_ref[:, :LANES] = v
    vext_ref[:, LANES:] = jnp.ones_like(v)


def _normalised_pv(p, vext):
    o = jnp.dot(p, vext, preferred_element_type=F32)
    return o[:, :LANES] / o[:, LANES:LANES + 1]


def _staged(n_tiles, *stages, reverse=False):
    vals = {}
    for t in range(n_tiles + len(stages) - 1):
        for k, stage in enumerate(stages):
            if 0 <= t - k < n_tiles:
                tile = n_tiles - 1 - (t - k) if reverse else t - k
                vals[tile] = stage(tile) if k == 0 else stage(tile, vals[tile])


def _mla_attn_kernel(q_ref, kv_ref, kr_ref, o_ref, kfull_ref, vext_ref, *, tq):
    seq = q_ref.shape[0]
    kfull_ref[:, :LANES] = kv_ref[:, :LANES]
    kfull_ref[:, LANES:] = kr_ref[...]
    _fill_values_ones(vext_ref, kv_ref[:, LANES:])
    row = lax.broadcasted_iota(jnp.int32, (tq, tq), 0)
    col = lax.broadcasted_iota(jnp.int32, (tq, tq), 1)
    causal = col <= row

    def scores(n):
        q = q_ref[n * tq:(n + 1) * tq, :]
        return lax.dot_general(q, kfull_ref[0:(n + 1) * tq, :], _NT, preferred_element_type=F32)

    def probs(n, s):
        diag = jnp.where(causal, s[:, n * tq:], -jnp.inf)
        return _softmax_numer(jnp.concatenate([s[:, :n * tq], diag], axis=1) if n else diag)

    def output(n, p):
        o_ref[n * tq:(n + 1) * tq, :] = _normalised_pv(p, vext_ref[0:(n + 1) * tq, :]).astype(o_ref.dtype)

    _staged(seq // tq, scores, probs, output, reverse=True)


def _mla_attention(q, kv, kr, batch, seq, heads):
    t = q.shape[0]
    return pl.pallas_call(
        functools.partial(_mla_attn_kernel, tq=512),
        grid=(batch, heads),
        in_specs=[pl.BlockSpec((seq, MLA_HEAD_PAD), lambda b, h: (b, h)),
                  pl.BlockSpec((seq, MLA_NOPE + MLA_V), lambda b, h: (b, h)),
                  pl.BlockSpec((seq, LANES), lambda b, h: (b, 0))],
        out_specs=pl.BlockSpec((seq, MLA_V), lambda b, h: (b, h)),
        out_shape=jax.ShapeDtypeStruct((t, heads * MLA_V), BF16),
        scratch_shapes=[pltpu.VMEM((seq, MLA_HEAD_PAD), BF16), pltpu.VMEM((seq, 2 * LANES), BF16)],
        compiler_params=_cparams(2, 48),
        name="mla_attn",
    )(q, kv, kr)


def _moba_attn_kernel(q_ref, k_ref, v_ref, o_ref, vext_ref, *, nb):
    blk = MOBA_BLOCK
    seq = k_ref.shape[0]
    _fill_values_ones(vext_ref, v_ref[...])
    r = lax.broadcasted_iota(jnp.int32, (LANES, seq), 0)
    c = lax.broadcasted_iota(jnp.int32, (LANES, seq), 1)
    ind = jnp.where(c // blk == r, 1.0 / blk, 0.0).astype(BF16)
    km = jnp.dot(ind, k_ref[...], preferred_element_type=F32)
    kmh, kml = _split_bf16(km)
    row = lax.broadcasted_iota(jnp.int32, (blk, blk), 0)
    col = lax.broadcasted_iota(jnp.int32, (blk, blk), 1)
    causal = col <= row
    lane = lax.broadcasted_iota(jnp.int32, (blk, LANES), 1)

    def scores(n):
        q = q_ref[n * blk:(n + 1) * blk, :]
        s = lax.dot_general(q, k_ref[0:(n + 1) * blk, :], _NT, preferred_element_type=F32)
        if n <= MOBA_TOPK:
            return s, None
        gate_t = (lax.dot_general(kmh, q, _NT, preferred_element_type=F32)
                  + lax.dot_general(kml, q, _NT, preferred_element_type=F32))[:SUBLANES, :]
        return s, gate_t

    def probs(n, s_gate):
        s, gate_t = s_gate
        if gate_t is not None:
            blk_id = lax.broadcasted_iota(jnp.int32, gate_t.shape, 0)
            keep_t = jnp.zeros(gate_t.shape, F32)
            for j in range(n):
                gj = gate_t[j:j + 1, :]
                beats = (blk_id < n) & ((gate_t > gj) | ((gate_t == gj) & (blk_id < j)))
                n_beats = jnp.sum(beats.astype(F32), axis=0, keepdims=True)
                keep_t = jnp.where((blk_id == j) & (n_beats < MOBA_TOPK), 1.0, keep_t)
            keep = jnp.concatenate([keep_t, jnp.zeros((LANES - SUBLANES, blk), F32)], axis=0).T
        parts = []
        for j in range(n):
            sj = s[:, j * blk:(j + 1) * blk]
            if gate_t is not None:
                sj = jnp.where(keep[:, j:j + 1] > 0.5, sj, -jnp.inf)
            parts.append(sj)
        parts.append(jnp.where(causal, s[:, n * blk:], -jnp.inf))
        return _softmax_numer(jnp.concatenate(parts, axis=1) if n else parts[0])

    def output(n, p):
        o_ref[n * blk:(n + 1) * blk, :] = _normalised_pv(p, vext_ref[0:(n + 1) * blk, :]).astype(o_ref.dtype)

    _staged(nb, scores, probs, output, reverse=True)


def _moba_attention(qkv, batch, seq, heads):
    t = qkv.shape[0]
    d = MOBA_HEAD_DIM
    return pl.pallas_call(
        functools.partial(_moba_attn_kernel, nb=seq // MOBA_BLOCK),
        grid=(batch, heads),
        in_specs=[pl.BlockSpec((seq, d), lambda b, h: (b, h)),
                  pl.BlockSpec((seq, d), lambda b, h: (b, heads + h)),
                  pl.BlockSpec((seq, d), lambda b, h: (b, 2 * heads + h))],
        out_specs=pl.BlockSpec((seq, d), lambda b, h: (b, h)),
        out_shape=jax.ShapeDtypeStruct((t, heads * d), BF16),
        scratch_shapes=[pltpu.VMEM((seq, 2 * LANES), BF16)],
        compiler_params=_cparams(2, 48),
        name="moba_attn",
    )(qkv, qkv, qkv)


def _swiglu_partial(x, wg, wu, wd):
    g = jnp.dot(x, wg, preferred_element_type=F32)
    u = jnp.dot(x, wu, preferred_element_type=F32)
    a = (g * jax.nn.sigmoid(g) * u).astype(BF16)
    return jnp.dot(a, wd, preferred_element_type=F32)


def _ffn_kernel(u_ref, wg_ref, wu_ref, wd_ref, o_ref):
    f = pl.program_id(1)
    @pl.when(f == 0)
    def _():
        o_ref[...] = jnp.zeros_like(o_ref)

    o_ref[...] += _swiglu_partial(u_ref[...], wg_ref[...].astype(BF16), wu_ref[...].astype(BF16),
                                  wd_ref[...].astype(BF16))


def _dense_ffn(u, w_gate_up, w_down, layer):
    t, d = u.shape
    dff = w_down.shape[1]
    tm, tf = 1024, 256
    nf = dff // tf
    return pl.pallas_call(
        _ffn_kernel,
        grid=(t // tm, nf),
        in_specs=[pl.BlockSpec((tm, d), lambda i, f: (i, 0)),
                  pl.BlockSpec((None, d, tf), lambda i, f: (layer, 0, f)),
                  pl.BlockSpec((None, d, tf), lambda i, f: (layer, 0, nf + f)),
                  pl.BlockSpec((None, tf, d), lambda i, f: (layer, f, 0))],
        out_specs=pl.BlockSpec((tm, d), lambda i, f: (i, 0)),
        out_shape=jax.ShapeDtypeStruct((t, d), F32),
        compiler_params=_cparams(2, 56),
        name="dense_ffn",
    )(u, w_gate_up, w_gate_up, w_down)


def _route_metadata(idx2, n_exp):
    rb, st = MOE_ROW_BLOCK, MOE_SUPER_BLOCKS
    t = idx2.shape[0]
    a = t * TOP_K
    n_items = (a // rb + n_exp - 1 + n_exp * (st - 1)) // st
    e_flat = idx2.reshape(a)
    onehot = (e_flat[:, None] == jnp.arange(n_exp, dtype=jnp.int32)[None, :]).astype(jnp.int32)
    csum = jnp.cumsum(onehot, axis=0)
    rank = jnp.sum((csum - onehot) * onehot, axis=1)
    counts = csum[-1]
    nsub = (counts + rb - 1) // rb
    sub_start = jnp.cumsum(nsub) - nsub
    dest = jnp.sum(onehot * (sub_start * rb)[None, :], axis=1) + rank
    p_rows = (a // rb + n_exp) * rb
    token_flat = jnp.arange(a, dtype=jnp.int32) // TOP_K
    row_token = jnp.zeros((p_rows,), jnp.int32).at[dest].set(token_flat)
    n_it = (nsub + st - 1) // st
    it_end = jnp.cumsum(n_it)
    it_start = it_end - n_it
    w = jnp.arange(n_items, dtype=jnp.int32)
    e_w = jnp.sum((it_end[None, :] <= w[:, None]).astype(jnp.int32), axis=1)
    active = e_w < n_exp
    e_c = jnp.minimum(e_w, n_exp - 1)
    local = w - it_start[e_c]
    item_nsub = jnp.where(active, jnp.clip(nsub[e_c] - local * st, 0, st), 0)
    item_row0 = jnp.where(active, (sub_start[e_c] + local * st) * rb, 0)
    e_last = jnp.max(jnp.where(n_it > 0, jnp.arange(n_exp, dtype=jnp.int32), 0))
    item_e = jnp.where(active, e_c, e_last)
    item_row0 = jnp.concatenate([item_row0, jnp.sum(nsub, keepdims=True) * rb])
    return (dest.astype(jnp.int32), row_token, item_e.astype(jnp.int32), item_row0.astype(jnp.int32),
            item_nsub.astype(jnp.int32), p_rows)


ROW_DMA_UNROLL = 8


def _row_copy(src_hbm, src_row, dst, dst_row, sem, chunks):
    pitch = _row_pitch(chunks)
    return pltpu.make_async_copy(src_hbm.at[pl.ds(src_row * pitch, chunks), :],
                                 dst.at[pl.ds(dst_row * pitch, chunks), :], sem)


def _rows_wait(src_hbm, dst, sem, rows, chunks):
    n = rows * chunks
    pltpu.make_async_copy(src_hbm.at[pl.ds(0, n), :], dst.at[pl.ds(0, n), :], sem).wait()


def _dispatch_kernel(tok_ref, u_hbm, o_ref, stage_ref, sem, *, rb, chunks):
    i = pl.program_id(0)
    pitch = _row_pitch(chunks)

    def issue(step, slot):
        def body(g, c):
            for k in range(ROW_DMA_UNROLL):
                r = g * ROW_DMA_UNROLL + k
                _row_copy(u_hbm, tok_ref[step * rb + r], stage_ref.at[slot], r, sem.at[slot],
                          chunks).start(priority=k % 2)
            return c
        lax.fori_loop(0, rb // ROW_DMA_UNROLL, body, 0)

    @pl.when(i == 0)
    def _():
        issue(0, 0)

    @pl.when(i + 1 < pl.num_programs(0))
    def _():
        issue(i + 1, (i + 1) % 2)

    slot = i % 2
    _rows_wait(u_hbm, stage_ref.at[slot], sem.at[slot], rb, chunks)
    for c in range(chunks):
        o_ref[:, c * LANES:(c + 1) * LANES] = stage_ref[slot, pl.ds(c, rb, stride=pitch), :].astype(o_ref.dtype)


def _dispatch(u_lin, row_token, d):
    p_rows = row_token.shape[0]
    rb = MOE_ROW_BLOCK
    chunks = d // LANES
    return pl.pallas_call(
        functools.partial(_dispatch_kernel, rb=rb, chunks=chunks),
        grid_spec=pltpu.PrefetchScalarGridSpec(
            num_scalar_prefetch=1,
            grid=(p_rows // rb,),
            in_specs=[pl.BlockSpec(memory_space=pl.ANY)],
            out_specs=pl.BlockSpec((rb, d), lambda i, tok: (i, 0)),
            scratch_shapes=[pltpu.VMEM((2, rb * _row_pitch(chunks), LANES), F32),
                            pltpu.SemaphoreType.DMA((2,))]),
        out_shape=jax.ShapeDtypeStruct((p_rows, d), BF16),
        compiler_params=_cparams(1, 40),
        name="moe_dispatch",
    )(row_token, u_lin)


def _moe_ffn_kernel(e_ref, row0_ref, nsub_ref, xs_hbm, wg_ref, wu_ref, wd_ref, ys_hbm,
                    x_ref, acc_ref, wgb_ref, wub_ref, wdb_ref, stage_ref, sem, *, rb, nf, n_items, chunks):
    w = pl.program_id(0)
    f = pl.program_id(1)
    nsub = nsub_ref[w]
    row0 = row0_ref[w]

    def for_range(n, fn):
        def body(r, c):
            fn(r)
            return c
        lax.fori_loop(0, n, body, 0)

    def load(r):
        return pltpu.make_async_copy(xs_hbm.at[pl.ds(pl.multiple_of(row0 + r * rb, rb), rb), :],
                                     x_ref.at[pl.ds(pl.multiple_of(r * rb, rb), rb), :], sem.at[0])

    pitch = _row_pitch(chunks)

    def store(row, slot):
        return pltpu.make_async_copy(
            stage_ref.at[slot], ys_hbm.at[pl.ds(pl.multiple_of(row * pitch, rb * pitch), rb * pitch), :],
            sem.at[1 + slot])

    @pl.when(f == 0)
    def _():
        for_range(nsub, lambda r: load(r).start())
        acc_ref[...] = jnp.zeros_like(acc_ref)
        for_range(nsub, lambda r: load(r).wait())

    @pl.when(nsub > 0)
    def _():
        wgb_ref[...] = wg_ref[...].astype(BF16)
        wub_ref[...] = wu_ref[...].astype(BF16)
        wdb_ref[...] = wd_ref[...].astype(BF16)

        def chunk(start, size):
            rows = pl.ds(pl.multiple_of(start, rb), size)
            acc_ref[rows, :] += _swiglu_partial(x_ref[rows, :], wgb_ref[...], wub_ref[...], wdb_ref[...])

        for_range(nsub // 2, lambda p: chunk(p * (2 * rb), 2 * rb))

        @pl.when(nsub % 2 == 1)
        def _():
            chunk((nsub - 1) * rb, rb)

    @pl.when(f == nf - 1)
    def _():
        def emit(r):
            slot = r % 2

            @pl.when(r >= 2)
            def _():
                store(row0 + (r - 2) * rb, slot).wait()

            _store_linear(stage_ref.at[slot], acc_ref[pl.ds(pl.multiple_of(r * rb, rb), rb), :], chunks)
            store(row0 + r * rb, slot).start()

        for_range(nsub, emit)

        @pl.when(nsub >= 2)
        def _():
            store(row0, nsub % 2).wait()

        @pl.when(nsub >= 1)
        def _():
            store(row0, (nsub - 1) % 2).wait()

    @pl.when((f == nf - 1) & (w == n_items - 1))
    def _():
        used = row0_ref[n_items]
        n_tail = (ys_hbm.shape[0] // pitch - used) // rb
        stage_ref[0] = jnp.zeros(stage_ref.shape[1:], F32)
        for_range(n_tail, lambda r: store(used + r * rb, 0).start())
        for_range(n_tail, lambda r: store(used + r * rb, 0).wait())


def _moe_ffn(xs, w_gate_up, w_down, layer, item_e, item_row0, item_nsub):
    p_rows, d = xs.shape
    dff = w_down.shape[2]
    rb, st = MOE_ROW_BLOCK, MOE_SUPER_BLOCKS
    n_items = item_e.shape[0]
    chunks = d // LANES
    tf = 256
    nf = dff // tf

    def f_eff(f, nsub, w):
        return jnp.where(nsub[w] > 0, f, nf - 1)

    return pl.pallas_call(
        functools.partial(_moe_ffn_kernel, rb=rb, nf=nf, n_items=n_items, chunks=chunks),
        grid_spec=pltpu.PrefetchScalarGridSpec(
            num_scalar_prefetch=3,
            grid=(n_items, nf),
            in_specs=[pl.BlockSpec(memory_space=pl.ANY),
                      pl.BlockSpec((None, None, d, tf), lambda w, f, e, r0, ns: (layer, e[w], 0, f_eff(f, ns, w))),
                      pl.BlockSpec((None, None, d, tf),
                                   lambda w, f, e, r0, ns: (layer, e[w], 0, nf + f_eff(f, ns, w))),
                      pl.BlockSpec((None, None, tf, d), lambda w, f, e, r0, ns: (layer, e[w], f_eff(f, ns, w), 0))],
            out_specs=pl.BlockSpec(memory_space=pl.ANY),
            scratch_shapes=[pltpu.VMEM((st * rb, d), BF16), pltpu.VMEM((st * rb, d), F32),
                            pltpu.VMEM((d, tf), BF16), pltpu.VMEM((d, tf), BF16), pltpu.VMEM((tf, d), BF16),
                            pltpu.VMEM((2, rb * _row_pitch(chunks), LANES), F32), pltpu.SemaphoreType.DMA((3,))]),
        out_shape=jax.ShapeDtypeStruct((p_rows * _row_pitch(chunks), LANES), F32),
        compiler_params=_cparams(2, 58),
        name="moe_ffn",
    )(item_e, item_row0, item_nsub, xs, w_gate_up, w_gate_up, w_down)


def _moe_combine_ln_kernel(*refs, alpha, emit_u, tm, chunks):
    if emit_u:
        (pos_ref, ys_hbm, wt_ref, x_ref, g_ref, lg_ref, lb_ref, sc_ref, sh_ref, xo_ref, uo_ref,
         stage_ref, y_ref, sem) = refs
        sc_sh = (sc_ref[0], sh_ref[0])
    else:
        pos_ref, ys_hbm, wt_ref, x_ref, g_ref, lg_ref, lb_ref, xo_ref, stage_ref, y_ref, sem = refs
        sc_sh = None
    i = pl.program_id(0)
    pitch = _row_pitch(chunks)

    def issue(step, slot):
        def body(g, c):
            for j in range(ROW_DMA_UNROLL // TOP_K):
                t = g * (ROW_DMA_UNROLL // TOP_K) + j
                for k in range(TOP_K):
                    _row_copy(ys_hbm, pos_ref[(step * tm + t) * TOP_K + k], stage_ref.at[slot, k], t,
                              sem.at[slot, k], chunks).start(priority=k % 2)
            return c
        lax.fori_loop(0, tm * TOP_K // ROW_DMA_UNROLL, body, 0)

    @pl.when(i == 0)
    def _():
        issue(0, 0)

    @pl.when(i + 1 < pl.num_programs(0))
    def _():
        issue(i + 1, (i + 1) % 2)

    slot = i % 2
    for k in range(TOP_K):
        _rows_wait(ys_hbm, stage_ref.at[slot, k], sem.at[slot, k], tm, chunks)
    w1 = wt_ref[:, 0:1]
    w2 = wt_ref[:, 1:2]
    for c in range(chunks):
        y_ref[:, c * LANES:(c + 1) * LANES] = (stage_ref[slot, 0, pl.ds(c, tm, stride=pitch), :] * w1
                                               + stage_ref[slot, 1, pl.ds(c, tm, stride=pitch), :] * w2)
    z = alpha * x_ref[...] + (1.0 + g_ref[0]) * y_ref[...]
    xn, u = _ln_modulate(z, lg_ref[0], lb_ref[0], sc_sh)
    xo_ref[...] = xn
    if emit_u:
        uo_ref[...] = u.astype(uo_ref.dtype)


def _moe_combine_ln(ys_lin, pos, wt, x, ada_rows, ln_g, ln_b, *, alpha, seq, layer, comp_g, nxt):
    t, d = x.shape
    tm = 256
    tpb = seq // tm
    chunks = d // LANES
    row = lambda l, comp: pl.BlockSpec((1, 1, d), lambda i, p: (_ada_row(l, i // tpb, comp), 0, 0))
    tile = pl.BlockSpec((tm, d), lambda i, p: (i, 0))
    lnp = pl.BlockSpec((1, 1, d), lambda i, p: (layer, 0, 0))
    in_specs = [pl.BlockSpec(memory_space=pl.ANY), pl.BlockSpec((tm, LANES), lambda i, p: (i, 0)), tile,
                row(layer, comp_g), lnp, lnp]
    args = [ys_lin, wt, x, ada_rows, ln_g, ln_b]
    out_specs = [tile]
    out_shape = [jax.ShapeDtypeStruct((t, d), F32)]
    if nxt is not None:
        in_specs += [row(nxt[0], nxt[1]), row(nxt[0], nxt[2])]
        args += [ada_rows, ada_rows]
        out_specs.append(tile)
        out_shape.append(jax.ShapeDtypeStruct((t, d), BF16))
    outs = pl.pallas_call(
        functools.partial(_moe_combine_ln_kernel, alpha=alpha, emit_u=nxt is not None, tm=tm, chunks=chunks),
        grid_spec=pltpu.PrefetchScalarGridSpec(
            num_scalar_prefetch=1,
            grid=(t // tm,),
            in_specs=in_specs, out_specs=out_specs,
            scratch_shapes=[pltpu.VMEM((2, TOP_K, tm * _row_pitch(chunks), LANES), F32), pltpu.VMEM((tm, d), F32),
                            pltpu.SemaphoreType.DMA((2, TOP_K))]),
        out_shape=out_shape,
        compiler_params=_cparams(1, 48),
        name="moe_combine_ln",
    )(pos, *args)
    return (outs[0], outs[1]) if nxt is not None else (outs[0], None)


def _rope_cos_sin(positions, dim):
    inv_freq = ROPE_THETA ** (-jnp.arange(0, dim, 2, dtype=F32) / dim)
    ang = positions.astype(F32).reshape(-1)[:, None] * inv_freq
    return jnp.cos(ang), jnp.sin(ang)


def _mla_rope_tables(positions):
    cos, sin = _rope_cos_sin(positions, MLA_ROPE)
    z = jnp.zeros_like(cos)
    return jnp.concatenate([cos, z, cos, z], axis=1), jnp.concatenate([-sin, z, sin, z], axis=1)


def _moba_rope_tables(positions, scale):
    cos, sin = _rope_cos_sin(positions, MOBA_ROT_DIM)
    t, half = cos.shape
    rest = LANES - 2 * half
    c = jnp.concatenate([cos, cos, jnp.ones((t, rest), F32)], axis=1)
    s1 = jnp.concatenate([jnp.zeros((t, half), F32), sin, jnp.zeros((t, rest), F32)], axis=1)
    s2 = jnp.concatenate([-sin, jnp.zeros((t, half + rest), F32)], axis=1)
    rot = jnp.concatenate([c, s1, s2], axis=1)
    ident = jnp.concatenate([jnp.ones((t, LANES), F32), jnp.zeros((t, 2 * LANES), F32)], axis=1)
    return jnp.stack([rot * scale, rot, ident])


def _spread_rope_cols(w_rope):
    half = MLA_ROPE // 2
    z = jnp.zeros(w_rope.shape[:-1] + (LANES // 2 - half,), w_rope.dtype)
    return jnp.concatenate([w_rope[..., :half], z, w_rope[..., half:], z], axis=-1)


def _mla_weights(w_down, w_uq, ql, kvl):
    k = w_uq.shape[0]
    heads = w_uq.shape[1] // (MLA_NOPE + MLA_ROPE)
    wd = jnp.concatenate([w_down[:, :ql + kvl], _spread_rope_cols(w_down[:, ql + kvl:])], axis=1).astype(BF16)
    wq = w_uq.reshape(k, heads, MLA_NOPE + MLA_ROPE)
    wq = jnp.concatenate([wq[..., :MLA_NOPE], _spread_rope_cols(wq[..., MLA_NOPE:])], axis=-1)
    return wd, wq.reshape(k, heads * MLA_HEAD_PAD).astype(BF16), heads


def kernel(x, c, positions, w_ada, b_ada, ln_mix_g, ln_mix_b, ln_ffn_g, ln_ffn_b, mla_w_down, mla_q_norm,
           mla_kv_norm, mla_w_uq, mla_w_ukv, mla_w_o, moba_w_qkv, moba_w_o, ffn_w_gate_up, ffn_w_down,
           moe_w_router, moe_w_gate_up, moe_w_down):
    batch, seq, d = x.shape
    depth = w_ada.shape[0]
    t = batch * seq
    alpha = (2.0 * depth) ** 0.25
    ql = mla_q_norm.shape[1]
    kvl = mla_kv_norm.shape[1]
    n_exp = moe_w_router.shape[2]

    ada_rows = _ada_all(c, w_ada, b_ada)
    cos_mla, sin_mla = _mla_rope_tables(positions)
    moba_tables = _moba_rope_tables(positions, MOBA_HEAD_DIM ** -0.5)
    mla_w_ukv_b = mla_w_ukv.astype(BF16)
    mla_w_o_b = mla_w_o.astype(BF16)
    moba_w_o_b = moba_w_o.astype(BF16)
    ln3 = lambda p: p.reshape(depth, 1, d)
    ln_mix_g, ln_mix_b, ln_ffn_g, ln_ffn_b = ln3(ln_mix_g), ln3(ln_mix_b), ln3(ln_ffn_g), ln3(ln_ffn_b)
    q_norm3 = mla_q_norm.reshape(-1, 1, ql)
    kv_norm3 = mla_kv_norm.reshape(-1, 1, kvl)

    xf = x.reshape(t, d)
    u = None
    for l in range(depth):
        j = l // 2
        moe_layer = l % 2 == 1
        if l % 2 == 0:
            wd_p, wq_p, heads = _mla_weights(mla_w_down[j], mla_w_uq[j], ql, kvl)
            first = (xf, (ada_rows, seq, l, 1, 0)) if u is None else (u, None)
            cq, ckv, kr = _mla_down(first[0], wd_p, q_norm3, kv_norm3, j, cos_mla, sin_mla, ql, kvl, mod=first[1])
            q, kv = _mla_up(cq, ckv, wq_p, mla_w_ukv_b, j, cos_mla, sin_mla, (MLA_NOPE + MLA_ROPE) ** -0.5)
            o = _mla_attention(q, kv, kr, batch, seq, heads)
            y = (o, mla_w_o_b, j)
        else:
            heads = moba_w_qkv.shape[2] // (3 * MOBA_HEAD_DIM)
            qkv = _moba_qkv(u, moba_w_qkv, j, moba_tables)
            o = _moba_attention(qkv, batch, seq, heads)
            y = (o, moba_w_o_b, j)
        ln_args = dict(alpha=alpha, seq=seq, layer=l)
        nxt = (l + 1, 1, 0) if l + 1 < depth else None
        if not moe_layer:
            xf, u = _resid_ln(xf, y, ada_rows, ln_mix_g, ln_mix_b, comp_g=2, nxt=(l, 4, 3), **ln_args)
            y = _dense_ffn(u, ffn_w_gate_up, ffn_w_down, j)
            outs = _resid_ln(xf, y, ada_rows, ln_ffn_g, ln_ffn_b, comp_g=5, nxt=nxt, **ln_args)
            xf, u = outs[0], (outs[1] if nxt is not None else None)
        else:
            w_router_pad = jnp.zeros((d, LANES), F32).at[:, :n_exp].set(moe_w_router[j])
            xf, u_lin, idx, wt = _resid_ln(xf, y, ada_rows, ln_mix_g, ln_mix_b, comp_g=2, nxt=(l, 4, 3),
                                           w_router_pad=w_router_pad, n_exp=n_exp, **ln_args)
            pos, row_token, item_e, item_row0, item_nsub, _ = _route_metadata(idx[:, :TOP_K], n_exp)
            xs = _dispatch(u_lin, row_token, d)
            ys_lin = _moe_ffn(xs, moe_w_gate_up, moe_w_down, j, item_e, item_row0, item_nsub)
            xf, u = _moe_combine_ln(ys_lin, pos, wt, xf, ada_rows, ln_ffn_g, ln_ffn_b, comp_g=5, nxt=nxt, **ln_args)
    return xf.reshape(batch, seq, d)
```

```python
import functools

import jax
import jax.numpy as jnp
from jax import lax
from jax.experimental import pallas as pl
from jax.experimental.pallas import tpu as pltpu

F32 = jnp.float32
BF16 = jnp.bfloat16

ROPE_THETA = 500000.0
LN_EPS = 1e-5
RMS_EPS = 1e-6
MLA_NOPE = 128
MLA_ROPE = 64
MLA_V = 128
MOBA_HEAD_DIM = 128
MOBA_ROT_DIM = 32
MOBA_BLOCK = 256
MOBA_TOPK = 3
TOP_K = 2

LANES = 128
SUBLANES = 8
MLA_HEAD_PAD = 2 * LANES

ADA_BATCH_PAD = SUBLANES
MOE_ROW_BLOCK = 256
MOE_SUPER_BLOCKS = 10
EPILOGUE_ROWS = 256
PROJ_LN_ROWS = 128


def _cparams(n_axes, vmem_mb):
    return pltpu.CompilerParams(dimension_semantics=("arbitrary",) * n_axes,
                                vmem_limit_bytes=vmem_mb * 1024 * 1024)


def _split_bf16(x):
    hi = x.astype(BF16)
    return hi, (x - hi.astype(F32)).astype(BF16)


def _ada_row(layer, batch, comp):
    return (layer * ADA_BATCH_PAD + batch) * 6 + comp


def _ada_kernel(c_ref, w_ref, b_ref, o_ref):
    c = c_ref[...]
    ca = (c * jax.nn.sigmoid(c)).astype(BF16)
    o_ref[...] = jnp.dot(ca, w_ref[...].astype(BF16), preferred_element_type=F32) + b_ref[...]


def _ada_all(c, w_ada, b_ada):
    depth, d, n6 = w_ada.shape
    b = c.shape[0]
    c_pad = jnp.zeros((ADA_BATCH_PAD, d), F32).at[:b].set(c)
    tn = 1024
    out = pl.pallas_call(
        _ada_kernel,
        grid=(depth, n6 // tn),
        in_specs=[pl.BlockSpec((ADA_BATCH_PAD, d), lambda l, j: (0, 0)),
                  pl.BlockSpec((None, d, tn), lambda l, j: (l, 0, j)),
                  pl.BlockSpec((None, 1, tn), lambda l, j: (l, 0, j))],
        out_specs=pl.BlockSpec((None, ADA_BATCH_PAD, tn), lambda l, j: (l, 0, j)),
        out_shape=jax.ShapeDtypeStruct((depth, ADA_BATCH_PAD, n6), F32),
        compiler_params=_cparams(2, 40),
        name="ada",
    )(c_pad, w_ada, b_ada.reshape(depth, 1, n6))
    return out.reshape(depth * ADA_BATCH_PAD * 6, 1, d)


def _ln_modulate(z, lg, lb, sc_sh):
    mu = jnp.mean(z, axis=-1, keepdims=True)
    zc = z - mu
    var = jnp.mean(zc * zc, axis=-1, keepdims=True)
    xn = zc * lax.rsqrt(var + LN_EPS) * lg + lb
    if sc_sh is None:
        return xn, None
    sc, sh = sc_sh
    return xn, xn * (1.0 + sc) + sh


def _row_pitch(chunks):
    return chunks + 1


def _store_linear(dst_ref, val, chunks):
    rows = val.shape[0]
    pitch = _row_pitch(chunks)
    for c in range(chunks):
        dst_ref[pl.ds(c, rows, stride=pitch), :] = val[:, c * LANES:(c + 1) * LANES]
    dst_ref[pl.ds(chunks, rows, stride=pitch), :] = jnp.zeros((rows, LANES), val.dtype)


def _top2_route(logits, n_exp):
    lane = lax.broadcasted_iota(jnp.int32, logits.shape, 1)
    lg = jnp.where(lane < n_exp, logits, -jnp.inf)
    m1 = jnp.max(lg, axis=-1, keepdims=True)
    i1 = jnp.min(jnp.where(lg == m1, lane, LANES), axis=-1, keepdims=True)
    lg2 = jnp.where(lane == i1, -jnp.inf, lg)
    m2 = jnp.max(lg2, axis=-1, keepdims=True)
    i2 = jnp.min(jnp.where(lg2 == m2, lane, LANES), axis=-1, keepdims=True)
    e = jnp.exp(m2 - m1)
    w1 = 1.0 / (1.0 + e)
    w2 = e / (1.0 + e)
    idx = jnp.where(lane == 0, i1, jnp.where(lane == 1, i2, 0))
    wt = jnp.where(lane == 0, w1, jnp.where(lane == 1, w2, 0.0))
    return idx, wt


def _resid_ln_kernel(*refs, alpha, mode, n_exp, chunks, proj):
    if proj:
        x_ref, a_ref, w_ref, g_ref, lg_ref, lb_ref = refs[:6]
        rest = refs[6:]
    else:
        x_ref, y_ref, g_ref, lg_ref, lb_ref = refs[:5]
        rest = refs[5:]
    if mode == "last":
        (xo_ref,) = rest
        sc_sh = None
    elif mode == "next":
        sc_ref, sh_ref, xo_ref, uo_ref = rest
        sc_sh = (sc_ref[0], sh_ref[0])
    else:
        sc_ref, sh_ref, wr_ref, xo_ref, uo_ref, idx_ref, wt_ref = rest
        sc_sh = (sc_ref[0], sh_ref[0])
    tm = x_ref.shape[0]
    rc = PROJ_LN_ROWS if proj else tm
    if mode == "route":
        w_cat = jnp.concatenate(_split_bf16(wr_ref[...]), axis=1)

    def project(t):
        return jnp.dot(a_ref[t * rc:(t + 1) * rc, :], w_ref[...], preferred_element_type=F32)

    def finish(t, y):
        rows = slice(t * rc, (t + 1) * rc)
        z = alpha * x_ref[rows, :] + (1.0 + g_ref[0]) * y
        xn, u = _ln_modulate(z, lg_ref[0], lb_ref[0], sc_sh)
        xo_ref[rows, :] = xn
        if mode == "next":
            uo_ref[rows, :] = u.astype(uo_ref.dtype)
        elif mode == "route":
            pitch = _row_pitch(chunks)
            _store_linear(uo_ref.at[t * rc * pitch:(t + 1) * rc * pitch, :], u, chunks)
            uh, ul = _split_bf16(u)
            r = (jnp.dot(uh, w_cat, preferred_element_type=F32) + jnp.dot(ul, w_cat, preferred_element_type=F32))
            idx, wt = _top2_route(r[:, :LANES] + r[:, LANES:], n_exp)
            idx_ref[rows, :] = idx
            wt_ref[rows, :] = wt

    if proj:
        _staged(tm // rc, project, finish)
    else:
        finish(0, y_ref[...].astype(F32))


def _resid_ln(x, y, ada_rows, ln_g, ln_b, *, alpha, seq, layer, comp_g, nxt, w_router_pad=None, n_exp=0):
    t, d = x.shape
    proj = isinstance(y, tuple)
    tm = 512 if proj else 256
    tpb = seq // tm
    chunks = d // LANES
    mode = "last" if nxt is None else ("route" if w_router_pad is not None else "next")
    row = lambda l, comp: pl.BlockSpec((1, 1, d), lambda i: (_ada_row(l, i // tpb, comp), 0, 0))
    tile = pl.BlockSpec((tm, d), lambda i: (i, 0))
    lanes = pl.BlockSpec((tm, LANES), lambda i: (i, 0))
    lnp = pl.BlockSpec((1, 1, d), lambda i: (layer, 0, 0))
    if proj:
        a, w_stack, j = y
        k = a.shape[1]
        in_specs = [tile, pl.BlockSpec((tm, k), lambda i: (i, 0)), pl.BlockSpec((None, k, d), lambda i: (j, 0, 0))]
        args = [x, a, w_stack]
    else:
        in_specs = [tile, tile]
        args = [x, y]
    in_specs += [row(layer, comp_g), lnp, lnp]
    args += [ada_rows, ln_g, ln_b]
    out_specs = [tile]
    out_shape = [jax.ShapeDtypeStruct((t, d), F32)]
    if mode != "last":
        in_specs += [row(nxt[0], nxt[1]), row(nxt[0], nxt[2])]
        args += [ada_rows, ada_rows]
    if mode == "next":
        out_specs.append(tile)
        out_shape.append(jax.ShapeDtypeStruct((t, d), BF16))
    elif mode == "route":
        pitch = _row_pitch(chunks)
        in_specs.append(pl.BlockSpec((d, LANES), lambda i: (0, 0)))
        args.append(w_router_pad)
        out_specs += [pl.BlockSpec((tm * pitch, LANES), lambda i: (i, 0)), lanes, lanes]
        out_shape += [jax.ShapeDtypeStruct((t * pitch, LANES), F32),
                      jax.ShapeDtypeStruct((t, LANES), jnp.int32), jax.ShapeDtypeStruct((t, LANES), F32)]
    return pl.pallas_call(
        functools.partial(_resid_ln_kernel, alpha=alpha, mode=mode, n_exp=n_exp, chunks=chunks, proj=proj),
        grid=(t // tm,),
        in_specs=in_specs, out_specs=out_specs, out_shape=out_shape,
        compiler_params=_cparams(1, 56 if proj else 48),
        name=("proj_ln_" if proj else "resid_ln_") + mode,
    )(*args)


def _rms(x, g):
    ms = jnp.mean(x * x, axis=-1, keepdims=True)
    return x * lax.rsqrt(ms + RMS_EPS) * g


def _mla_down_kernel(*refs, ql, kvl, modulate):
    if modulate:
        u_ref, sc_ref, sh_ref, w_ref, qn_ref, kvn_ref, c_ref, s_ref, cq_ref, ckv_ref, kr_ref = refs
    else:
        u_ref, w_ref, qn_ref, kvn_ref, c_ref, s_ref, cq_ref, ckv_ref, kr_ref = refs
    rc = EPILOGUE_ROWS

    def matmul(t):
        u = u_ref[t * rc:(t + 1) * rc, :]
        if modulate:
            u = (u * (1.0 + sc_ref[0]) + sh_ref[0]).astype(BF16)
        return jnp.dot(u, w_ref[...], preferred_element_type=F32)

    def norm_rotate(t, acc):
        rows = slice(t * rc, (t + 1) * rc)
        cq_ref[rows, :] = _rms(acc[:, :ql], qn_ref[0]).astype(BF16)
        ckv_ref[rows, :] = _rms(acc[:, ql:ql + kvl], kvn_ref[0]).astype(BF16)
        xr = acc[:, ql + kvl:]
        kr_ref[rows, :] = (xr * c_ref[rows, :] + pltpu.roll(xr, LANES // 2, 1) * s_ref[rows, :]).astype(BF16)

    _staged(u_ref.shape[0] // rc, matmul, norm_rotate)


def _mla_down(u, w_perm, q_norm, kv_norm, layer, cos_t, sin_t, ql, kvl, mod=None):
    t, d = u.shape
    n = w_perm.shape[1]
    tm = 512
    nrm = lambda width: pl.BlockSpec((1, 1, width), lambda i: (layer, 0, 0))
    rows = lambda width: pl.BlockSpec((tm, width), lambda i: (i, 0))
    in_specs = [rows(d)]
    args = [u]
    if mod is not None:
        ada_rows, seq, ada_layer, comp_sc, comp_sh = mod
        tpb = seq // tm
        ada = lambda comp: pl.BlockSpec((1, 1, d), lambda i: (_ada_row(ada_layer, i // tpb, comp), 0, 0))
        in_specs += [ada(comp_sc), ada(comp_sh)]
        args += [ada_rows, ada_rows]
    in_specs += [pl.BlockSpec((d, n), lambda i: (0, 0)), nrm(ql), nrm(kvl), rows(LANES), rows(LANES)]
    args += [w_perm, q_norm, kv_norm, cos_t, sin_t]
    return pl.pallas_call(
        functools.partial(_mla_down_kernel, ql=ql, kvl=kvl, modulate=mod is not None),
        grid=(t // tm,),
        in_specs=in_specs,
        out_specs=[rows(ql), rows(kvl), rows(LANES)],
        out_shape=[jax.ShapeDtypeStruct((t, ql), BF16), jax.ShapeDtypeStruct((t, kvl), BF16),
                   jax.ShapeDtypeStruct((t, LANES), BF16)],
        compiler_params=_cparams(1, 48),
        name="mla_down",
    )(*args)


def _mla_up_kernel(cq_ref, ckv_ref, wq_ref, wkv_ref, c_ref, s_ref, q_ref, kv_ref, *, scale, heads):
    rc = EPILOGUE_ROWS

    def matmul(t):
        rows = slice(t * rc, (t + 1) * rc)
        return (jnp.dot(cq_ref[rows, :], wq_ref[...], preferred_element_type=F32),
                jnp.dot(ckv_ref[rows, :], wkv_ref[...], preferred_element_type=F32))

    def rotate(t, accs):
        acc, acc_kv = accs
        rows = slice(t * rc, (t + 1) * rc)
        kv_ref[rows, :] = acc_kv.astype(BF16)
        c = c_ref[rows, :]
        s = s_ref[rows, :]
        for h in range(heads):
            b0 = h * MLA_HEAD_PAD
            q_ref[rows, b0:b0 + LANES] = (acc[:, b0:b0 + LANES] * scale).astype(BF16)
            xr = acc[:, b0 + LANES:b0 + MLA_HEAD_PAD]
            q_ref[rows, b0 + LANES:b0 + MLA_HEAD_PAD] = (
                (xr * c + pltpu.roll(xr, LANES // 2, 1) * s) * scale).astype(BF16)

    _staged(cq_ref.shape[0] // rc, matmul, rotate)


def _mla_up(cq, ckv, wq_perm, wkv_stack, layer, cos_t, sin_t, scale):
    t, k = cq.shape
    n = wq_perm.shape[1]
    assert wkv_stack.shape[1:] == (ckv.shape[1], n)
    tm, tn = 1024, 1024
    rows = pl.BlockSpec((tm, LANES), lambda j, i: (i, 0))
    lat = pl.BlockSpec((tm, k), lambda j, i: (i, 0))
    out = pl.BlockSpec((tm, tn), lambda j, i: (i, j))
    return pl.pallas_call(
        functools.partial(_mla_up_kernel, scale=scale, heads=tn // MLA_HEAD_PAD),
        grid=(n // tn, t // tm),
        in_specs=[lat, lat, pl.BlockSpec((k, tn), lambda j, i: (0, j)),
                  pl.BlockSpec((None, k, tn), lambda j, i: (layer, 0, j)), rows, rows],
        out_specs=[out, out],
        out_shape=[jax.ShapeDtypeStruct((t, n), BF16), jax.ShapeDtypeStruct((t, n), BF16)],
        compiler_params=_cparams(2, 48),
        name="mla_up",
    )(cq, ckv, wq_perm, wkv_stack, cos_t, sin_t)


def _moba_qkv_kernel(u_ref, w_ref, tab_ref, o_ref, wb_ref, *, heads):
    @pl.when(pl.program_id(1) == 0)
    def _():
        wb_ref[...] = w_ref[...].astype(BF16)

    half = MOBA_ROT_DIM // 2
    rc = EPILOGUE_ROWS

    def matmul(t):
        return jnp.dot(u_ref[t * rc:(t + 1) * rc, :], wb_ref[...], preferred_element_type=F32)

    def rotate(t, acc):
        rows = slice(t * rc, (t + 1) * rc)
        c = tab_ref[rows, :LANES]
        s1 = tab_ref[rows, LANES:2 * LANES]
        s2 = tab_ref[rows, 2 * LANES:]
        for h in range(heads):
            x = acc[:, h * LANES:(h + 1) * LANES]
            r = x * c + pltpu.roll(x, half, 1) * s1 + pltpu.roll(x, LANES - half, 1) * s2
            o_ref[rows, h * LANES:(h + 1) * LANES] = r.astype(BF16)

    _staged(u_ref.shape[0] // rc, matmul, rotate)


def _moba_qkv(u, w_stack, layer, tables):
    t, k = u.shape
    n = w_stack.shape[2]
    tm, tn = 2048, 512
    tiles_per_sec = (n // 3) // tn
    return pl.pallas_call(
        functools.partial(_moba_qkv_kernel, heads=tn // LANES),
        grid=(n // tn, t // tm),
        in_specs=[pl.BlockSpec((tm, k), lambda j, i: (i, 0)),
                  pl.BlockSpec((None, k, tn), lambda j, i: (layer, 0, j)),
                  pl.BlockSpec((None, tm, 3 * LANES), lambda j, i: (j // tiles_per_sec, i, 0))],
        out_specs=pl.BlockSpec((tm, tn), lambda j, i: (i, j)),
        out_shape=jax.ShapeDtypeStruct((t, n), BF16),
        scratch_shapes=[pltpu.VMEM((k, tn), BF16)],
        compiler_params=_cparams(2, 48),
        name="moba_qkv",
    )(u, w_stack, tables)


_NT = (((1,), (1,)), ((), ()))


def _softmax_numer(s):
    m = jnp.max(s, axis=-1, keepdims=True)
    return jnp.exp(s - m).astype(BF16)


def _fill_values_ones(vext_ref, v):
    vext_ref[:, :LANES] = v
    vext_ref[:, LANES:] = jnp.ones_like(v)


def _normalised_pv(p, vext):
    o = jnp.dot(p, vext, preferred_element_type=F32)
    return o[:, :LANES] / o[:, LANES:LANES + 1]


def _staged(n_tiles, *stages, reverse=False):
    vals = {}
    for t in range(n_tiles + len(stages) - 1):
        for k, stage in enumerate(stages):
            if 0 <= t - k < n_tiles:
                tile = n_tiles - 1 - (t - k) if reverse else t - k
                vals[tile] = stage(tile) if k == 0 else stage(tile, vals[tile])


def _mla_attn_kernel(q_ref, kv_ref, kr_ref, o_ref, kfull_ref, vext_ref, *, tq):
    seq = q_ref.shape[0]
    kfull_ref[:, :LANES] = kv_ref[:, :LANES]
    kfull_ref[:, LANES:] = kr_ref[...]
    _fill_values_ones(vext_ref, kv_ref[:, LANES:])
    row = lax.broadcasted_iota(jnp.int32, (tq, tq), 0)
    col = lax.broadcasted_iota(jnp.int32, (tq, tq), 1)
    causal = col <= row

    def scores(n):
        q = q_ref[n * tq:(n + 1) * tq, :]
        return lax.dot_general(q, kfull_ref[0:(n + 1) * tq, :], _NT, preferred_element_type=F32)

    def probs(n, s):
        diag = jnp.where(causal, s[:, n * tq:], -jnp.inf)
        return _softmax_numer(jnp.concatenate([s[:, :n * tq], diag], axis=1) if n else diag)

    def output(n, p):
        o_ref[n * tq:(n + 1) * tq, :] = _normalised_pv(p, vext_ref[0:(n + 1) * tq, :]).astype(o_ref.dtype)

    _staged(seq // tq, scores, probs, output, reverse=True)


def _mla_attention(q, kv, kr, batch, seq, heads):
    t = q.shape[0]
    return pl.pallas_call(
        functools.partial(_mla_attn_kernel, tq=256),
        grid=(batch, heads),
        in_specs=[pl.BlockSpec((seq, MLA_HEAD_PAD), lambda b, h: (b, h)),
                  pl.BlockSpec((seq, MLA_NOPE + MLA_V), lambda b, h: (b, h)),
                  pl.BlockSpec((seq, LANES), lambda b, h: (b, 0))],
        out_specs=pl.BlockSpec((seq, MLA_V), lambda b, h: (b, h)),
        out_shape=jax.ShapeDtypeStruct((t, heads * MLA_V), BF16),
        scratch_shapes=[pltpu.VMEM((seq, MLA_HEAD_PAD), BF16), pltpu.VMEM((seq, 2 * LANES), BF16)],
        compiler_params=_cparams(2, 48),
        name="mla_attn",
    )(q, kv, kr)


def _moba_attn_kernel(q_ref, k_ref, v_ref, o_ref, vext_ref, *, nb):
    blk = MOBA_BLOCK
    seq = k_ref.shape[0]
    _fill_values_ones(vext_ref, v_ref[...])
    r = lax.broadcasted_iota(jnp.int32, (LANES, seq), 0)
    c = lax.broadcasted_iota(jnp.int32, (LANES, seq), 1)
    ind = jnp.where(c // blk == r, 1.0 / blk, 0.0).astype(BF16)
    km = jnp.dot(ind, k_ref[...], preferred_element_type=F32)
    kmh, kml = _split_bf16(km)
    row = lax.broadcasted_iota(jnp.int32, (blk, blk), 0)
    col = lax.broadcasted_iota(jnp.int32, (blk, blk), 1)
    causal = col <= row
    lane = lax.broadcasted_iota(jnp.int32, (blk, LANES), 1)

    def scores(n):
        q = q_ref[n * blk:(n + 1) * blk, :]
        s = lax.dot_general(q, k_ref[0:(n + 1) * blk, :], _NT, preferred_element_type=F32)
        if n <= MOBA_TOPK:
            return s, None
        gate_t = (lax.dot_general(kmh, q, _NT, preferred_element_type=F32)
                  + lax.dot_general(kml, q, _NT, preferred_element_type=F32))[:SUBLANES, :]
        return s, gate_t

    def probs(n, s_gate):
        s, gate_t = s_gate
        if gate_t is not None:
            blk_id = lax.broadcasted_iota(jnp.int32, gate_t.shape, 0)
            keep_t = jnp.zeros(gate_t.shape, F32)
            for j in range(n):
                gj = gate_t[j:j + 1, :]
                beats = (blk_id < n) & ((gate_t > gj) | ((gate_t == gj) & (blk_id < j)))
                n_beats = jnp.sum(beats.astype(F32), axis=0, keepdims=True)
                keep_t = jnp.where((blk_id == j) & (n_beats < MOBA_TOPK), 1.0, keep_t)
            keep = jnp.concatenate([keep_t, jnp.zeros((LANES - SUBLANES, blk), F32)], axis=0).T
        parts = []
        for j in range(n):
            sj = s[:, j * blk:(j + 1) * blk]
            if gate_t is not None:
                sj = jnp.where(keep[:, j:j + 1] > 0.5, sj, -jnp.inf)
            parts.append(sj)
        parts.append(jnp.where(causal, s[:, n * blk:], -jnp.inf))
        return _softmax_numer(jnp.concatenate(parts, axis=1) if n else parts[0])

    def output(n, p):
        o_ref[n * blk:(n + 1) * blk, :] = _normalised_pv(p, vext_ref[0:(n + 1) * blk, :]).astype(o_ref.dtype)

    _staged(nb, scores, probs, output, reverse=True)


def _moba_attention(qkv, batch, seq, heads):
    t = qkv.shape[0]
    d = MOBA_HEAD_DIM
    return pl.pallas_call(
        functools.partial(_moba_attn_kernel, nb=seq // MOBA_BLOCK),
        grid=(batch, heads),
        in_specs=[pl.BlockSpec((seq, d), lambda b, h: (b, h)),
                  pl.BlockSpec((seq, d), lambda b, h: (b, heads + h)),
                  pl.BlockSpec((seq, d), lambda b, h: (b, 2 * heads + h))],
        out_specs=pl.BlockSpec((seq, d), lambda b, h: (b, h)),
        out_shape=jax.ShapeDtypeStruct((t, heads * d), BF16),
        scratch_shapes=[pltpu.VMEM((seq, 2 * LANES), BF16)],
        compiler_params=_cparams(2, 48),
        name="moba_attn",
    )(qkv, qkv, qkv)


def _swiglu_partial(x, wg, wu, wd):
    g = jnp.dot(x, wg, preferred_element_type=F32)
    u = jnp.dot(x, wu, preferred_element_type=F32)
    a = (g * jax.nn.sigmoid(g) * u).astype(BF16)
    return jnp.dot(a, wd, preferred_element_type=F32)


def _ffn_kernel(u_ref, wg_ref, wu_ref, wd_ref, o_ref):
    f = pl.program_id(1)
    @pl.when(f == 0)
    def _():
        o_ref[...] = jnp.zeros_like(o_ref)

    o_ref[...] += _swiglu_partial(u_ref[...], wg_ref[...].astype(BF16), wu_ref[...].astype(BF16),
                                  wd_ref[...].astype(BF16))


def _dense_ffn(u, w_gate_up, w_down, layer):
    t, d = u.shape
    dff = w_down.shape[1]
    tm, tf = 1024, 256
    nf = dff // tf
    return pl.pallas_call(
        _ffn_kernel,
        grid=(t // tm, nf),
        in_specs=[pl.BlockSpec((tm, d), lambda i, f: (i, 0)),
                  pl.BlockSpec((None, d, tf), lambda i, f: (layer, 0, f)),
                  pl.BlockSpec((None, d, tf), lambda i, f: (layer, 0, nf + f)),
                  pl.BlockSpec((None, tf, d), lambda i, f: (layer, f, 0))],
        out_specs=pl.BlockSpec((tm, d), lambda i, f: (i, 0)),
        out_shape=jax.ShapeDtypeStruct((t, d), F32),
        compiler_params=_cparams(2, 56),
        name="dense_ffn",
    )(u, w_gate_up, w_gate_up, w_down)


def _route_metadata(idx2, n_exp):
    rb, st = MOE_ROW_BLOCK, MOE_SUPER_BLOCKS
    t = idx2.shape[0]
    a = t * TOP_K
    n_items = (a // rb + n_exp - 1 + n_exp * (st - 1)) // st
    e_flat = idx2.reshape(a)
    onehot = (e_flat[:, None] == jnp.arange(n_exp, dtype=jnp.int32)[None, :]).astype(jnp.int32)
    csum = jnp.cumsum(onehot, axis=0)
    rank = jnp.sum((csum - onehot) * onehot, axis=1)
    counts = csum[-1]
    nsub = (counts + rb - 1) // rb
    sub_start = jnp.cumsum(nsub) - nsub
    dest = jnp.sum(onehot * (sub_start * rb)[None, :], axis=1) + rank
    p_rows = (a // rb + n_exp) * rb
    token_flat = jnp.arange(a, dtype=jnp.int32) // TOP_K
    row_token = jnp.zeros((p_rows,), jnp.int32).at[dest].set(token_flat)
    n_it = (nsub + st - 1) // st
    it_end = jnp.cumsum(n_it)
    it_start = it_end - n_it
    w = jnp.arange(n_items, dtype=jnp.int32)
    e_w = jnp.sum((it_end[None, :] <= w[:, None]).astype(jnp.int32), axis=1)
    active = e_w < n_exp
    e_c = jnp.minimum(e_w, n_exp - 1)
    local = w - it_start[e_c]
    item_nsub = jnp.where(active, jnp.clip(nsub[e_c] - local * st, 0, st), 0)
    item_row0 = jnp.where(active, (sub_start[e_c] + local * st) * rb, 0)
    e_last = jnp.max(jnp.where(n_it > 0, jnp.arange(n_exp, dtype=jnp.int32), 0))
    item_e = jnp.where(active, e_c, e_last)
    item_row0 = jnp.concatenate([item_row0, jnp.sum(nsub, keepdims=True) * rb])
    return (dest.astype(jnp.int32), row_token, item_e.astype(jnp.int32), item_row0.astype(jnp.int32),
            item_nsub.astype(jnp.int32), p_rows)


ROW_DMA_UNROLL = 8


def _row_copy(src_hbm, src_row, dst, dst_row, sem, chunks):
    pitch = _row_pitch(chunks)
    return pltpu.make_async_copy(src_hbm.at[pl.ds(src_row * pitch, chunks), :],
                                 dst.at[pl.ds(dst_row * pitch, chunks), :], sem)


def _rows_wait(src_hbm, dst, sem, rows, chunks):
    n = rows * chunks
    pltpu.make_async_copy(src_hbm.at[pl.ds(0, n), :], dst.at[pl.ds(0, n), :], sem).wait()


def _dispatch_kernel(tok_ref, u_hbm, o_ref, stage_ref, sem, *, rb, chunks):
    i = pl.program_id(0)
    pitch = _row_pitch(chunks)

    def issue(step, slot):
        def body(g, c):
            for k in range(ROW_DMA_UNROLL):
                r = g * ROW_DMA_UNROLL + k
                _row_copy(u_hbm, tok_ref[step * rb + r], stage_ref.at[slot], r, sem.at[slot],
                          chunks).start(priority=k % 2)
            return c
        lax.fori_loop(0, rb // ROW_DMA_UNROLL, body, 0)

    @pl.when(i == 0)
    def _():
        issue(0, 0)

    @pl.when(i + 1 < pl.num_programs(0))
    def _():
        issue(i + 1, (i + 1) % 2)

    slot = i % 2
    _rows_wait(u_hbm, stage_ref.at[slot], sem.at[slot], rb, chunks)
    for c in range(chunks):
        o_ref[:, c * LANES:(c + 1) * LANES] = stage_ref[slot, pl.ds(c, rb, stride=pitch), :].astype(o_ref.dtype)


def _dispatch(u_lin, row_token, d):
    p_rows = row_token.shape[0]
    rb = MOE_ROW_BLOCK
    chunks = d // LANES
    return pl.pallas_call(
        functools.partial(_dispatch_kernel, rb=rb, chunks=chunks),
        grid_spec=pltpu.PrefetchScalarGridSpec(
            num_scalar_prefetch=1,
            grid=(p_rows // rb,),
            in_specs=[pl.BlockSpec(memory_space=pl.ANY)],
            out_specs=pl.BlockSpec((rb, d), lambda i, tok: (i, 0)),
            scratch_shapes=[pltpu.VMEM((2, rb * _row_pitch(chunks), LANES), F32),
                            pltpu.SemaphoreType.DMA((2,))]),
        out_shape=jax.ShapeDtypeStruct((p_rows, d), BF16),
        compiler_params=_cparams(1, 40),
        name="moe_dispatch",
    )(row_token, u_lin)


def _moe_ffn_kernel(e_ref, row0_ref, nsub_ref, xs_hbm, wg_ref, wu_ref, wd_ref, ys_hbm,
                    x_ref, acc_ref, wgb_ref, wub_ref, wdb_ref, stage_ref, sem, *, rb, nf, n_items, chunks):
    w = pl.program_id(0)
    f = pl.program_id(1)
    nsub = nsub_ref[w]
    row0 = row0_ref[w]

    def for_range(n, fn):
        def body(r, c):
            fn(r)
            return c
        lax.fori_loop(0, n, body, 0)

    def load(r):
        return pltpu.make_async_copy(xs_hbm.at[pl.ds(pl.multiple_of(row0 + r * rb, rb), rb), :],
                                     x_ref.at[pl.ds(pl.multiple_of(r * rb, rb), rb), :], sem.at[0])

    pitch = _row_pitch(chunks)

    def store(row, slot):
        return pltpu.make_async_copy(
            stage_ref.at[slot], ys_hbm.at[pl.ds(pl.multiple_of(row * pitch, rb * pitch), rb * pitch), :],
            sem.at[1 + slot])

    @pl.when(f == 0)
    def _():
        for_range(nsub, lambda r: load(r).start())
        acc_ref[...] = jnp.zeros_like(acc_ref)
        for_range(nsub, lambda r: load(r).wait())

    @pl.when(nsub > 0)
    def _():
        wgb_ref[...] = wg_ref[...].astype(BF16)
        wub_ref[...] = wu_ref[...].astype(BF16)
        wdb_ref[...] = wd_ref[...].astype(BF16)

        def chunk(start, size):
            rows = pl.ds(pl.multiple_of(start, rb), size)
            acc_ref[rows, :] += _swiglu_partial(x_ref[rows, :], wgb_ref[...], wub_ref[...], wdb_ref[...])

        for_range(nsub // 2, lambda p: chunk(p * (2 * rb), 2 * rb))

        @pl.when(nsub % 2 == 1)
        def _():
            chunk((nsub - 1) * rb, rb)

    @pl.when(f == nf - 1)
    def _():
        def emit(r):
            slot = r % 2

            @pl.when(r >= 2)
            def _():
                store(row0 + (r - 2) * rb, slot).wait()

            _store_linear(stage_ref.at[slot], acc_ref[pl.ds(pl.multiple_of(r * rb, rb), rb), :], chunks)
            store(row0 + r * rb, slot).start()

        for_range(nsub, emit)

        @pl.when(nsub >= 2)
        def _():
            store(row0, nsub % 2).wait()

        @pl.when(nsub >= 1)
        def _():
            store(row0, (nsub - 1) % 2).wait()

    @pl.when((f == nf - 1) & (w == n_items - 1))
    def _():
        used = row0_ref[n_items]
        n_tail = (ys_hbm.shape[0] // pitch - used) // rb
        stage_ref[0] = jnp.zeros(stage_ref.shape[1:], F32)
        for_range(n_tail, lambda r: store(used + r * rb, 0).start())
        for_range(n_tail, lambda r: store(used + r * rb, 0).wait())


def _moe_ffn(xs, w_gate_up, w_down, layer, item_e, item_row0, item_nsub):
    p_rows, d = xs.shape
    dff = w_down.shape[2]
    rb, st = MOE_ROW_BLOCK, MOE_SUPER_BLOCKS
    n_items = item_e.shape[0]
    chunks = d // LANES
    tf = 256
    nf = dff // tf

    def f_eff(f, nsub, w):
        return jnp.where(nsub[w] > 0, f, nf - 1)

    return pl.pallas_call(
        functools.partial(_moe_ffn_kernel, rb=rb, nf=nf, n_items=n_items, chunks=chunks),
        grid_spec=pltpu.PrefetchScalarGridSpec(
            num_scalar_prefetch=3,
            grid=(n_items, nf),
            in_specs=[pl.BlockSpec(memory_space=pl.ANY),
                      pl.BlockSpec((None, None, d, tf), lambda w, f, e, r0, ns: (layer, e[w], 0, f_eff(f, ns, w))),
                      pl.BlockSpec((None, None, d, tf),
                                   lambda w, f, e, r0, ns: (layer, e[w], 0, nf + f_eff(f, ns, w))),
                      pl.BlockSpec((None, None, tf, d), lambda w, f, e, r0, ns: (layer, e[w], f_eff(f, ns, w), 0))],
            out_specs=pl.BlockSpec(memory_space=pl.ANY),
            scratch_shapes=[pltpu.VMEM((st * rb, d), BF16), pltpu.VMEM((st * rb, d), F32),
                            pltpu.VMEM((d, tf), BF16), pltpu.VMEM((d, tf), BF16), pltpu.VMEM((tf, d), BF16),
                            pltpu.VMEM((2, rb * _row_pitch(chunks), LANES), F32), pltpu.SemaphoreType.DMA((3,))]),
        out_shape=jax.ShapeDtypeStruct((p_rows * _row_pitch(chunks), LANES), F32),
        compiler_params=_cparams(2, 58),
        name="moe_ffn",
    )(item_e, item_row0, item_nsub, xs, w_gate_up, w_gate_up, w_down)


def _moe_combine_ln_kernel(*refs, alpha, emit_u, tm, chunks):
    if emit_u:
        (pos_ref, ys_hbm, wt_ref, x_ref, g_ref, lg_ref, lb_ref, sc_ref, sh_ref, xo_ref, uo_ref,
         stage_ref, y_ref, sem) = refs
        sc_sh = (sc_ref[0], sh_ref[0])
    else:
        pos_ref, ys_hbm, wt_ref, x_ref, g_ref, lg_ref, lb_ref, xo_ref, stage_ref, y_ref, sem = refs
        sc_sh = None
    i = pl.program_id(0)
    pitch = _row_pitch(chunks)

    def issue(step, slot):
        def body(g, c):
            for j in range(ROW_DMA_UNROLL // TOP_K):
                t = g * (ROW_DMA_UNROLL // TOP_K) + j
                for k in range(TOP_K):
                    _row_copy(ys_hbm, pos_ref[(step * tm + t) * TOP_K + k], stage_ref.at[slot, k], t,
                              sem.at[slot, k], chunks).start(priority=k % 2)
            return c
        lax.fori_loop(0, tm * TOP_K // ROW_DMA_UNROLL, body, 0)

    @pl.when(i == 0)
    def _():
        issue(0, 0)

    @pl.when(i + 1 < pl.num_programs(0))
    def _():
        issue(i + 1, (i + 1) % 2)

    slot = i % 2
    for k in range(TOP_K):
        _rows_wait(ys_hbm, stage_ref.at[slot, k], sem.at[slot, k], tm, chunks)
    w1 = wt_ref[:, 0:1]
    w2 = wt_ref[:, 1:2]
    for c in range(chunks):
        y_ref[:, c * LANES:(c + 1) * LANES] = (stage_ref[slot, 0, pl.ds(c, tm, stride=pitch), :] * w1
                                               + stage_ref[slot, 1, pl.ds(c, tm, stride=pitch), :] * w2)
    z = alpha * x_ref[...] + (1.0 + g_ref[0]) * y_ref[...]
    xn, u = _ln_modulate(z, lg_ref[0], lb_ref[0], sc_sh)
    xo_ref[...] = xn
    if emit_u:
        uo_ref[...] = u.astype(uo_ref.dtype)


def _moe_combine_ln(ys_lin, pos, wt, x, ada_rows, ln_g, ln_b, *, alpha, seq, layer, comp_g, nxt):
    t, d = x.shape
    tm = 256
    tpb = seq // tm
    chunks = d // LANES
    row = lambda l, comp: pl.BlockSpec((1, 1, d), lambda i, p: (_ada_row(l, i // tpb, comp), 0, 0))
    tile = pl.BlockSpec((tm, d), lambda i, p: (i, 0))
    lnp = pl.BlockSpec((1, 1, d), lambda i, p: (layer, 0, 0))
    in_specs = [pl.BlockSpec(memory_space=pl.ANY), pl.BlockSpec((tm, LANES), lambda i, p: (i, 0)), tile,
                row(layer, comp_g), lnp, lnp]
    args = [ys_lin, wt, x, ada_rows, ln_g, ln_b]
    out_specs = [tile]
    out_shape = [jax.ShapeDtypeStruct((t, d), F32)]
    if nxt is not None:
        in_specs += [row(nxt[0], nxt[1]), row(nxt[0], nxt[2])]
        args += [ada_rows, ada_rows]
        out_specs.append(tile)
        out_shape.append(jax.ShapeDtypeStruct((t, d), BF16))
    outs = pl.pallas_call(
        functools.partial(_moe_combine_ln_kernel, alpha=alpha, emit_u=nxt is not None, tm=tm, chunks=chunks),
        grid_spec=pltpu.PrefetchScalarGridSpec(
            num_scalar_prefetch=1,
            grid=(t // tm,),
            in_specs=in_specs, out_specs=out_specs,
            scratch_shapes=[pltpu.VMEM((2, TOP_K, tm * _row_pitch(chunks), LANES), F32), pltpu.VMEM((tm, d), F32),
                            pltpu.SemaphoreType.DMA((2, TOP_K))]),
        out_shape=out_shape,
        compiler_params=_cparams(1, 48),
        name="moe_combine_ln",
    )(pos, *args)
    return (outs[0], outs[1]) if nxt is not None else (outs[0], None)


def _rope_cos_sin(positions, dim):
    inv_freq = ROPE_THETA ** (-jnp.arange(0, dim, 2, dtype=F32) / dim)
    ang = positions.astype(F32).reshape(-1)[:, None] * inv_freq
    return jnp.cos(ang), jnp.sin(ang)


def _mla_rope_tables(positions):
    cos, sin = _rope_cos_sin(positions, MLA_ROPE)
    z = jnp.zeros_like(cos)
    return jnp.concatenate([cos, z, cos, z], axis=1), jnp.concatenate([-sin, z, sin, z], axis=1)


def _moba_rope_tables(positions, scale):
    cos, sin = _rope_cos_sin(positions, MOBA_ROT_DIM)
    t, half = cos.shape
    rest = LANES - 2 * half
    c = jnp.concatenate([cos, cos, jnp.ones((t, rest), F32)], axis=1)
    s1 = jnp.concatenate([jnp.zeros((t, half), F32), sin, jnp.zeros((t, rest), F32)], axis=1)
    s2 = jnp.concatenate([-sin, jnp.zeros((t, half + rest), F32)], axis=1)
    rot = jnp.concatenate([c, s1, s2], axis=1)
    ident = jnp.concatenate([jnp.ones((t, LANES), F32), jnp.zeros((t, 2 * LANES), F32)], axis=1)
    return jnp.stack([rot * scale, rot, ident])


def _spread_rope_cols(w_rope):
    half = MLA_ROPE // 2
    z = jnp.zeros(w_rope.shape[:-1] + (LANES // 2 - half,), w_rope.dtype)
    return jnp.concatenate([w_rope[..., :half], z, w_rope[..., half:], z], axis=-1)


def _mla_weights(w_down, w_uq, ql, kvl):
    k = w_uq.shape[0]
    heads = w_uq.shape[1] // (MLA_NOPE + MLA_ROPE)
    wd = jnp.concatenate([w_down[:, :ql + kvl], _spread_rope_cols(w_down[:, ql + kvl:])], axis=1).astype(BF16)
    wq = w_uq.reshape(k, heads, MLA_NOPE + MLA_ROPE)
    wq = jnp.concatenate([wq[..., :MLA_NOPE], _spread_rope_cols(wq[..., MLA_NOPE:])], axis=-1)
    return wd, wq.reshape(k, heads * MLA_HEAD_PAD).astype(BF16), heads


def kernel(x, c, positions, w_ada, b_ada, ln_mix_g, ln_mix_b, ln_ffn_g, ln_ffn_b, mla_w_down, mla_q_norm,
           mla_kv_norm, mla_w_uq, mla_w_ukv, mla_w_o, moba_w_qkv, moba_w_o, ffn_w_gate_up, ffn_w_down,
           moe_w_router, moe_w_gate_up, moe_w_down):
    batch, seq, d = x.shape
    depth = w_ada.shape[0]
    t = batch * seq
    alpha = (2.0 * depth) ** 0.25
    ql = mla_q_norm.shape[1]
    kvl = mla_kv_norm.shape[1]
    n_exp = moe_w_router.shape[2]

    ada_rows = _ada_all(c, w_ada, b_ada)
    cos_mla, sin_mla = _mla_rope_tables(positions)
    moba_tables = _moba_rope_tables(positions, MOBA_HEAD_DIM ** -0.5)
    mla_w_ukv_b = mla_w_ukv.astype(BF16)
    mla_w_o_b = mla_w_o.astype(BF16)
    moba_w_o_b = moba_w_o.astype(BF16)
    ln3 = lambda p: p.reshape(depth, 1, d)
    ln_mix_g, ln_mix_b, ln_ffn_g, ln_ffn_b = ln3(ln_mix_g), ln3(ln_mix_b), ln3(ln_ffn_g), ln3(ln_ffn_b)
    q_norm3 = mla_q_norm.reshape(-1, 1, ql)
    kv_norm3 = mla_kv_norm.reshape(-1, 1, kvl)

    xf = x.reshape(t, d)
    u = None
    for l in range(depth):
        j = l // 2
        moe_layer = l % 2 == 1
        if l % 2 == 0:
            wd_p, wq_p, heads = _mla_weights(mla_w_down[j], mla_w_uq[j], ql, kvl)
            first = (xf, (ada_rows, seq, l, 1, 0)) if u is None else (u, None)
            cq, ckv, kr = _mla_down(first[0], wd_p, q_norm3, kv_norm3, j, cos_mla, sin_mla, ql, kvl, mod=first[1])
            q, kv = _mla_up(cq, ckv, wq_p, mla_w_ukv_b, j, cos_mla, sin_mla, (MLA_NOPE + MLA_ROPE) ** -0.5)
            o = _mla_attention(q, kv, kr, batch, seq, heads)
            y = (o, mla_w_o_b, j)
        else:
            heads = moba_w_qkv.shape[2] // (3 * MOBA_HEAD_DIM)
            qkv = _moba_qkv(u, moba_w_qkv, j, moba_tables)
            o = _moba_attention(qkv, batch, seq, heads)
            y = (o, moba_w_o_b, j)
        ln_args = dict(alpha=alpha, seq=seq, layer=l)
        nxt = (l + 1, 1, 0) if l + 1 < depth else None
        if not moe_layer:
            xf, u = _resid_ln(xf, y, ada_rows, ln_mix_g, ln_mix_b, comp_g=2, nxt=(l, 4, 3), **ln_args)
            y = _dense_ffn(u, ffn_w_gate_up, ffn_w_down, j)
            outs = _resid_ln(xf, y, ada_rows, ln_ffn_g, ln_ffn_b, comp_g=5, nxt=nxt, **ln_args)
            xf, u = outs[0], (outs[1] if nxt is not None else None)
        else:
            w_router_pad = jnp.zeros((d, LANES), F32).at[:, :n_exp].set(moe_w_router[j])
            xf, u_lin, idx, wt = _resid_ln(xf, y, ada_rows, ln_mix_g, ln_mix_b, comp_g=2, nxt=(l, 4, 3),
                                           w_router_pad=w_router_pad, n_exp=n_exp, **ln_args)
            pos, row_token, item_e, item_row0, item_nsub, _ = _route_metadata(idx[:, :TOP_K], n_exp)
            xs = _dispatch(u_lin, row_token, d)
            ys_lin = _moe_ffn(xs, moe_w_gate_up, moe_w_down, j, item_e, item_row0, item_nsub)
            xf, u = _moe_combine_ln(ys_lin, pos, wt, xf, ada_rows, ln_ffn_g, ln_ffn_b, comp_g=5, nxt=nxt, **ln_args)
    return xf.reshape(batch, seq, d)
```

```python
import functools

import jax
import jax.numpy as jnp
from jax import lax
from jax.experimental import pallas as pl
from jax.experimental.pallas import tpu as pltpu

F32 = jnp.float32
BF16 = jnp.bfloat16

ROPE_THETA = 500000.0
LN_EPS = 1e-5
RMS_EPS = 1e-6
MLA_NOPE = 128
MLA_ROPE = 64
MLA_V = 128
MOBA_HEAD_DIM = 128
MOBA_ROT_DIM = 32
MOBA_BLOCK = 256
MOBA_TOPK = 3
TOP_K = 2

LANES = 128
SUBLANES = 8
MLA_HEAD_PAD = 2 * LANES

ADA_BATCH_PAD = SUBLANES
MOE_ROW_BLOCK = 256
MOE_SUPER_BLOCKS = 10
EPILOGUE_ROWS = 256
PROJ_LN_ROWS = 128


def _cparams(n_axes, vmem_mb):
    return pltpu.CompilerParams(dimension_semantics=("arbitrary",) * n_axes,
                                vmem_limit_bytes=vmem_mb * 1024 * 1024)


def _split_bf16(x):
    hi = x.astype(BF16)
    return hi, (x - hi.astype(F32)).astype(BF16)


def _ada_row(layer, batch, comp):
    return (layer * ADA_BATCH_PAD + batch) * 6 + comp


def _ada_kernel(c_ref, w_ref, b_ref, o_ref):
    c = c_ref[...]
    ca = (c * jax.nn.sigmoid(c)).astype(BF16)
    o_ref[...] = jnp.dot(ca, w_ref[...].astype(BF16), preferred_element_type=F32) + b_ref[...]


def _ada_all(c, w_ada, b_ada):
    depth, d, n6 = w_ada.shape
    b = c.shape[0]
    c_pad = jnp.zeros((ADA_BATCH_PAD, d), F32).at[:b].set(c)
    tn = 1024
    out = pl.pallas_call(
        _ada_kernel,
        grid=(depth, n6 // tn),
        in_specs=[pl.BlockSpec((ADA_BATCH_PAD, d), lambda l, j: (0, 0)),
                  pl.BlockSpec((None, d, tn), lambda l, j: (l, 0, j)),
                  pl.BlockSpec((None, 1, tn), lambda l, j: (l, 0, j))],
        out_specs=pl.BlockSpec((None, ADA_BATCH_PAD, tn), lambda l, j: (l, 0, j)),
        out_shape=jax.ShapeDtypeStruct((depth, ADA_BATCH_PAD, n6), F32),
        compiler_params=_cparams(2, 40),
        name="ada",
    )(c_pad, w_ada, b_ada.reshape(depth, 1, n6))
    return out.reshape(depth * ADA_BATCH_PAD * 6, 1, d)


def _ln_modulate(z, lg, lb, sc_sh):
    mu = jnp.mean(z, axis=-1, keepdims=True)
    zc = z - mu
    var = jnp.mean(zc * zc, axis=-1, keepdims=True)
    xn = zc * lax.rsqrt(var + LN_EPS) * lg + lb
    if sc_sh is None:
        return xn, None
    sc, sh = sc_sh
    return xn, xn * (1.0 + sc) + sh


def _row_pitch(chunks):
    return chunks + 1


def _store_linear(dst_ref, val, chunks):
    rows = val.shape[0]
    pitch = _row_pitch(chunks)
    for c in range(chunks):
        dst_ref[pl.ds(c, rows, stride=pitch), :] = val[:, c * LANES:(c + 1) * LANES]
    dst_ref[pl.ds(chunks, rows, stride=pitch), :] = jnp.zeros((rows, LANES), val.dtype)


def _top2_route(logits, n_exp):
    lane = lax.broadcasted_iota(jnp.int32, logits.shape, 1)
    lg = jnp.where(lane < n_exp, logits, -jnp.inf)
    m1 = jnp.max(lg, axis=-1, keepdims=True)
    i1 = jnp.min(jnp.where(lg == m1, lane, LANES), axis=-1, keepdims=True)
    lg2 = jnp.where(lane == i1, -jnp.inf, lg)
    m2 = jnp.max(lg2, axis=-1, keepdims=True)
    i2 = jnp.min(jnp.where(lg2 == m2, lane, LANES), axis=-1, keepdims=True)
    e = jnp.exp(m2 - m1)
    w1 = 1.0 / (1.0 + e)
    w2 = e / (1.0 + e)
    idx = jnp.where(lane == 0, i1, jnp.where(lane == 1, i2, 0))
    wt = jnp.where(lane == 0, w1, jnp.where(lane == 1, w2, 0.0))
    return idx, wt


def _resid_ln_kernel(*refs, alpha, mode, n_exp, chunks, proj):
    if proj:
        x_ref, a_ref, w_ref, g_ref, lg_ref, lb_ref = refs[:6]
        rest = refs[6:]
    else:
        x_ref, y_ref, g_ref, lg_ref, lb_ref = refs[:5]
        rest = refs[5:]
    if mode == "last":
        (xo_ref,) = rest
        sc_sh = None
    elif mode == "next":
        sc_ref, sh_ref, xo_ref, uo_ref = rest
        sc_sh = (sc_ref[0], sh_ref[0])
    else:
        sc_ref, sh_ref, wr_ref, xo_ref, uo_ref, idx_ref, wt_ref = rest
        sc_sh = (sc_ref[0], sh_ref[0])
    tm = x_ref.shape[0]
    rc = PROJ_LN_ROWS if proj else tm
    if mode == "route":
        w_cat = jnp.concatenate(_split_bf16(wr_ref[...]), axis=1)

    def project(t):
        return jnp.dot(a_ref[t * rc:(t + 1) * rc, :], w_ref[...], preferred_element_type=F32)

    def finish(t, y):
        rows = slice(t * rc, (t + 1) * rc)
        z = alpha * x_ref[rows, :] + (1.0 + g_ref[0]) * y
        xn, u = _ln_modulate(z, lg_ref[0], lb_ref[0], sc_sh)
        xo_ref[rows, :] = xn
        if mode == "next":
            uo_ref[rows, :] = u.astype(uo_ref.dtype)
        elif mode == "route":
            pitch = _row_pitch(chunks)
            _store_linear(uo_ref.at[t * rc * pitch:(t + 1) * rc * pitch, :], u, chunks)
            uh, ul = _split_bf16(u)
            r = (jnp.dot(uh, w_cat, preferred_element_type=F32) + jnp.dot(ul, w_cat, preferred_element_type=F32))
            idx, wt = _top2_route(r[:, :LANES] + r[:, LANES:], n_exp)
            idx_ref[rows, :] = idx
            wt_ref[rows, :] = wt

    if proj:
        _staged(tm // rc, project, finish)
    else:
        finish(0, y_ref[...].astype(F32))


def _resid_ln(x, y, ada_rows, ln_g, ln_b, *, alpha, seq, layer, comp_g, nxt, w_router_pad=None, n_exp=0):
    t, d = x.shape
    proj = isinstance(y, tuple)
    tm = 512 if proj else 256
    tpb = seq // tm
    chunks = d // LANES
    mode = "last" if nxt is None else ("route" if w_router_pad is not None else "next")
    row = lambda l, comp: pl.BlockSpec((1, 1, d), lambda i: (_ada_row(l, i // tpb, comp), 0, 0))
    tile = pl.BlockSpec((tm, d), lambda i: (i, 0))
    lanes = pl.BlockSpec((tm, LANES), lambda i: (i, 0))
    lnp = pl.BlockSpec((1, 1, d), lambda i: (layer, 0, 0))
    if proj:
        a, w_stack, j = y
        k = a.shape[1]
        in_specs = [tile, pl.BlockSpec((tm, k), lambda i: (i, 0)), pl.BlockSpec((None, k, d), lambda i: (j, 0, 0))]
        args = [x, a, w_stack]
    else:
        in_specs = [tile, tile]
        args = [x, y]
    in_specs += [row(layer, comp_g), lnp, lnp]
    args += [ada_rows, ln_g, ln_b]
    out_specs = [tile]
    out_shape = [jax.ShapeDtypeStruct((t, d), F32)]
    if mode != "last":
        in_specs += [row(nxt[0], nxt[1]), row(nxt[0], nxt[2])]
        args += [ada_rows, ada_rows]
    if mode == "next":
        out_specs.append(tile)
        out_shape.append(jax.ShapeDtypeStruct((t, d), BF16))
    elif mode == "route":
        pitch = _row_pitch(chunks)
        in_specs.append(pl.BlockSpec((d, LANES), lambda i: (0, 0)))
        args.append(w_router_pad)
        out_specs += [pl.BlockSpec((tm * pitch, LANES), lambda i: (i, 0)), lanes, lanes]
        out_shape += [jax.ShapeDtypeStruct((t * pitch, LANES), F32),
                      jax.ShapeDtypeStruct((t, LANES), jnp.int32), jax.ShapeDtypeStruct((t, LANES), F32)]
    return pl.pallas_call(
        functools.partial(_resid_ln_kernel, alpha=alpha, mode=mode, n_exp=n_exp, chunks=chunks, proj=proj),
        grid=(t // tm,),
        in_specs=in_specs, out_specs=out_specs, out_shape=out_shape,
        compiler_params=_cparams(1, 56 if proj else 48),
        name=("proj_ln_" if proj else "resid_ln_") + mode,
    )(*args)


def _rms(x, g):
    ms = jnp.mean(x * x, axis=-1, keepdims=True)
    return x * lax.rsqrt(ms + RMS_EPS) * g


def _mla_down_kernel(*refs, ql, kvl, modulate):
    if modulate:
        u_ref, sc_ref, sh_ref, w_ref, qn_ref, kvn_ref, c_ref, s_ref, cq_ref, ckv_ref, kr_ref = refs
    else:
        u_ref, w_ref, qn_ref, kvn_ref, c_ref, s_ref, cq_ref, ckv_ref, kr_ref = refs
    rc = EPILOGUE_ROWS

    def matmul(t):
        u = u_ref[t * rc:(t + 1) * rc, :]
        if modulate:
            u = (u * (1.0 + sc_ref[0]) + sh_ref[0]).astype(BF16)
        return jnp.dot(u, w_ref[...], preferred_element_type=F32)

    def norm_rotate(t, acc):
        rows = slice(t * rc, (t + 1) * rc)
        cq_ref[rows, :] = _rms(acc[:, :ql], qn_ref[0]).astype(BF16)
        ckv_ref[rows, :] = _rms(acc[:, ql:ql + kvl], kvn_ref[0]).astype(BF16)
        xr = acc[:, ql + kvl:]
        kr_ref[rows, :] = (xr * c_ref[rows, :] + pltpu.roll(xr, LANES // 2, 1) * s_ref[rows, :]).astype(BF16)

    _staged(u_ref.shape[0] // rc, matmul, norm_rotate)


def _mla_down(u, w_perm, q_norm, kv_norm, layer, cos_t, sin_t, ql, kvl, mod=None):
    t, d = u.shape
    n = w_perm.shape[1]
    tm = 512
    nrm = lambda width: pl.BlockSpec((1, 1, width), lambda i: (layer, 0, 0))
    rows = lambda width: pl.BlockSpec((tm, width), lambda i: (i, 0))
    in_specs = [rows(d)]
    args = [u]
    if mod is not None:
        ada_rows, seq, ada_layer, comp_sc, comp_sh = mod
        tpb = seq // tm
        ada = lambda comp: pl.BlockSpec((1, 1, d), lambda i: (_ada_row(ada_layer, i // tpb, comp), 0, 0))
        in_specs += [ada(comp_sc), ada(comp_sh)]
        args += [ada_rows, ada_rows]
    in_specs += [pl.BlockSpec((d, n), lambda i: (0, 0)), nrm(ql), nrm(kvl), rows(LANES), rows(LANES)]
    args += [w_perm, q_norm, kv_norm, cos_t, sin_t]
    return pl.pallas_call(
        functools.partial(_mla_down_kernel, ql=ql, kvl=kvl, modulate=mod is not None),
        grid=(t // tm,),
        in_specs=in_specs,
        out_specs=[rows(ql), rows(kvl), rows(LANES)],
        out_shape=[jax.ShapeDtypeStruct((t, ql), BF16), jax.ShapeDtypeStruct((t, kvl), BF16),
                   jax.ShapeDtypeStruct((t, LANES), BF16)],
        compiler_params=_cparams(1, 48),
        name="mla_down",
    )(*args)


def _mla_up_kernel(cq_ref, ckv_ref, wq_ref, wkv_ref, c_ref, s_ref, q_ref, kv_ref, *, scale, heads):
    rc = EPILOGUE_ROWS

    def matmul(t):
        rows = slice(t * rc, (t + 1) * rc)
        return (jnp.dot(cq_ref[rows, :], wq_ref[...], preferred_element_type=F32),
                jnp.dot(ckv_ref[rows, :], wkv_ref[...], preferred_element_type=F32))

    def rotate(t, accs):
        acc, acc_kv = accs
        rows = slice(t * rc, (t + 1) * rc)
        kv_ref[rows, :] = acc_kv.astype(BF16)
        c = c_ref[rows, :]
        s = s_ref[rows, :]
        for h in range(heads):
            b0 = h * MLA_HEAD_PAD
            q_ref[rows, b0:b0 + LANES] = (acc[:, b0:b0 + LANES] * scale).astype(BF16)
            xr = acc[:, b0 + LANES:b0 + MLA_HEAD_PAD]
            q_ref[rows, b0 + LANES:b0 + MLA_HEAD_PAD] = (
                (xr * c + pltpu.roll(xr, LANES // 2, 1) * s) * scale).astype(BF16)

    _staged(cq_ref.shape[0] // rc, matmul, rotate)


def _mla_up(cq, ckv, wq_perm, wkv_stack, layer, cos_t, sin_t, scale):
    t, k = cq.shape
    n = wq_perm.shape[1]
    assert wkv_stack.shape[1:] == (ckv.shape[1], n)
    tm, tn = 1024, 1024
    rows = pl.BlockSpec((tm, LANES), lambda j, i: (i, 0))
    lat = pl.BlockSpec((tm, k), lambda j, i: (i, 0))
    out = pl.BlockSpec((tm, tn), lambda j, i: (i, j))
    return pl.pallas_call(
        functools.partial(_mla_up_kernel, scale=scale, heads=tn // MLA_HEAD_PAD),
        grid=(n // tn, t // tm),
        in_specs=[lat, lat, pl.BlockSpec((k, tn), lambda j, i: (0, j)),
                  pl.BlockSpec((None, k, tn), lambda j, i: (layer, 0, j)), rows, rows],
        out_specs=[out, out],
        out_shape=[jax.ShapeDtypeStruct((t, n), BF16), jax.ShapeDtypeStruct((t, n), BF16)],
        compiler_params=_cparams(2, 48),
        name="mla_up",
    )(cq, ckv, wq_perm, wkv_stack, cos_t, sin_t)


def _moba_qkv_kernel(u_ref, w_ref, tab_ref, o_ref, wb_ref, *, heads):
    @pl.when(pl.program_id(1) == 0)
    def _():
        wb_ref[...] = w_ref[...].astype(BF16)

    half = MOBA_ROT_DIM // 2
    rc = EPILOGUE_ROWS

    def matmul(t):
        return jnp.dot(u_ref[t * rc:(t + 1) * rc, :], wb_ref[...], preferred_element_type=F32)

    def rotate(t, acc):
        rows = slice(t * rc, (t + 1) * rc)
        c = tab_ref[rows, :LANES]
        s1 = tab_ref[rows, LANES:2 * LANES]
        s2 = tab_ref[rows, 2 * LANES:]
        for h in range(heads):
            x = acc[:, h * LANES:(h + 1) * LANES]
            r = x * c + pltpu.roll(x, half, 1) * s1 + pltpu.roll(x, LANES - half, 1) * s2
            o_ref[rows, h * LANES:(h + 1) * LANES] = r.astype(BF16)

    _staged(u_ref.shape[0] // rc, matmul, rotate)


def _moba_qkv(u, w_stack, layer, tables):
    t, k = u.shape
    n = w_stack.shape[2]
    tm, tn = 2048, 512
    tiles_per_sec = (n // 3) // tn
    return pl.pallas_call(
        functools.partial(_moba_qkv_kernel, heads=tn // LANES),
        grid=(n // tn, t // tm),
        in_specs=[pl.BlockSpec((tm, k), lambda j, i: (i, 0)),
                  pl.BlockSpec((None, k, tn), lambda j, i: (layer, 0, j)),
                  pl.BlockSpec((None, tm, 3 * LANES), lambda j, i: (j // tiles_per_sec, i, 0))],
        out_specs=pl.BlockSpec((tm, tn), lambda j, i: (i, j)),
        out_shape=jax.ShapeDtypeStruct((t, n), BF16),
        scratch_shapes=[pltpu.VMEM((k, tn), BF16)],
        compiler_params=_cparams(2, 48),
        name="moba_qkv",
    )(u, w_stack, tables)


_NT = (((1,), (1,)), ((), ()))


def _softmax_numer(s):
    m = jnp.max(s, axis=-1, keepdims=True)
    return jnp.exp(s - m).astype(BF16)


def _fill_values_ones(vext_ref, v):
    vext_ref[:, :LANES] = v
    vext_ref[:, LANES:] = jnp.ones_like(v)


def _normalised_pv(p, vext):
    o = jnp.dot(p, vext, preferred_element_type=F32)
    return o[:, :LANES] / o[:, LANES:LANES + 1]


def _staged(n_tiles, *stages, reverse=False):
    vals = {}
    for t in range(n_tiles + len(stages) - 1):
        for k, stage in enumerate(stages):
            if 0 <= t - k < n_tiles:
                tile = n_tiles - 1 - (t - k) if reverse else t - k
                vals[tile] = stage(tile) if k == 0 else stage(tile, vals[tile])


def _mla_attn_kernel(q_ref, kv_ref, kr_ref, o_ref, kfull_ref, vext_ref, *, tq):
    seq = q_ref.shape[0]
    kfull_ref[:, :LANES] = kv_ref[:, :LANES]
    kfull_ref[:, LANES:] = kr_ref[...]
    _fill_values_ones(vext_ref, kv_ref[:, LANES:])
    row = lax.broadcasted_iota(jnp.int32, (tq, tq), 0)
    col = lax.broadcasted_iota(jnp.int32, (tq, tq), 1)
    causal = col <= row

    def scores(n):
        q = q_ref[n * tq:(n + 1) * tq, :]
        return lax.dot_general(q, kfull_ref[0:(n + 1) * tq, :], _NT, preferred_element_type=F32)

    def probs(n, s):
        diag = jnp.where(causal, s[:, n * tq:], -jnp.inf)
        return _softmax_numer(jnp.concatenate([s[:, :n * tq], diag], axis=1) if n else diag)

    def output(n, p):
        o_ref[n * tq:(n + 1) * tq, :] = _normalised_pv(p, vext_ref[0:(n + 1) * tq, :]).astype(o_ref.dtype)

    _staged(seq // tq, scores, probs, output, reverse=True)


def _mla_attention(q, kv, kr, batch, seq, heads):
    t = q.shape[0]
    return pl.pallas_call(
        functools.partial(_mla_attn_kernel, tq=256),
        grid=(batch, heads),
        in_specs=[pl.BlockSpec((seq, MLA_HEAD_PAD), lambda b, h: (b, h)),
                  pl.BlockSpec((seq, MLA_NOPE + MLA_V), lambda b, h: (b, h)),
                  pl.BlockSpec((seq, LANES), lambda b, h: (b, 0))],
        out_specs=pl.BlockSpec((seq, MLA_V), lambda b, h: (b, h)),
        out_shape=jax.ShapeDtypeStruct((t, heads * MLA_V), BF16),
        scratch_shapes=[pltpu.VMEM((seq, MLA_HEAD_PAD), BF16), pltpu.VMEM((seq, 2 * LANES), BF16)],
        compiler_params=_cparams(2, 48),
        name="mla_attn",
    )(q, kv, kr)


def _moba_attn_kernel(q_ref, k_ref, v_ref, o_ref, vext_ref, *, nb):
    blk = MOBA_BLOCK
    seq = k_ref.shape[0]
    _fill_values_ones(vext_ref, v_ref[...])
    r = lax.broadcasted_iota(jnp.int32, (LANES, seq), 0)
    c = lax.broadcasted_iota(jnp.int32, (LANES, seq), 1)
    ind = jnp.where(c // blk == r, 1.0 / blk, 0.0).astype(BF16)
    km = jnp.dot(ind, k_ref[...], preferred_element_type=F32)
    kmh, kml = _split_bf16(km)
    row = lax.broadcasted_iota(jnp.int32, (blk, blk), 0)
    col = lax.broadcasted_iota(jnp.int32, (blk, blk), 1)
    causal = col <= row
    lane = lax.broadcasted_iota(jnp.int32, (blk, LANES), 1)

    def scores(n):
        q = q_ref[n * blk:(n + 1) * blk, :]
        s = lax.dot_general(q, k_ref[0:(n + 1) * blk, :], _NT, preferred_element_type=F32)
        if n <= MOBA_TOPK:
            return s, None
        gate_t = (lax.dot_general(kmh, q, _NT, preferred_element_type=F32)
                  + lax.dot_general(kml, q, _NT, preferred_element_type=F32))[:SUBLANES, :]
        return s, gate_t

    def probs(n, s_gate):
        s, gate_t = s_gate
        if gate_t is not None:
            blk_id = lax.broadcasted_iota(jnp.int32, gate_t.shape, 0)
            keep_t = jnp.zeros(gate_t.shape, F32)
            for j in range(n):
                gj = gate_t[j:j + 1, :]
                beats = (blk_id < n) & ((gate_t > gj) | ((gate_t == gj) & (blk_id < j)))
                n_beats = jnp.sum(beats.astype(F32), axis=0, keepdims=True)
                keep_t = jnp.where((blk_id == j) & (n_beats < MOBA_TOPK), 1.0, keep_t)
            keep = jnp.concatenate([keep_t, jnp.zeros((LANES - SUBLANES, blk), F32)], axis=0).T
        parts = []
        for j in range(n):
            sj = s[:, j * blk:(j + 1) * blk]
            if gate_t is not None:
                sj = jnp.where(keep[:, j:j + 1] > 0.5, sj, -jnp.inf)
            parts.append(sj)
        parts.append(jnp.where(causal, s[:, n * blk:], -jnp.inf))
        return _softmax_numer(jnp.concatenate(parts, axis=1) if n else parts[0])

    def output(n, p):
        o_ref[n * blk:(n + 1) * blk, :] = _normalised_pv(p, vext_ref[0:(n + 1) * blk, :]).astype(o_ref.dtype)

    _staged(nb, scores, probs, output, reverse=True)


def _moba_attention(qkv, batch, seq, heads):
    t = qkv.shape[0]
    d = MOBA_HEAD_DIM
    return pl.pallas_call(
        functools.partial(_moba_attn_kernel, nb=seq // MOBA_BLOCK),
        grid=(batch, heads),
        in_specs=[pl.BlockSpec((seq, d), lambda b, h: (b, h)),
                  pl.BlockSpec((seq, d), lambda b, h: (b, heads + h)),
                  pl.BlockSpec((seq, d), lambda b, h: (b, 2 * heads + h))],
        out_specs=pl.BlockSpec((seq, d), lambda b, h: (b, h)),
        out_shape=jax.ShapeDtypeStruct((t, heads * d), BF16),
        scratch_shapes=[pltpu.VMEM((seq, 2 * LANES), BF16)],
        compiler_params=_cparams(2, 48),
        name="moba_attn",
    )(qkv, qkv, qkv)


def _swiglu_partial(x, wg, wu, wd):
    g = jnp.dot(x, wg, preferred_element_type=F32)
    u = jnp.dot(x, wu, preferred_element_type=F32)
    a = (g * jax.nn.sigmoid(g) * u).astype(BF16)
    return jnp.dot(a, wd, preferred_element_type=F32)


def _ffn_kernel(u_ref, wg_ref, wu_ref, wd_ref, o_ref):
    f = pl.program_id(1)
    @pl.when(f == 0)
    def _():
        o_ref[...] = jnp.zeros_like(o_ref)

    o_ref[...] += _swiglu_partial(u_ref[...], wg_ref[...].astype(BF16), wu_ref[...].astype(BF16),
                                  wd_ref[...].astype(BF16))


def _dense_ffn(u, w_gate_up, w_down, layer):
    t, d = u.shape
    dff = w_down.shape[1]
    tm, tf = 1024, 256
    nf = dff // tf
    return pl.pallas_call(
        _ffn_kernel,
        grid=(t // tm, nf),
        in_specs=[pl.BlockSpec((tm, d), lambda i, f: (i, 0)),
                  pl.BlockSpec((None, d, tf), lambda i, f: (layer, 0, f)),
                  pl.BlockSpec((None, d, tf), lambda i, f: (layer, 0, nf + f)),
                  pl.BlockSpec((None, tf, d), lambda i, f: (layer, f, 0))],
        out_specs=pl.BlockSpec((tm, d), lambda i, f: (i, 0)),
        out_shape=jax.ShapeDtypeStruct((t, d), F32),
        compiler_params=_cparams(2, 56),
        name="dense_ffn",
    )(u, w_gate_up, w_gate_up, w_down)


def _route_metadata(idx2, n_exp):
    rb, st = MOE_ROW_BLOCK, MOE_SUPER_BLOCKS
    t = idx2.shape[0]
    a = t * TOP_K
    n_over = (a // rb + n_exp - 2) // st
    e_flat = idx2.reshape(a)
    onehot = (e_flat[:, None] == jnp.arange(n_exp, dtype=jnp.int32)[None, :]).astype(jnp.int32)
    csum = jnp.cumsum(onehot, axis=0)
    rank = jnp.sum((csum - onehot) * onehot, axis=1)
    counts = csum[-1]
    nsub = (counts + rb - 1) // rb
    sub_start = jnp.cumsum(nsub) - nsub
    dest = jnp.sum(onehot * (sub_start * rb)[None, :], axis=1) + rank
    p_rows = (a // rb + n_exp) * rb
    token_flat = jnp.arange(a, dtype=jnp.int32) // TOP_K
    row_token = jnp.zeros((p_rows,), jnp.int32).at[dest].set(token_flat)
    i32 = lambda *xs: tuple(x.astype(jnp.int32) for x in xs)
    experts = jnp.arange(n_exp, dtype=jnp.int32)
    first_row0 = jnp.concatenate([sub_start * rb, jnp.sum(nsub, keepdims=True) * rb])
    first = i32(experts, first_row0, jnp.minimum(nsub, st))
    n_it = jnp.maximum((nsub + st - 1) // st - 1, 0)
    it_end = jnp.cumsum(n_it)
    it_start = it_end - n_it
    w = jnp.arange(n_over, dtype=jnp.int32)
    e_w = jnp.sum((it_end[None, :] <= w[:, None]).astype(jnp.int32), axis=1)
    active = e_w < n_exp
    e_c = jnp.minimum(e_w, n_exp - 1)
    local = w - it_start[e_c] + 1
    over_nsub = jnp.where(active, jnp.clip(nsub[e_c] - local * st, 0, st), 0)
    over_row0 = jnp.where(active, (sub_start[e_c] + local * st) * rb, 0)
    e_last = jnp.max(jnp.where(n_it > 0, experts, 0))
    overflow = i32(jnp.where(active, e_c, e_last), over_row0, over_nsub)
    return dest.astype(jnp.int32), row_token, first, overflow, it_end[-1] > 0


ROW_DMA_UNROLL = 8


def _row_copy(src_hbm, src_row, dst, dst_row, sem, chunks):
    pitch = _row_pitch(chunks)
    return pltpu.make_async_copy(src_hbm.at[pl.ds(src_row * pitch, chunks), :],
                                 dst.at[pl.ds(dst_row * pitch, chunks), :], sem)


def _rows_wait(src_hbm, dst, sem, rows, chunks):
    n = rows * chunks
    pltpu.make_async_copy(src_hbm.at[pl.ds(0, n), :], dst.at[pl.ds(0, n), :], sem).wait()


def _dispatch_kernel(tok_ref, u_hbm, o_ref, stage_ref, sem, *, rb, chunks):
    i = pl.program_id(0)
    pitch = _row_pitch(chunks)

    def issue(step, slot):
        def body(g, c):
            for k in range(ROW_DMA_UNROLL):
                r = g * ROW_DMA_UNROLL + k
                _row_copy(u_hbm, tok_ref[step * rb + r], stage_ref.at[slot], r, sem.at[slot],
                          chunks).start(priority=k % 2)
            return c
        lax.fori_loop(0, rb // ROW_DMA_UNROLL, body, 0)

    @pl.when(i == 0)
    def _():
        issue(0, 0)

    @pl.when(i + 1 < pl.num_programs(0))
    def _():
        issue(i + 1, (i + 1) % 2)

    slot = i % 2
    _rows_wait(u_hbm, stage_ref.at[slot], sem.at[slot], rb, chunks)
    for c in range(chunks):
        o_ref[:, c * LANES:(c + 1) * LANES] = stage_ref[slot, pl.ds(c, rb, stride=pitch), :].astype(o_ref.dtype)


def _dispatch(u_lin, row_token, d):
    p_rows = row_token.shape[0]
    rb = 2 * MOE_ROW_BLOCK if p_rows % (2 * MOE_ROW_BLOCK) == 0 else MOE_ROW_BLOCK
    chunks = d // LANES
    return pl.pallas_call(
        functools.partial(_dispatch_kernel, rb=rb, chunks=chunks),
        grid_spec=pltpu.PrefetchScalarGridSpec(
            num_scalar_prefetch=1,
            grid=(p_rows // rb,),
            in_specs=[pl.BlockSpec(memory_space=pl.ANY)],
            out_specs=pl.BlockSpec((rb, d), lambda i, tok: (i, 0)),
            scratch_shapes=[pltpu.VMEM((2, rb * _row_pitch(chunks), LANES), F32),
                            pltpu.SemaphoreType.DMA((2,))]),
        out_shape=jax.ShapeDtypeStruct((p_rows, d), BF16),
        compiler_params=_cparams(1, 40),
        name="moe_dispatch",
    )(row_token, u_lin)


def _moe_ffn_kernel(*refs, rb, nf, n_items, chunks, fill_tail):
    e_ref, row0_ref, nsub_ref, xs_hbm, wg_ref, wu_ref, wd_ref = refs[:7]
    ys_hbm, x_ref, acc_ref, wgb_ref, wub_ref, wdb_ref, stage_ref, sem = refs[7 if fill_tail else 8:]
    w = pl.program_id(0)
    f = pl.program_id(1)
    nsub = nsub_ref[w]
    row0 = row0_ref[w]

    def for_range(n, fn):
        def body(r, c):
            fn(r)
            return c
        lax.fori_loop(0, n, body, 0)

    def load(r):
        return pltpu.make_async_copy(xs_hbm.at[pl.ds(pl.multiple_of(row0 + r * rb, rb), rb), :],
                                     x_ref.at[pl.ds(pl.multiple_of(r * rb, rb), rb), :], sem.at[0])

    pitch = _row_pitch(chunks)

    def store(row, slot):
        return pltpu.make_async_copy(
            stage_ref.at[slot], ys_hbm.at[pl.ds(pl.multiple_of(row * pitch, rb * pitch), rb * pitch), :],
            sem.at[1 + slot])

    @pl.when(f == 0)
    def _():
        for_range(nsub, lambda r: load(r).start())
        acc_ref[...] = jnp.zeros_like(acc_ref)
        for_range(nsub, lambda r: load(r).wait())

    @pl.when(nsub > 0)
    def _():
        wgb_ref[...] = wg_ref[...].astype(BF16)
        wub_ref[...] = wu_ref[...].astype(BF16)
        wdb_ref[...] = wd_ref[...].astype(BF16)

        def chunk(start, size):
            rows = pl.ds(pl.multiple_of(start, rb), size)
            acc_ref[rows, :] += _swiglu_partial(x_ref[rows, :], wgb_ref[...], wub_ref[...], wdb_ref[...])

        for_range(nsub // 2, lambda p: chunk(p * (2 * rb), 2 * rb))

        @pl.when(nsub % 2 == 1)
        def _():
            chunk((nsub - 1) * rb, rb)

    @pl.when(f == nf - 1)
    def _():
        def emit(r):
            slot = r % 2

            @pl.when(r >= 2)
            def _():
                store(row0 + (r - 2) * rb, slot).wait()

            _store_linear(stage_ref.at[slot], acc_ref[pl.ds(pl.multiple_of(r * rb, rb), rb), :], chunks)
            store(row0 + r * rb, slot).start()

        for_range(nsub, emit)

        @pl.when(nsub >= 2)
        def _():
            store(row0, nsub % 2).wait()

        @pl.when(nsub >= 1)
        def _():
            store(row0, (nsub - 1) % 2).wait()

    if fill_tail:
        @pl.when((f == nf - 1) & (w == n_items - 1))
        def _():
            used = row0_ref[n_items]
            n_tail = (ys_hbm.shape[0] // pitch - used) // rb
            stage_ref[0] = jnp.zeros(stage_ref.shape[1:], F32)
            for_range(n_tail, lambda r: store(used + r * rb, 0).start())
            for_range(n_tail, lambda r: store(used + r * rb, 0).wait())


def _moe_ffn(xs, w_gate_up, w_down, layer, items, ys_in=None):
    item_e, item_row0, item_nsub = items
    extra = () if ys_in is None else (ys_in,)
    hbm = pl.BlockSpec(memory_space=pl.ANY)
    p_rows, d = xs.shape
    dff = w_down.shape[2]
    rb, st = MOE_ROW_BLOCK, MOE_SUPER_BLOCKS
    n_items = item_e.shape[0]
    chunks = d // LANES
    tf = 256
    nf = dff // tf

    def f_eff(f, nsub, w):
        return jnp.where(nsub[w] > 0, f, nf - 1)

    return pl.pallas_call(
        functools.partial(_moe_ffn_kernel, rb=rb, nf=nf, n_items=n_items, chunks=chunks, fill_tail=ys_in is None),
        grid_spec=pltpu.PrefetchScalarGridSpec(
            num_scalar_prefetch=3,
            grid=(n_items, nf),
            in_specs=[hbm,
                      pl.BlockSpec((None, None, d, tf), lambda w, f, e, r0, ns: (layer, e[w], 0, f_eff(f, ns, w))),
                      pl.BlockSpec((None, None, d, tf),
                                   lambda w, f, e, r0, ns: (layer, e[w], 0, nf + f_eff(f, ns, w))),
                      pl.BlockSpec((None, None, tf, d), lambda w, f, e, r0, ns: (layer, e[w], f_eff(f, ns, w), 0))]
            + [hbm] * len(extra),
            out_specs=hbm,
            scratch_shapes=[pltpu.VMEM((st * rb, d), BF16), pltpu.VMEM((st * rb, d), F32),
                            pltpu.VMEM((d, tf), BF16), pltpu.VMEM((d, tf), BF16), pltpu.VMEM((tf, d), BF16),
                            pltpu.VMEM((2, rb * _row_pitch(chunks), LANES), F32), pltpu.SemaphoreType.DMA((3,))]),
        out_shape=jax.ShapeDtypeStruct((p_rows * _row_pitch(chunks), LANES), F32),
        input_output_aliases={} if ys_in is None else {7: 0},
        compiler_params=_cparams(2, 58),
        name="moe_ffn",
    )(item_e, item_row0, item_nsub, xs, w_gate_up, w_gate_up, w_down, *extra)


def _moe_combine_ln_kernel(*refs, alpha, emit_u, tm, chunks):
    if emit_u:
        (pos_ref, ys_hbm, wt_ref, x_ref, g_ref, lg_ref, lb_ref, sc_ref, sh_ref, xo_ref, uo_ref,
         stage_ref, y_ref, sem) = refs
        sc_sh = (sc_ref[0], sh_ref[0])
    else:
        pos_ref, ys_hbm, wt_ref, x_ref, g_ref, lg_ref, lb_ref, xo_ref, stage_ref, y_ref, sem = refs
        sc_sh = None
    i = pl.program_id(0)
    pitch = _row_pitch(chunks)

    def issue(step, slot):
        def body(g, c):
            for j in range(ROW_DMA_UNROLL // TOP_K):
                t = g * (ROW_DMA_UNROLL // TOP_K) + j
                for k in range(TOP_K):
                    _row_copy(ys_hbm, pos_ref[(step * tm + t) * TOP_K + k], stage_ref.at[slot, k], t,
                              sem.at[slot, k], chunks).start(priority=k % 2)
            return c
        lax.fori_loop(0, tm * TOP_K // ROW_DMA_UNROLL, body, 0)

    @pl.when(i == 0)
    def _():
        issue(0, 0)

    @pl.when(i + 1 < pl.num_programs(0))
    def _():
        issue(i + 1, (i + 1) % 2)

    slot = i % 2
    for k in range(TOP_K):
        _rows_wait(ys_hbm, stage_ref.at[slot, k], sem.at[slot, k], tm, chunks)
    w1 = wt_ref[:, 0:1]
    w2 = wt_ref[:, 1:2]
    for c in range(chunks):
        y_ref[:, c * LANES:(c + 1) * LANES] = (stage_ref[slot, 0, pl.ds(c, tm, stride=pitch), :] * w1
                                               + stage_ref[slot, 1, pl.ds(c, tm, stride=pitch), :] * w2)
    z = alpha * x_ref[...] + (1.0 + g_ref[0]) * y_ref[...]
    xn, u = _ln_modulate(z, lg_ref[0], lb_ref[0], sc_sh)
    xo_ref[...] = xn
    if emit_u:
        uo_ref[...] = u.astype(uo_ref.dtype)


def _moe_combine_ln(ys_lin, pos, wt, x, ada_rows, ln_g, ln_b, *, alpha, seq, layer, comp_g, nxt):
    t, d = x.shape
    tm = 256
    tpb = seq // tm
    chunks = d // LANES
    row = lambda l, comp: pl.BlockSpec((1, 1, d), lambda i, p: (_ada_row(l, i // tpb, comp), 0, 0))
    tile = pl.BlockSpec((tm, d), lambda i, p: (i, 0))
    lnp = pl.BlockSpec((1, 1, d), lambda i, p: (layer, 0, 0))
    in_specs = [pl.BlockSpec(memory_space=pl.ANY), pl.BlockSpec((tm, LANES), lambda i, p: (i, 0)), tile,
                row(layer, comp_g), lnp, lnp]
    args = [ys_lin, wt, x, ada_rows, ln_g, ln_b]
    out_specs = [tile]
    out_shape = [jax.ShapeDtypeStruct((t, d), F32)]
    if nxt is not None:
        in_specs += [row(nxt[0], nxt[1]), row(nxt[0], nxt[2])]
        args += [ada_rows, ada_rows]
        out_specs.append(tile)
        out_shape.append(jax.ShapeDtypeStruct((t, d), BF16))
    outs = pl.pallas_call(
        functools.partial(_moe_combine_ln_kernel, alpha=alpha, emit_u=nxt is not None, tm=tm, chunks=chunks),
        grid_spec=pltpu.PrefetchScalarGridSpec(
            num_scalar_prefetch=1,
            grid=(t // tm,),
            in_specs=in_specs, out_specs=out_specs,
            scratch_shapes=[pltpu.VMEM((2, TOP_K, tm * _row_pitch(chunks), LANES), F32), pltpu.VMEM((tm, d), F32),
                            pltpu.SemaphoreType.DMA((2, TOP_K))]),
        out_shape=out_shape,
        compiler_params=_cparams(1, 48),
        name="moe_combine_ln",
    )(pos, *args)
    return (outs[0], outs[1]) if nxt is not None else (outs[0], None)


def _rope_cos_sin(positions, dim):
    inv_freq = ROPE_THETA ** (-jnp.arange(0, dim, 2, dtype=F32) / dim)
    ang = positions.astype(F32).reshape(-1)[:, None] * inv_freq
    return jnp.cos(ang), jnp.sin(ang)


def _mla_rope_tables(positions):
    cos, sin = _rope_cos_sin(positions, MLA_ROPE)
    z = jnp.zeros_like(cos)
    return jnp.concatenate([cos, z, cos, z], axis=1), jnp.concatenate([-sin, z, sin, z], axis=1)


def _moba_rope_tables(positions, scale):
    cos, sin = _rope_cos_sin(positions, MOBA_ROT_DIM)
    t, half = cos.shape
    rest = LANES - 2 * half
    c = jnp.concatenate([cos, cos, jnp.ones((t, rest), F32)], axis=1)
    s1 = jnp.concatenate([jnp.zeros((t, half), F32), sin, jnp.zeros((t, rest), F32)], axis=1)
    s2 = jnp.concatenate([-sin, jnp.zeros((t, half + rest), F32)], axis=1)
    rot = jnp.concatenate([c, s1, s2], axis=1)
    ident = jnp.concatenate([jnp.ones((t, LANES), F32), jnp.zeros((t, 2 * LANES), F32)], axis=1)
    return jnp.stack([rot * scale, rot, ident])


def _spread_rope_cols(w_rope):
    half = MLA_ROPE // 2
    z = jnp.zeros(w_rope.shape[:-1] + (LANES // 2 - half,), w_rope.dtype)
    return jnp.concatenate([w_rope[..., :half], z, w_rope[..., half:], z], axis=-1)


def _mla_weights(w_down, w_uq, ql, kvl):
    k = w_uq.shape[0]
    heads = w_uq.shape[1] // (MLA_NOPE + MLA_ROPE)
    wd = jnp.concatenate([w_down[:, :ql + kvl], _spread_rope_cols(w_down[:, ql + kvl:])], axis=1).astype(BF16)
    wq = w_uq.reshape(k, heads, MLA_NOPE + MLA_ROPE)
    wq = jnp.concatenate([wq[..., :MLA_NOPE], _spread_rope_cols(wq[..., MLA_NOPE:])], axis=-1)
    return wd, wq.reshape(k, heads * MLA_HEAD_PAD).astype(BF16), heads


def kernel(x, c, positions, w_ada, b_ada, ln_mix_g, ln_mix_b, ln_ffn_g, ln_ffn_b, mla_w_down, mla_q_norm,
           mla_kv_norm, mla_w_uq, mla_w_ukv, mla_w_o, moba_w_qkv, moba_w_o, ffn_w_gate_up, ffn_w_down,
           moe_w_router, moe_w_gate_up, moe_w_down):
    batch, seq, d = x.shape
    depth = w_ada.shape[0]
    t = batch * seq
    alpha = (2.0 * depth) ** 0.25
    ql = mla_q_norm.shape[1]
    kvl = mla_kv_norm.shape[1]
    n_exp = moe_w_router.shape[2]

    ada_rows = _ada_all(c, w_ada, b_ada)
    cos_mla, sin_mla = _mla_rope_tables(positions)
    moba_tables = _moba_rope_tables(positions, MOBA_HEAD_DIM ** -0.5)
    mla_w_ukv_b = mla_w_ukv.astype(BF16)
    mla_w_o_b = mla_w_o.astype(BF16)
    moba_w_o_b = moba_w_o.astype(BF16)
    ln3 = lambda p: p.reshape(depth, 1, d)
    ln_mix_g, ln_mix_b, ln_ffn_g, ln_ffn_b = ln3(ln_mix_g), ln3(ln_mix_b), ln3(ln_ffn_g), ln3(ln_ffn_b)
    q_norm3 = mla_q_norm.reshape(-1, 1, ql)
    kv_norm3 = mla_kv_norm.reshape(-1, 1, kvl)

    xf = x.reshape(t, d)
    u = None
    for l in range(depth):
        j = l // 2
        moe_layer = l % 2 == 1
        if l % 2 == 0:
            wd_p, wq_p, heads = _mla_weights(mla_w_down[j], mla_w_uq[j], ql, kvl)
            first = (xf, (ada_rows, seq, l, 1, 0)) if u is None else (u, None)
            cq, ckv, kr = _mla_down(first[0], wd_p, q_norm3, kv_norm3, j, cos_mla, sin_mla, ql, kvl, mod=first[1])
            q, kv = _mla_up(cq, ckv, wq_p, mla_w_ukv_b, j, cos_mla, sin_mla, (MLA_NOPE + MLA_ROPE) ** -0.5)
            o = _mla_attention(q, kv, kr, batch, seq, heads)
            y = (o, mla_w_o_b, j)
        else:
            heads = moba_w_qkv.shape[2] // (3 * MOBA_HEAD_DIM)
            qkv = _moba_qkv(u, moba_w_qkv, j, moba_tables)
            o = _moba_attention(qkv, batch, seq, heads)
            y = (o, moba_w_o_b, j)
        ln_args = dict(alpha=alpha, seq=seq, layer=l)
        nxt = (l + 1, 1, 0) if l + 1 < depth else None
        if not moe_layer:
            xf, u = _resid_ln(xf, y, ada_rows, ln_mix_g, ln_mix_b, comp_g=2, nxt=(l, 4, 3), **ln_args)
            y = _dense_ffn(u, ffn_w_gate_up, ffn_w_down, j)
            outs = _resid_ln(xf, y, ada_rows, ln_ffn_g, ln_ffn_b, comp_g=5, nxt=nxt, **ln_args)
            xf, u = outs[0], (outs[1] if nxt is not None else None)
        else:
            w_router_pad = jnp.zeros((d, LANES), F32).at[:, :n_exp].set(moe_w_router[j])
            xf, u_lin, idx, wt = _resid_ln(xf, y, ada_rows, ln_mix_g, ln_mix_b, comp_g=2, nxt=(l, 4, 3),
                                           w_router_pad=w_router_pad, n_exp=n_exp, **ln_args)
            pos, row_token, first, overflow, any_overflow = _route_metadata(idx[:, :TOP_K], n_exp)
            xs = _dispatch(u_lin, row_token, d)
            ys_lin = _moe_ffn(xs, moe_w_gate_up, moe_w_down, j, first)
            ys_lin = lax.cond(any_overflow,
                              lambda ys: _moe_ffn(xs, moe_w_gate_up, moe_w_down, j, overflow, ys_in=ys),
                              lambda ys: ys, ys_lin)
            xf, u = _moe_combine_ln(ys_lin, pos, wt, xf, ada_rows, ln_ffn_g, ln_ffn_b, comp_g=5, nxt=nxt, **ln_args)
    return xf.reshape(batch, seq, d)
```

```python
import functools

import jax
import jax.numpy as jnp
from jax import lax
from jax.experimental import pallas as pl
from jax.experimental.pallas import tpu as pltpu

F32 = jnp.float32
BF16 = jnp.bfloat16

ROPE_THETA = 500000.0
LN_EPS = 1e-5
RMS_EPS = 1e-6
MLA_NOPE = 128
MLA_ROPE = 64
MLA_V = 128
MOBA_HEAD_DIM = 128
MOBA_ROT_DIM = 32
MOBA_BLOCK = 256
MOBA_TOPK = 3
TOP_K = 2
LOG2_E = 1.4426950408889634
MASKED_SCORE = -1e30

LANES = 128
SUBLANES = 8
MLA_HEAD_PAD = 2 * LANES

ADA_BATCH_PAD = SUBLANES
MOE_ROW_BLOCK = 256
MOE_SUPER_BLOCKS = 10
EPILOGUE_ROWS = 256
PROJ_LN_ROWS = 128


def _cparams(n_axes, vmem_mb):
    return pltpu.CompilerParams(dimension_semantics=("arbitrary",) * n_axes,
                                vmem_limit_bytes=vmem_mb * 1024 * 1024)


def _split_bf16(x):
    hi = x.astype(BF16)
    return hi, (x - hi.astype(F32)).astype(BF16)


def _ada_row(layer, batch, comp):
    return (layer * ADA_BATCH_PAD + batch) * 6 + comp


def _ada_kernel(c_ref, w_ref, b_ref, o_ref):
    c = c_ref[...]
    ca = (c * jax.nn.sigmoid(c)).astype(BF16)
    o_ref[...] = jnp.dot(ca, w_ref[...].astype(BF16), preferred_element_type=F32) + b_ref[...]


def _ada_all(c, w_ada, b_ada):
    depth, d, n6 = w_ada.shape
    b = c.shape[0]
    c_pad = jnp.zeros((ADA_BATCH_PAD, d), F32).at[:b].set(c)
    tn = 1024
    out = pl.pallas_call(
        _ada_kernel,
        grid=(depth, n6 // tn),
        in_specs=[pl.BlockSpec((ADA_BATCH_PAD, d), lambda l, j: (0, 0)),
                  pl.BlockSpec((None, d, tn), lambda l, j: (l, 0, j)),
                  pl.BlockSpec((None, 1, tn), lambda l, j: (l, 0, j))],
        out_specs=pl.BlockSpec((None, ADA_BATCH_PAD, tn), lambda l, j: (l, 0, j)),
        out_shape=jax.ShapeDtypeStruct((depth, ADA_BATCH_PAD, n6), F32),
        compiler_params=_cparams(2, 40),
        name="ada",
    )(c_pad, w_ada, b_ada.reshape(depth, 1, n6))
    return out.reshape(depth * ADA_BATCH_PAD * 6, 1, d)


def _ln_modulate(z, lg, lb, sc_sh):
    mu = jnp.mean(z, axis=-1, keepdims=True)
    zc = z - mu
    var = jnp.mean(zc * zc, axis=-1, keepdims=True)
    xn = zc * lax.rsqrt(var + LN_EPS) * lg + lb
    if sc_sh is None:
        return xn, None
    sc, sh = sc_sh
    return xn, xn * (1.0 + sc) + sh


def _row_pitch(chunks):
    return chunks + 1


def _store_linear(dst_ref, val, chunks):
    rows = val.shape[0]
    pitch = _row_pitch(chunks)
    for c in range(chunks):
        dst_ref[pl.ds(c, rows, stride=pitch), :] = val[:, c * LANES:(c + 1) * LANES]
    dst_ref[pl.ds(chunks, rows, stride=pitch), :] = jnp.zeros((rows, LANES), val.dtype)


def _top2_route(logits, n_exp):
    lane = lax.broadcasted_iota(jnp.int32, logits.shape, 1)
    lg = jnp.where(lane < n_exp, logits, -jnp.inf)
    m1 = jnp.max(lg, axis=-1, keepdims=True)
    i1 = jnp.min(jnp.where(lg == m1, lane, LANES), axis=-1, keepdims=True)
    lg2 = jnp.where(lane == i1, -jnp.inf, lg)
    m2 = jnp.max(lg2, axis=-1, keepdims=True)
    i2 = jnp.min(jnp.where(lg2 == m2, lane, LANES), axis=-1, keepdims=True)
    e = jnp.exp(m2 - m1)
    w1 = 1.0 / (1.0 + e)
    w2 = e / (1.0 + e)
    idx = jnp.where(lane == 0, i1, jnp.where(lane == 1, i2, 0))
    wt = jnp.where(lane == 0, w1, jnp.where(lane == 1, w2, 0.0))
    return idx, wt


def _resid_ln_kernel(*refs, alpha, mode, n_exp, chunks, proj):
    if proj:
        x_ref, a_ref, w_ref, g_ref, lg_ref, lb_ref = refs[:6]
        rest = refs[6:]
    else:
        x_ref, y_ref, g_ref, lg_ref, lb_ref = refs[:5]
        rest = refs[5:]
    if mode == "last":
        (xo_ref,) = rest
        sc_sh = None
    elif mode == "next":
        sc_ref, sh_ref, xo_ref, uo_ref = rest
        sc_sh = (sc_ref[0], sh_ref[0])
    else:
        sc_ref, sh_ref, wr_ref, xo_ref, uo_ref, idx_ref, wt_ref = rest
        sc_sh = (sc_ref[0], sh_ref[0])
    tm = x_ref.shape[0]
    rc = PROJ_LN_ROWS if proj else tm
    if mode == "route":
        w_cat = jnp.concatenate(_split_bf16(wr_ref[...]), axis=1)

    def project(t):
        return jnp.dot(a_ref[t * rc:(t + 1) * rc, :], w_ref[...], preferred_element_type=F32)

    def finish(t, y):
        rows = slice(t * rc, (t + 1) * rc)
        z = alpha * x_ref[rows, :] + (1.0 + g_ref[0]) * y
        xn, u = _ln_modulate(z, lg_ref[0], lb_ref[0], sc_sh)
        xo_ref[rows, :] = xn
        if mode == "next":
            uo_ref[rows, :] = u.astype(uo_ref.dtype)
        elif mode == "route":
            pitch = _row_pitch(chunks)
            _store_linear(uo_ref.at[t * rc * pitch:(t + 1) * rc * pitch, :], u, chunks)
            uh, ul = _split_bf16(u)
            r = (jnp.dot(uh, w_cat, preferred_element_type=F32) + jnp.dot(ul, w_cat, preferred_element_type=F32))
            idx, wt = _top2_route(r[:, :LANES] + r[:, LANES:], n_exp)
            idx_ref[rows, :] = idx
            wt_ref[rows, :] = wt

    if proj:
        _staged(tm // rc, project, finish)
    else:
        finish(0, y_ref[...].astype(F32))


def _resid_ln(x, y, ada_rows, ln_g, ln_b, *, alpha, seq, layer, comp_g, nxt, w_router_pad=None, n_exp=0):
    t, d = x.shape
    proj = isinstance(y, tuple)
    tm = 512 if proj else 256
    tpb = seq // tm
    chunks = d // LANES
    mode = "last" if nxt is None else ("route" if w_router_pad is not None else "next")
    row = lambda l, comp: pl.BlockSpec((1, 1, d), lambda i: (_ada_row(l, i // tpb, comp), 0, 0))
    tile = pl.BlockSpec((tm, d), lambda i: (i, 0))
    lanes = pl.BlockSpec((tm, LANES), lambda i: (i, 0))
    lnp = pl.BlockSpec((1, 1, d), lambda i: (layer, 0, 0))
    if proj:
        a, w_stack, j = y
        k = a.shape[1]
        in_specs = [tile, pl.BlockSpec((tm, k), lambda i: (i, 0)), pl.BlockSpec((None, k, d), lambda i: (j, 0, 0))]
        args = [x, a, w_stack]
    else:
        in_specs = [tile, tile]
        args = [x, y]
    in_specs += [row(layer, comp_g), lnp, lnp]
    args += [ada_rows, ln_g, ln_b]
    out_specs = [tile]
    out_shape = [jax.ShapeDtypeStruct((t, d), F32)]
    if mode != "last":
        in_specs += [row(nxt[0], nxt[1]), row(nxt[0], nxt[2])]
        args += [ada_rows, ada_rows]
    if mode == "next":
        out_specs.append(tile)
        out_shape.append(jax.ShapeDtypeStruct((t, d), BF16))
    elif mode == "route":
        pitch = _row_pitch(chunks)
        in_specs.append(pl.BlockSpec((d, LANES), lambda i: (0, 0)))
        args.append(w_router_pad)
        out_specs += [pl.BlockSpec((tm * pitch, LANES), lambda i: (i, 0)), lanes, lanes]
        out_shape += [jax.ShapeDtypeStruct((t * pitch, LANES), F32),
                      jax.ShapeDtypeStruct((t, LANES), jnp.int32), jax.ShapeDtypeStruct((t, LANES), F32)]
    return pl.pallas_call(
        functools.partial(_resid_ln_kernel, alpha=alpha, mode=mode, n_exp=n_exp, chunks=chunks, proj=proj),
        grid=(t // tm,),
        in_specs=in_specs, out_specs=out_specs, out_shape=out_shape,
        compiler_params=_cparams(1, 56 if proj else 48),
        name=("proj_ln_" if proj else "resid_ln_") + mode,
    )(*args)


def _rms(x, g):
    ms = jnp.mean(x * x, axis=-1, keepdims=True)
    return x * lax.rsqrt(ms + RMS_EPS) * g


def _mla_down_kernel(*refs, ql, kvl, modulate):
    if modulate:
        u_ref, sc_ref, sh_ref, w_ref, qn_ref, kvn_ref, c_ref, s_ref, cq_ref, ckv_ref, kr_ref = refs
    else:
        u_ref, w_ref, qn_ref, kvn_ref, c_ref, s_ref, cq_ref, ckv_ref, kr_ref = refs
    rc = EPILOGUE_ROWS

    def matmul(t):
        u = u_ref[t * rc:(t + 1) * rc, :]
        if modulate:
            u = (u * (1.0 + sc_ref[0]) + sh_ref[0]).astype(BF16)
        return jnp.dot(u, w_ref[...], preferred_element_type=F32)

    def norm_rotate(t, acc):
        rows = slice(t * rc, (t + 1) * rc)
        cq_ref[rows, :] = _rms(acc[:, :ql], qn_ref[0]).astype(BF16)
        ckv_ref[rows, :] = _rms(acc[:, ql:ql + kvl], kvn_ref[0]).astype(BF16)
        xr = acc[:, ql + kvl:]
        kr_ref[rows, :] = (xr * c_ref[rows, :] + pltpu.roll(xr, LANES // 2, 1) * s_ref[rows, :]).astype(BF16)

    _staged(u_ref.shape[0] // rc, matmul, norm_rotate)


def _mla_down(u, w_perm, q_norm, kv_norm, layer, cos_t, sin_t, ql, kvl, mod=None):
    t, d = u.shape
    n = w_perm.shape[1]
    tm = 512
    nrm = lambda width: pl.BlockSpec((1, 1, width), lambda i: (layer, 0, 0))
    rows = lambda width: pl.BlockSpec((tm, width), lambda i: (i, 0))
    in_specs = [rows(d)]
    args = [u]
    if mod is not None:
        ada_rows, seq, ada_layer, comp_sc, comp_sh = mod
        tpb = seq // tm
        ada = lambda comp: pl.BlockSpec((1, 1, d), lambda i: (_ada_row(ada_layer, i // tpb, comp), 0, 0))
        in_specs += [ada(comp_sc), ada(comp_sh)]
        args += [ada_rows, ada_rows]
    in_specs += [pl.BlockSpec((d, n), lambda i: (0, 0)), nrm(ql), nrm(kvl), rows(LANES), rows(LANES)]
    args += [w_perm, q_norm, kv_norm, cos_t, sin_t]
    return pl.pallas_call(
        functools.partial(_mla_down_kernel, ql=ql, kvl=kvl, modulate=mod is not None),
        grid=(t // tm,),
        in_specs=in_specs,
        out_specs=[rows(ql), rows(kvl), rows(LANES)],
        out_shape=[jax.ShapeDtypeStruct((t, ql), BF16), jax.ShapeDtypeStruct((t, kvl), BF16),
                   jax.ShapeDtypeStruct((t, LANES), BF16)],
        compiler_params=_cparams(1, 48),
        name="mla_down",
    )(*args)


def _mla_up_kernel(cq_ref, ckv_ref, wq_ref, wkv_ref, c_ref, s_ref, q_ref, kv_ref, *, scale, heads):
    rc = EPILOGUE_ROWS

    def matmul(t):
        rows = slice(t * rc, (t + 1) * rc)
        return (jnp.dot(cq_ref[rows, :], wq_ref[...], preferred_element_type=F32),
                jnp.dot(ckv_ref[rows, :], wkv_ref[...], preferred_element_type=F32))

    def rotate(t, accs):
        acc, acc_kv = accs
        rows = slice(t * rc, (t + 1) * rc)
        kv_ref[rows, :] = acc_kv.astype(BF16)
        c = c_ref[rows, :]
        s = s_ref[rows, :]
        for h in range(heads):
            b0 = h * MLA_HEAD_PAD
            q_ref[rows, b0:b0 + LANES] = (acc[:, b0:b0 + LANES] * scale).astype(BF16)
            xr = acc[:, b0 + LANES:b0 + MLA_HEAD_PAD]
            q_ref[rows, b0 + LANES:b0 + MLA_HEAD_PAD] = (
                (xr * c + pltpu.roll(xr, LANES // 2, 1) * s) * scale).astype(BF16)

    _staged(cq_ref.shape[0] // rc, matmul, rotate)


def _mla_up(cq, ckv, wq_perm, wkv_stack, layer, cos_t, sin_t, scale):
    t, k = cq.shape
    n = wq_perm.shape[1]
    assert wkv_stack.shape[1:] == (ckv.shape[1], n)
    tm, tn = 1024, 1024
    rows = pl.BlockSpec((tm, LANES), lambda j, i: (i, 0))
    lat = pl.BlockSpec((tm, k), lambda j, i: (i, 0))
    out = pl.BlockSpec((tm, tn), lambda j, i: (i, j))
    return pl.pallas_call(
        functools.partial(_mla_up_kernel, scale=scale, heads=tn // MLA_HEAD_PAD),
        grid=(n // tn, t // tm),
        in_specs=[lat, lat, pl.BlockSpec((k, tn), lambda j, i: (0, j)),
                  pl.BlockSpec((None, k, tn), lambda j, i: (layer, 0, j)), rows, rows],
        out_specs=[out, out],
        out_shape=[jax.ShapeDtypeStruct((t, n), BF16), jax.ShapeDtypeStruct((t, n), BF16)],
        compiler_params=_cparams(2, 48),
        name="mla_up",
    )(cq, ckv, wq_perm, wkv_stack, cos_t, sin_t)


def _moba_qkv_kernel(u_ref, w_ref, tab_ref, o_ref, wb_ref, *, heads):
    @pl.when(pl.program_id(1) == 0)
    def _():
        wb_ref[...] = w_ref[...].astype(BF16)

    half = MOBA_ROT_DIM // 2
    rc = EPILOGUE_ROWS

    def matmul(t):
        return jnp.dot(u_ref[t * rc:(t + 1) * rc, :], wb_ref[...], preferred_element_type=F32)

    def rotate(t, acc):
        rows = slice(t * rc, (t + 1) * rc)
        c = tab_ref[rows, :LANES]
        s1 = tab_ref[rows, LANES:2 * LANES]
        s2 = tab_ref[rows, 2 * LANES:]
        for h in range(heads):
            x = acc[:, h * LANES:(h + 1) * LANES]
            r = x * c + pltpu.roll(x, half, 1) * s1 + pltpu.roll(x, LANES - half, 1) * s2
            o_ref[rows, h * LANES:(h + 1) * LANES] = r.astype(BF16)

    _staged(u_ref.shape[0] // rc, matmul, rotate)


def _moba_qkv(u, w_stack, layer, tables):
    t, k = u.shape
    n = w_stack.shape[2]
    tm, tn = 2048, 512
    tiles_per_sec = (n // 3) // tn
    return pl.pallas_call(
        functools.partial(_moba_qkv_kernel, heads=tn // LANES),
        grid=(n // tn, t // tm),
        in_specs=[pl.BlockSpec((tm, k), lambda j, i: (i, 0)),
                  pl.BlockSpec((None, k, tn), lambda j, i: (layer, 0, j)),
                  pl.BlockSpec((None, tm, 3 * LANES), lambda j, i: (j // tiles_per_sec, i, 0))],
        out_specs=pl.BlockSpec((tm, tn), lambda j, i: (i, j)),
        out_shape=jax.ShapeDtypeStruct((t, n), BF16),
        scratch_shapes=[pltpu.VMEM((k, tn), BF16)],
        compiler_params=_cparams(2, 48),
        name="moba_qkv",
    )(u, w_stack, tables)


_NT = (((1,), (1,)), ((), ()))


def _softmax_numer(s):
    m = jnp.max(s, axis=-1, keepdims=True)
    return jnp.exp2(s - m).astype(BF16)


def _fill_values_ones(vext_ref, v):
    vext_ref[:, :LANES] = v
    vext_ref[:, LANES:] = jnp.ones_like(v)


def _normalised_pv(p, vext):
    o = jnp.dot(p, vext, preferred_element_type=F32)
    return o[:, :LANES] / o[:, LANES:LANES + 1]


def _staged(n_tiles, *stages, reverse=False):
    vals = {}
    for t in range(n_tiles + len(stages) - 1):
        for k, stage in enumerate(stages):
            if 0 <= t - k < n_tiles:
                tile = n_tiles - 1 - (t - k) if reverse else t - k
                vals[tile] = stage(tile) if k == 0 else stage(tile, vals[tile])


def _mla_attn_kernel(q_ref, kv_ref, kr_ref, o_ref, kfull_ref, vext_ref, *, tq):
    seq = q_ref.shape[0]
    kfull_ref[:, :LANES] = kv_ref[:, :LANES]
    kfull_ref[:, LANES:] = kr_ref[...]
    _fill_values_ones(vext_ref, kv_ref[:, LANES:])
    row = lax.broadcasted_iota(jnp.int32, (tq, tq), 0)
    col = lax.broadcasted_iota(jnp.int32, (tq, tq), 1)
    causal = col <= row

    def scores(n):
        q = q_ref[n * tq:(n + 1) * tq, :]
        return lax.dot_general(q, kfull_ref[0:(n + 1) * tq, :], _NT, preferred_element_type=F32)

    def probs(n, s):
        diag = jnp.where(causal, s[:, n * tq:], -jnp.inf)
        return _softmax_numer(jnp.concatenate([s[:, :n * tq], diag], axis=1) if n else diag)

    def output(n, p):
        o_ref[n * tq:(n + 1) * tq, :] = _normalised_pv(p, vext_ref[0:(n + 1) * tq, :]).astype(o_ref.dtype)

    _staged(seq // tq, scores, probs, output, reverse=True)


def _mla_attention(q, kv, kr, batch, seq, heads):
    t = q.shape[0]
    return pl.pallas_call(
        functools.partial(_mla_attn_kernel, tq=256),
        grid=(batch, heads),
        in_specs=[pl.BlockSpec((seq, MLA_HEAD_PAD), lambda b, h: (b, h)),
                  pl.BlockSpec((seq, MLA_NOPE + MLA_V), lambda b, h: (b, h)),
                  pl.BlockSpec((seq, LANES), lambda b, h: (b, 0))],
        out_specs=pl.BlockSpec((seq, MLA_V), lambda b, h: (b, h)),
        out_shape=jax.ShapeDtypeStruct((t, heads * MLA_V), BF16),
        scratch_shapes=[pltpu.VMEM((seq, MLA_HEAD_PAD), BF16), pltpu.VMEM((seq, 2 * LANES), BF16)],
        compiler_params=_cparams(2, 48),
        name="mla_attn",
    )(q, kv, kr)


def _moba_attn_kernel(q_ref, k_ref, v_ref, o_ref, vext_ref, kext_ref, *, nb):
    blk = MOBA_BLOCK
    seq = k_ref.shape[0]
    _fill_values_ones(vext_ref, v_ref[...])
    kext_ref[:, :LANES] = k_ref[...]
    key_blk = lax.broadcasted_iota(jnp.int32, (seq, LANES), 0) // blk
    kext_ref[:, LANES:] = jnp.where(key_blk == lax.broadcasted_iota(jnp.int32, (seq, LANES), 1), 1.0, 0.0).astype(BF16)
    r = lax.broadcasted_iota(jnp.int32, (LANES, seq), 0)
    c = lax.broadcasted_iota(jnp.int32, (LANES, seq), 1)
    ind = jnp.where(c // blk == r, 1.0 / blk, 0.0).astype(BF16)
    km = jnp.dot(ind, k_ref[...], preferred_element_type=F32)
    kmh, kml = _split_bf16(km)
    row = lax.broadcasted_iota(jnp.int32, (blk, blk), 0)
    col = lax.broadcasted_iota(jnp.int32, (blk, blk), 1)
    causal = col <= row

    def gated_queries(n):
        q = q_ref[n * blk:(n + 1) * blk, :]
        if n <= MOBA_TOPK:
            return q
        gate_t = (lax.dot_general(kmh, q, _NT, preferred_element_type=F32)
                  + lax.dot_general(kml, q, _NT, preferred_element_type=F32))[:SUBLANES, :]
        blk_id = lax.broadcasted_iota(jnp.int32, gate_t.shape, 0)
        bias_t = jnp.where(blk_id == n, 0.0, MASKED_SCORE)
        for j in range(n):
            gj = gate_t[j:j + 1, :]
            beats = (blk_id < n) & ((gate_t > gj) | ((gate_t == gj) & (blk_id < j)))
            n_beats = jnp.sum(beats.astype(F32), axis=0, keepdims=True)
            bias_t = jnp.where((blk_id == j) & (n_beats < MOBA_TOPK), 0.0, bias_t)
        bias = jnp.concatenate([bias_t, jnp.zeros((LANES - SUBLANES, blk), F32)], axis=0).T
        return jnp.concatenate([q, bias.astype(BF16)], axis=1)

    def scores(n, q):
        keys = k_ref if q.shape[1] == LANES else kext_ref
        return lax.dot_general(q, keys[0:(n + 1) * blk, :], _NT, preferred_element_type=F32)

    def probs(n, s):
        diag = jnp.where(causal, s[:, n * blk:], -jnp.inf)
        return _softmax_numer(jnp.concatenate([s[:, :n * blk], diag], axis=1) if n else diag)

    def output(n, p):
        o_ref[n * blk:(n + 1) * blk, :] = _normalised_pv(p, vext_ref[0:(n + 1) * blk, :]).astype(o_ref.dtype)

    _staged(nb, gated_queries, scores, probs, output, reverse=True)


def _moba_attention(qkv, batch, seq, heads):
    t = qkv.shape[0]
    d = MOBA_HEAD_DIM
    return pl.pallas_call(
        functools.partial(_moba_attn_kernel, nb=seq // MOBA_BLOCK),
        grid=(batch, heads),
        in_specs=[pl.BlockSpec((seq, d), lambda b, h: (b, h)),
                  pl.BlockSpec((seq, d), lambda b, h: (b, heads + h)),
                  pl.BlockSpec((seq, d), lambda b, h: (b, 2 * heads + h))],
        out_specs=pl.BlockSpec((seq, d), lambda b, h: (b, h)),
        out_shape=jax.ShapeDtypeStruct((t, heads * d), BF16),
        scratch_shapes=[pltpu.VMEM((seq, 2 * LANES), BF16), pltpu.VMEM((seq, 2 * LANES), BF16)],
        compiler_params=_cparams(2, 48),
        name="moba_attn",
    )(qkv, qkv, qkv)


def _swiglu_partial(x, wg, wu, wd):
    g = jnp.dot(x, wg, preferred_element_type=F32)
    u = jnp.dot(x, wu, preferred_element_type=F32)
    a = (g * jax.nn.sigmoid(g) * u).astype(BF16)
    return jnp.dot(a, wd, preferred_element_type=F32)


def _ffn_kernel(u_ref, wg_ref, wu_ref, wd_ref, o_ref):
    f = pl.program_id(1)
    @pl.when(f == 0)
    def _():
        o_ref[...] = jnp.zeros_like(o_ref)

    o_ref[...] += _swiglu_partial(u_ref[...], wg_ref[...].astype(BF16), wu_ref[...].astype(BF16),
                                  wd_ref[...].astype(BF16))


def _dense_ffn(u, w_gate_up, w_down, layer):
    t, d = u.shape
    dff = w_down.shape[1]
    tm, tf = 1024, 256
    nf = dff // tf
    return pl.pallas_call(
        _ffn_kernel,
        grid=(t // tm, nf),
        in_specs=[pl.BlockSpec((tm, d), lambda i, f: (i, 0)),
                  pl.BlockSpec((None, d, tf), lambda i, f: (layer, 0, f)),
                  pl.BlockSpec((None, d, tf), lambda i, f: (layer, 0, nf + f)),
                  pl.BlockSpec((None, tf, d), lambda i, f: (layer, f, 0))],
        out_specs=pl.BlockSpec((tm, d), lambda i, f: (i, 0)),
        out_shape=jax.ShapeDtypeStruct((t, d), F32),
        compiler_params=_cparams(2, 56),
        name="dense_ffn",
    )(u, w_gate_up, w_gate_up, w_down)


def _route_metadata(idx2, n_exp):
    rb, st = MOE_ROW_BLOCK, MOE_SUPER_BLOCKS
    t = idx2.shape[0]
    a = t * TOP_K
    n_over = (a // rb + n_exp - 2) // st
    e_flat = idx2.reshape(a)
    onehot = (e_flat[:, None] == jnp.arange(n_exp, dtype=jnp.int32)[None, :]).astype(jnp.int32)
    csum = jnp.cumsum(onehot, axis=0)
    rank = jnp.sum((csum - onehot) * onehot, axis=1)
    counts = csum[-1]
    nsub = (counts + rb - 1) // rb
    sub_start = jnp.cumsum(nsub) - nsub
    dest = jnp.sum(onehot * (sub_start * rb)[None, :], axis=1) + rank
    p_rows = (a // rb + n_exp) * rb
    token_flat = jnp.arange(a, dtype=jnp.int32) // TOP_K
    row_token = jnp.zeros((p_rows,), jnp.int32).at[dest].set(token_flat)
    i32 = lambda *xs: tuple(x.astype(jnp.int32) for x in xs)
    experts = jnp.arange(n_exp, dtype=jnp.int32)
    first_row0 = jnp.concatenate([sub_start * rb, jnp.sum(nsub, keepdims=True) * rb])
    first = i32(experts, first_row0, jnp.minimum(nsub, st))
    n_it = jnp.maximum((nsub + st - 1) // st - 1, 0)
    it_end = jnp.cumsum(n_it)
    it_start = it_end - n_it
    w = jnp.arange(n_over, dtype=jnp.int32)
    e_w = jnp.sum((it_end[None, :] <= w[:, None]).astype(jnp.int32), axis=1)
    active = e_w < n_exp
    e_c = jnp.minimum(e_w, n_exp - 1)
    local = w - it_start[e_c] + 1
    over_nsub = jnp.where(active, jnp.clip(nsub[e_c] - local * st, 0, st), 0)
    over_row0 = jnp.where(active, (sub_start[e_c] + local * st) * rb, 0)
    e_last = jnp.max(jnp.where(n_it > 0, experts, 0))
    overflow = i32(jnp.where(active, e_c, e_last), over_row0, over_nsub)
    return dest.astype(jnp.int32), row_token, first, overflow, it_end[-1] > 0


ROW_DMA_UNROLL = 8


def _row_copy(src_hbm, src_row, dst, dst_row, sem, chunks):
    pitch = _row_pitch(chunks)
    return pltpu.make_async_copy(src_hbm.at[pl.ds(src_row * pitch, chunks), :],
                                 dst.at[pl.ds(dst_row * pitch, chunks), :], sem)


def _rows_wait(src_hbm, dst, sem, rows, chunks):
    n = rows * chunks
    pltpu.make_async_copy(src_hbm.at[pl.ds(0, n), :], dst.at[pl.ds(0, n), :], sem).wait()


def _dispatch_kernel(tok_ref, u_hbm, o_ref, stage_ref, sem, *, rb, chunks):
    i = pl.program_id(0)
    pitch = _row_pitch(chunks)

    def issue(step, slot):
        def body(g, c):
            for k in range(ROW_DMA_UNROLL):
                r = g * ROW_DMA_UNROLL + k
                _row_copy(u_hbm, tok_ref[step * rb + r], stage_ref.at[slot], r, sem.at[slot],
                          chunks).start(priority=k % 2)
            return c
        lax.fori_loop(0, rb // ROW_DMA_UNROLL, body, 0)

    @pl.when(i == 0)
    def _():
        issue(0, 0)

    @pl.when(i + 1 < pl.num_programs(0))
    def _():
        issue(i + 1, (i + 1) % 2)

    slot = i % 2
    _rows_wait(u_hbm, stage_ref.at[slot], sem.at[slot], rb, chunks)
    for c in range(chunks):
        o_ref[:, c * LANES:(c + 1) * LANES] = stage_ref[slot, pl.ds(c, rb, stride=pitch), :].astype(o_ref.dtype)


def _dispatch(u_lin, row_token, d):
    p_rows = row_token.shape[0]
    rb = 2 * MOE_ROW_BLOCK if p_rows % (2 * MOE_ROW_BLOCK) == 0 else MOE_ROW_BLOCK
    chunks = d // LANES
    return pl.pallas_call(
        functools.partial(_dispatch_kernel, rb=rb, chunks=chunks),
        grid_spec=pltpu.PrefetchScalarGridSpec(
            num_scalar_prefetch=1,
            grid=(p_rows // rb,),
            in_specs=[pl.BlockSpec(memory_space=pl.ANY)],
            out_specs=pl.BlockSpec((rb, d), lambda i, tok: (i, 0)),
            scratch_shapes=[pltpu.VMEM((2, rb * _row_pitch(chunks), LANES), F32),
                            pltpu.SemaphoreType.DMA((2,))]),
        out_shape=jax.ShapeDtypeStruct((p_rows, d), BF16),
        compiler_params=_cparams(1, 40),
        name="moe_dispatch",
    )(row_token, u_lin)


def _moe_ffn_kernel(*refs, rb, nf, n_items, chunks, fill_tail):
    e_ref, row0_ref, nsub_ref, xs_hbm, wg_ref, wu_ref, wd_ref = refs[:7]
    ys_hbm, x_ref, acc_ref, wgb_ref, wub_ref, wdb_ref, stage_ref, sem = refs[7 if fill_tail else 8:]
    w = pl.program_id(0)
    f = pl.program_id(1)
    nsub = nsub_ref[w]
    row0 = row0_ref[w]

    def for_range(n, fn):
        def body(r, c):
            fn(r)
            return c
        lax.fori_loop(0, n, body, 0)

    def load(r):
        return pltpu.make_async_copy(xs_hbm.at[pl.ds(pl.multiple_of(row0 + r * rb, rb), rb), :],
                                     x_ref.at[pl.ds(pl.multiple_of(r * rb, rb), rb), :], sem.at[0])

    pitch = _row_pitch(chunks)

    def store(row, slot):
        return pltpu.make_async_copy(
            stage_ref.at[slot], ys_hbm.at[pl.ds(pl.multiple_of(row * pitch, rb * pitch), rb * pitch), :],
            sem.at[1 + slot])

    @pl.when(f == 0)
    def _():
        for_range(nsub, lambda r: load(r).start())
        acc_ref[...] = jnp.zeros_like(acc_ref)
        for_range(nsub, lambda r: load(r).wait())

    @pl.when(nsub > 0)
    def _():
        wgb_ref[...] = wg_ref[...].astype(BF16)
        wub_ref[...] = wu_ref[...].astype(BF16)
        wdb_ref[...] = wd_ref[...].astype(BF16)

        def chunk(start, size):
            rows = pl.ds(pl.multiple_of(start, rb), size)
            acc_ref[rows, :] += _swiglu_partial(x_ref[rows, :], wgb_ref[...], wub_ref[...], wdb_ref[...])

        for_range(nsub // 2, lambda p: chunk(p * (2 * rb), 2 * rb))

        @pl.when(nsub % 2 == 1)
        def _():
            chunk((nsub - 1) * rb, rb)

    @pl.when(f == nf - 1)
    def _():
        def emit(r):
            slot = r % 2

            @pl.when(r >= 2)
            def _():
                store(row0 + (r - 2) * rb, slot).wait()

            _store_linear(stage_ref.at[slot], acc_ref[pl.ds(pl.multiple_of(r * rb, rb), rb), :], chunks)
            store(row0 + r * rb, slot).start()

        for_range(nsub, emit)

        @pl.when(nsub >= 2)
        def _():
            store(row0, nsub % 2).wait()

        @pl.when(nsub >= 1)
        def _():
            store(row0, (nsub - 1) % 2).wait()

    if fill_tail:
        @pl.when((f == nf - 1) & (w == n_items - 1))
        def _():
            used = row0_ref[n_items]
            n_tail = (ys_hbm.shape[0] // pitch - used) // rb
            stage_ref[0] = jnp.zeros(stage_ref.shape[1:], F32)
            for_range(n_tail, lambda r: store(used + r * rb, 0).start())
            for_range(n_tail, lambda r: store(used + r * rb, 0).wait())


def _moe_ffn(xs, w_gate_up, w_down, layer, items, ys_in=None):
    item_e, item_row0, item_nsub = items
    extra = () if ys_in is None else (ys_in,)
    hbm = pl.BlockSpec(memory_space=pl.ANY)
    p_rows, d = xs.shape
    dff = w_down.shape[2]
    rb, st = MOE_ROW_BLOCK, MOE_SUPER_BLOCKS
    n_items = item_e.shape[0]
    chunks = d // LANES
    tf = 256
    nf = dff // tf

    def f_eff(f, nsub, w):
        return jnp.where(nsub[w] > 0, f, nf - 1)

    return pl.pallas_call(
        functools.partial(_moe_ffn_kernel, rb=rb, nf=nf, n_items=n_items, chunks=chunks, fill_tail=ys_in is None),
        grid_spec=pltpu.PrefetchScalarGridSpec(
            num_scalar_prefetch=3,
            grid=(n_items, nf),
            in_specs=[hbm,
                      pl.BlockSpec((None, None, d, tf), lambda w, f, e, r0, ns: (layer, e[w], 0, f_eff(f, ns, w))),
                      pl.BlockSpec((None, None, d, tf),
                                   lambda w, f, e, r0, ns: (layer, e[w], 0, nf + f_eff(f, ns, w))),
                      pl.BlockSpec((None, None, tf, d), lambda w, f, e, r0, ns: (layer, e[w], f_eff(f, ns, w), 0))]
            + [hbm] * len(extra),
            out_specs=hbm,
            scratch_shapes=[pltpu.VMEM((st * rb, d), BF16), pltpu.VMEM((st * rb, d), F32),
                            pltpu.VMEM((d, tf), BF16), pltpu.VMEM((d, tf), BF16), pltpu.VMEM((tf, d), BF16),
                            pltpu.VMEM((2, rb * _row_pitch(chunks), LANES), F32), pltpu.SemaphoreType.DMA((3,))]),
        out_shape=jax.ShapeDtypeStruct((p_rows * _row_pitch(chunks), LANES), F32),
        input_output_aliases={} if ys_in is None else {7: 0},
        compiler_params=_cparams(2, 58),
        name="moe_ffn",
    )(item_e, item_row0, item_nsub, xs, w_gate_up, w_gate_up, w_down, *extra)


def _moe_combine_ln_kernel(*refs, alpha, emit_u, tm, chunks):
    if emit_u:
        (pos_ref, ys_hbm, wt_ref, x_ref, g_ref, lg_ref, lb_ref, sc_ref, sh_ref, xo_ref, uo_ref,
         stage_ref, y_ref, sem) = refs
        sc_sh = (sc_ref[0], sh_ref[0])
    else:
        pos_ref, ys_hbm, wt_ref, x_ref, g_ref, lg_ref, lb_ref, xo_ref, stage_ref, y_ref, sem = refs
        sc_sh = None
    i = pl.program_id(0)
    pitch = _row_pitch(chunks)

    def issue(step, slot):
        def body(g, c):
            for j in range(ROW_DMA_UNROLL // TOP_K):
                t = g * (ROW_DMA_UNROLL // TOP_K) + j
                for k in range(TOP_K):
                    _row_copy(ys_hbm, pos_ref[(step * tm + t) * TOP_K + k], stage_ref.at[slot, k], t,
                              sem.at[slot, k], chunks).start(priority=k % 2)
            return c
        lax.fori_loop(0, tm * TOP_K // ROW_DMA_UNROLL, body, 0)

    @pl.when(i == 0)
    def _():
        issue(0, 0)

    @pl.when(i + 1 < pl.num_programs(0))
    def _():
        issue(i + 1, (i + 1) % 2)

    slot = i % 2
    for k in range(TOP_K):
        _rows_wait(ys_hbm, stage_ref.at[slot, k], sem.at[slot, k], tm, chunks)
    w1 = wt_ref[:, 0:1]
    w2 = wt_ref[:, 1:2]
    for c in range(chunks):
        y_ref[:, c * LANES:(c + 1) * LANES] = (stage_ref[slot, 0, pl.ds(c, tm, stride=pitch), :] * w1
                                               + stage_ref[slot, 1, pl.ds(c, tm, stride=pitch), :] * w2)
    z = alpha * x_ref[...] + (1.0 + g_ref[0]) * y_ref[...]
    xn, u = _ln_modulate(z, lg_ref[0], lb_ref[0], sc_sh)
    xo_ref[...] = xn
    if emit_u:
        uo_ref[...] = u.astype(uo_ref.dtype)


def _moe_combine_ln(ys_lin, pos, wt, x, ada_rows, ln_g, ln_b, *, alpha, seq, layer, comp_g, nxt):
    t, d = x.shape
    tm = 256
    tpb = seq // tm
    chunks = d // LANES
    row = lambda l, comp: pl.BlockSpec((1, 1, d), lambda i, p: (_ada_row(l, i // tpb, comp), 0, 0))
    tile = pl.BlockSpec((tm, d), lambda i, p: (i, 0))
    lnp = pl.BlockSpec((1, 1, d), lambda i, p: (layer, 0, 0))
    in_specs = [pl.BlockSpec(memory_space=pl.ANY), pl.BlockSpec((tm, LANES), lambda i, p: (i, 0)), tile,
                row(layer, comp_g), lnp, lnp]
    args = [ys_lin, wt, x, ada_rows, ln_g, ln_b]
    out_specs = [tile]
    out_shape = [jax.ShapeDtypeStruct((t, d), F32)]
    if nxt is not None:
        in_specs += [row(nxt[0], nxt[1]), row(nxt[0], nxt[2])]
        args += [ada_rows, ada_rows]
        out_specs.append(tile)
        out_shape.append(jax.ShapeDtypeStruct((t, d), BF16))
    outs = pl.pallas_call(
        functools.partial(_moe_combine_ln_kernel, alpha=alpha, emit_u=nxt is not None, tm=tm, chunks=chunks),
        grid_spec=pltpu.PrefetchScalarGridSpec(
            num_scalar_prefetch=1,
            grid=(t // tm,),
            in_specs=in_specs, out_specs=out_specs,
            scratch_shapes=[pltpu.VMEM((2, TOP_K, tm * _row_pitch(chunks), LANES), F32), pltpu.VMEM((tm, d), F32),
                            pltpu.SemaphoreType.DMA((2, TOP_K))]),
        out_shape=out_shape,
        compiler_params=_cparams(1, 48),
        name="moe_combine_ln",
    )(pos, *args)
    return (outs[0], outs[1]) if nxt is not None else (outs[0], None)


def _rope_cos_sin(positions, dim):
    inv_freq = ROPE_THETA ** (-jnp.arange(0, dim, 2, dtype=F32) / dim)
    ang = positions.astype(F32).reshape(-1)[:, None] * inv_freq
    return jnp.cos(ang), jnp.sin(ang)


def _mla_rope_tables(positions):
    cos, sin = _rope_cos_sin(positions, MLA_ROPE)
    z = jnp.zeros_like(cos)
    return jnp.concatenate([cos, z, cos, z], axis=1), jnp.concatenate([-sin, z, sin, z], axis=1)


def _moba_rope_tables(positions, scale):
    cos, sin = _rope_cos_sin(positions, MOBA_ROT_DIM)
    t, half = cos.shape
    rest = LANES - 2 * half
    c = jnp.concatenate([cos, cos, jnp.ones((t, rest), F32)], axis=1)
    s1 = jnp.concatenate([jnp.zeros((t, half), F32), sin, jnp.zeros((t, rest), F32)], axis=1)
    s2 = jnp.concatenate([-sin, jnp.zeros((t, half + rest), F32)], axis=1)
    rot = jnp.concatenate([c, s1, s2], axis=1)
    ident = jnp.concatenate([jnp.ones((t, LANES), F32), jnp.zeros((t, 2 * LANES), F32)], axis=1)
    return jnp.stack([rot * scale, rot, ident])


def _spread_rope_cols(w_rope):
    half = MLA_ROPE // 2
    z = jnp.zeros(w_rope.shape[:-1] + (LANES // 2 - half,), w_rope.dtype)
    return jnp.concatenate([w_rope[..., :half], z, w_rope[..., half:], z], axis=-1)


def _mla_weights(w_down, w_uq, ql, kvl):
    k = w_uq.shape[0]
    heads = w_uq.shape[1] // (MLA_NOPE + MLA_ROPE)
    wd = jnp.concatenate([w_down[:, :ql + kvl], _spread_rope_cols(w_down[:, ql + kvl:])], axis=1).astype(BF16)
    wq = w_uq.reshape(k, heads, MLA_NOPE + MLA_ROPE)
    wq = jnp.concatenate([wq[..., :MLA_NOPE], _spread_rope_cols(wq[..., MLA_NOPE:])], axis=-1)
    return wd, wq.reshape(k, heads * MLA_HEAD_PAD).astype(BF16), heads


def kernel(x, c, positions, w_ada, b_ada, ln_mix_g, ln_mix_b, ln_ffn_g, ln_ffn_b, mla_w_down, mla_q_norm,
           mla_kv_norm, mla_w_uq, mla_w_ukv, mla_w_o, moba_w_qkv, moba_w_o, ffn_w_gate_up, ffn_w_down,
           moe_w_router, moe_w_gate_up, moe_w_down):
    batch, seq, d = x.shape
    depth = w_ada.shape[0]
    t = batch * seq
    alpha = (2.0 * depth) ** 0.25
    ql = mla_q_norm.shape[1]
    kvl = mla_kv_norm.shape[1]
    n_exp = moe_w_router.shape[2]

    ada_rows = _ada_all(c, w_ada, b_ada)
    cos_mla, sin_mla = _mla_rope_tables(positions)
    moba_tables = _moba_rope_tables(positions, MOBA_HEAD_DIM ** -0.5 * LOG2_E)
    mla_w_ukv_b = mla_w_ukv.astype(BF16)
    mla_w_o_b = mla_w_o.astype(BF16)
    moba_w_o_b = moba_w_o.astype(BF16)
    ln3 = lambda p: p.reshape(depth, 1, d)
    ln_mix_g, ln_mix_b, ln_ffn_g, ln_ffn_b = ln3(ln_mix_g), ln3(ln_mix_b), ln3(ln_ffn_g), ln3(ln_ffn_b)
    q_norm3 = mla_q_norm.reshape(-1, 1, ql)
    kv_norm3 = mla_kv_norm.reshape(-1, 1, kvl)

    xf = x.reshape(t, d)
    u = None
    for l in range(depth):
        j = l // 2
        moe_layer = l % 2 == 1
        if l % 2 == 0:
            wd_p, wq_p, heads = _mla_weights(mla_w_down[j], mla_w_uq[j], ql, kvl)
            first = (xf, (ada_rows, seq, l, 1, 0)) if u is None else (u, None)
            cq, ckv, kr = _mla_down(first[0], wd_p, q_norm3, kv_norm3, j, cos_mla, sin_mla, ql, kvl, mod=first[1])
            q, kv = _mla_up(cq, ckv, wq_p, mla_w_ukv_b, j, cos_mla, sin_mla,
                            (MLA_NOPE + MLA_ROPE) ** -0.5 * LOG2_E)
            o = _mla_attention(q, kv, kr, batch, seq, heads)
            y = (o, mla_w_o_b, j)
        else:
            heads = moba_w_qkv.shape[2] // (3 * MOBA_HEAD_DIM)
            qkv = _moba_qkv(u, moba_w_qkv, j, moba_tables)
            o = _moba_attention(qkv, batch, seq, heads)
            y = (o, moba_w_o_b, j)
        ln_args = dict(alpha=alpha, seq=seq, layer=l)
        nxt = (l + 1, 1, 0) if l + 1 < depth else None
        if not moe_layer:
            xf, u = _resid_ln(xf, y, ada_rows, ln_mix_g, ln_mix_b, comp_g=2, nxt=(l, 4, 3), **ln_args)
            y = _dense_ffn(u, ffn_w_gate_up, ffn_w_down, j)
            outs = _resid_ln(xf, y, ada_rows, ln_ffn_g, ln_ffn_b, comp_g=5, nxt=nxt, **ln_args)
            xf, u = outs[0], (outs[1] if nxt is not None else None)
        else:
            w_router_pad = jnp.zeros((d, LANES), F32).at[:, :n_exp].set(moe_w_router[j])
            xf, u_lin, idx, wt = _resid_ln(xf, y, ada_rows, ln_mix_g, ln_mix_b, comp_g=2, nxt=(l, 4, 3),
                                           w_router_pad=w_router_pad, n_exp=n_exp, **ln_args)
            pos, row_token, first, overflow, any_overflow = _route_metadata(idx[:, :TOP_K], n_exp)
            xs = _dispatch(u_lin, row_token, d)
            ys_lin = _moe_ffn(xs, moe_w_gate_up, moe_w_down, j, first)
            ys_lin = lax.cond(any_overflow,
                              lambda ys: _moe_ffn(xs, moe_w_gate_up, moe_w_down, j, overflow, ys_in=ys),
                              lambda ys: ys, ys_lin)
            xf, u = _moe_combine_ln(ys_lin, pos, wt, xf, ada_rows, ln_ffn_g, ln_ffn_b, comp_g=5, nxt=nxt, **ln_args)
    return xf.reshape(batch, seq, d)
```

```python
import functools

import jax
import jax.numpy as jnp
from jax import lax
from jax.experimental import pallas as pl
from jax.experimental.pallas import tpu as pltpu

F32 = jnp.float32
BF16 = jnp.bfloat16

ROPE_THETA = 500000.0
LN_EPS = 1e-5
RMS_EPS = 1e-6
MLA_NOPE = 128
MLA_ROPE = 64
MLA_V = 128
MOBA_HEAD_DIM = 128
MOBA_ROT_DIM = 32
MOBA_BLOCK = 256
MOBA_TOPK = 3
TOP_K = 2
LOG2_E = 1.4426950408889634
MASKED_SCORE = -1e30

LANES = 128
SUBLANES = 8
MLA_HEAD_PAD = 2 * LANES

ADA_BATCH_PAD = SUBLANES
MOE_ROW_BLOCK = 256
MOE_SUPER_BLOCKS = 10
EPILOGUE_ROWS = 256
PROJ_LN_ROWS = 128


def _cparams(n_axes, vmem_mb):
    return pltpu.CompilerParams(dimension_semantics=("arbitrary",) * n_axes,
                                vmem_limit_bytes=vmem_mb * 1024 * 1024)


def _split_bf16(x):
    hi = x.astype(BF16)
    return hi, (x - hi.astype(F32)).astype(BF16)


def _ada_row(layer, batch, comp):
    return (layer * ADA_BATCH_PAD + batch) * 6 + comp


def _ada_kernel(c_ref, w_ref, b_ref, o_ref):
    c = c_ref[...]
    ca = (c * jax.nn.sigmoid(c)).astype(BF16)
    o_ref[...] = jnp.dot(ca, w_ref[...].astype(BF16), preferred_element_type=F32) + b_ref[...]


def _ada_all(c, w_ada, b_ada):
    depth, d, n6 = w_ada.shape
    b = c.shape[0]
    c_pad = jnp.zeros((ADA_BATCH_PAD, d), F32).at[:b].set(c)
    tn = 1024
    out = pl.pallas_call(
        _ada_kernel,
        grid=(depth, n6 // tn),
        in_specs=[pl.BlockSpec((ADA_BATCH_PAD, d), lambda l, j: (0, 0)),
                  pl.BlockSpec((None, d, tn), lambda l, j: (l, 0, j)),
                  pl.BlockSpec((None, 1, tn), lambda l, j: (l, 0, j))],
        out_specs=pl.BlockSpec((None, ADA_BATCH_PAD, tn), lambda l, j: (l, 0, j)),
        out_shape=jax.ShapeDtypeStruct((depth, ADA_BATCH_PAD, n6), F32),
        compiler_params=_cparams(2, 40),
        name="ada",
    )(c_pad, w_ada, b_ada.reshape(depth, 1, n6))
    return out.reshape(depth * ADA_BATCH_PAD * 6, 1, d)


def _ln_modulate(z, lg, lb, sc_sh):
    mu = jnp.mean(z, axis=-1, keepdims=True)
    zc = z - mu
    var = jnp.mean(zc * zc, axis=-1, keepdims=True)
    xn = zc * lax.rsqrt(var + LN_EPS) * lg + lb
    if sc_sh is None:
        return xn, None
    sc, sh = sc_sh
    return xn, xn * (1.0 + sc) + sh


def _row_pitch(chunks):
    return chunks + 1


def _store_linear(dst_ref, val, chunks):
    rows = val.shape[0]
    pitch = _row_pitch(chunks)
    for c in range(chunks):
        dst_ref[pl.ds(c, rows, stride=pitch), :] = val[:, c * LANES:(c + 1) * LANES]
    dst_ref[pl.ds(chunks, rows, stride=pitch), :] = jnp.zeros((rows, LANES), val.dtype)


def _top2_route(logits, n_exp):
    lane = lax.broadcasted_iota(jnp.int32, logits.shape, 1)
    lg = jnp.where(lane < n_exp, logits, -jnp.inf)
    m1 = jnp.max(lg, axis=-1, keepdims=True)
    i1 = jnp.min(jnp.where(lg == m1, lane, LANES), axis=-1, keepdims=True)
    lg2 = jnp.where(lane == i1, -jnp.inf, lg)
    m2 = jnp.max(lg2, axis=-1, keepdims=True)
    i2 = jnp.min(jnp.where(lg2 == m2, lane, LANES), axis=-1, keepdims=True)
    e = jnp.exp(m2 - m1)
    w1 = 1.0 / (1.0 + e)
    w2 = e / (1.0 + e)
    idx = jnp.where(lane == 0, i1, jnp.where(lane == 1, i2, 0))
    wt = jnp.where(lane == 0, w1, jnp.where(lane == 1, w2, 0.0))
    return idx, wt


def _resid_ln_kernel(*refs, alpha, mode, n_exp, chunks, proj):
    if proj:
        x_ref, a_ref, w_ref, g_ref, lg_ref, lb_ref = refs[:6]
        rest = refs[6:]
    else:
        x_ref, y_ref, g_ref, lg_ref, lb_ref = refs[:5]
        rest = refs[5:]
    if mode == "last":
        (xo_ref,) = rest
        sc_sh = None
    elif mode == "next":
        sc_ref, sh_ref, xo_ref, uo_ref = rest
        sc_sh = (sc_ref[0], sh_ref[0])
    else:
        sc_ref, sh_ref, wr_ref, xo_ref, uo_ref, idx_ref, wt_ref = rest
        sc_sh = (sc_ref[0], sh_ref[0])
    tm = x_ref.shape[0]
    rc = PROJ_LN_ROWS if proj else tm
    if mode == "route":
        w_cat = jnp.concatenate(_split_bf16(wr_ref[...]), axis=1)

    def project(t):
        return jnp.dot(a_ref[t * rc:(t + 1) * rc, :], w_ref[...], preferred_element_type=F32)

    def finish(t, y):
        rows = slice(t * rc, (t + 1) * rc)
        z = alpha * x_ref[rows, :] + (1.0 + g_ref[0]) * y
        xn, u = _ln_modulate(z, lg_ref[0], lb_ref[0], sc_sh)
        xo_ref[rows, :] = xn
        if mode == "next":
            uo_ref[rows, :] = u.astype(uo_ref.dtype)
        elif mode == "route":
            pitch = _row_pitch(chunks)
            _store_linear(uo_ref.at[t * rc * pitch:(t + 1) * rc * pitch, :], u, chunks)
            uh, ul = _split_bf16(u)
            r = (jnp.dot(uh, w_cat, preferred_element_type=F32) + jnp.dot(ul, w_cat, preferred_element_type=F32))
            idx, wt = _top2_route(r[:, :LANES] + r[:, LANES:], n_exp)
            idx_ref[rows, :] = idx
            wt_ref[rows, :] = wt

    if proj:
        _staged(tm // rc, project, finish)
    else:
        finish(0, y_ref[...].astype(F32))


def _resid_ln(x, y, ada_rows, ln_g, ln_b, *, alpha, seq, layer, comp_g, nxt, w_router_pad=None, n_exp=0):
    t, d = x.shape
    proj = isinstance(y, tuple)
    tm = 512 if proj else 256
    tpb = seq // tm
    chunks = d // LANES
    mode = "last" if nxt is None else ("route" if w_router_pad is not None else "next")
    row = lambda l, comp: pl.BlockSpec((1, 1, d), lambda i: (_ada_row(l, i // tpb, comp), 0, 0))
    tile = pl.BlockSpec((tm, d), lambda i: (i, 0))
    lanes = pl.BlockSpec((tm, LANES), lambda i: (i, 0))
    lnp = pl.BlockSpec((1, 1, d), lambda i: (layer, 0, 0))
    if proj:
        a, w_stack, j = y
        k = a.shape[1]
        in_specs = [tile, pl.BlockSpec((tm, k), lambda i: (i, 0)), pl.BlockSpec((None, k, d), lambda i: (j, 0, 0))]
        args = [x, a, w_stack]
    else:
        in_specs = [tile, tile]
        args = [x, y]
    in_specs += [row(layer, comp_g), lnp, lnp]
    args += [ada_rows, ln_g, ln_b]
    out_specs = [tile]
    out_shape = [jax.ShapeDtypeStruct((t, d), F32)]
    if mode != "last":
        in_specs += [row(nxt[0], nxt[1]), row(nxt[0], nxt[2])]
        args += [ada_rows, ada_rows]
    if mode == "next":
        out_specs.append(tile)
        out_shape.append(jax.ShapeDtypeStruct((t, d), BF16))
    elif mode == "route":
        pitch = _row_pitch(chunks)
        in_specs.append(pl.BlockSpec((d, LANES), lambda i: (0, 0)))
        args.append(w_router_pad)
        out_specs += [pl.BlockSpec((tm * pitch, LANES), lambda i: (i, 0)), lanes, lanes]
        out_shape += [jax.ShapeDtypeStruct((t * pitch, LANES), F32),
                      jax.ShapeDtypeStruct((t, LANES), jnp.int32), jax.ShapeDtypeStruct((t, LANES), F32)]
    return pl.pallas_call(
        functools.partial(_resid_ln_kernel, alpha=alpha, mode=mode, n_exp=n_exp, chunks=chunks, proj=proj),
        grid=(t // tm,),
        in_specs=in_specs, out_specs=out_specs, out_shape=out_shape,
        compiler_params=_cparams(1, 56 if proj else 48),
        name=("proj_ln_" if proj else "resid_ln_") + mode,
    )(*args)


def _rms(x, g):
    ms = jnp.mean(x * x, axis=-1, keepdims=True)
    return x * lax.rsqrt(ms + RMS_EPS) * g


def _mla_down_kernel(*refs, ql, kvl, modulate):
    if modulate:
        u_ref, sc_ref, sh_ref, w_ref, qn_ref, kvn_ref, c_ref, s_ref, cq_ref, ckv_ref, kr_ref = refs
    else:
        u_ref, w_ref, qn_ref, kvn_ref, c_ref, s_ref, cq_ref, ckv_ref, kr_ref = refs
    rc = EPILOGUE_ROWS

    def matmul(t):
        u = u_ref[t * rc:(t + 1) * rc, :]
        if modulate:
            u = (u * (1.0 + sc_ref[0]) + sh_ref[0]).astype(BF16)
        return jnp.dot(u, w_ref[...], preferred_element_type=F32)

    def norm_rotate(t, acc):
        rows = slice(t * rc, (t + 1) * rc)
        cq_ref[rows, :] = _rms(acc[:, :ql], qn_ref[0]).astype(BF16)
        ckv_ref[rows, :] = _rms(acc[:, ql:ql + kvl], kvn_ref[0]).astype(BF16)
        xr = acc[:, ql + kvl:]
        kr_ref[rows, :] = (xr * c_ref[rows, :] + pltpu.roll(xr, LANES // 2, 1) * s_ref[rows, :]).astype(BF16)

    _staged(u_ref.shape[0] // rc, matmul, norm_rotate)


def _mla_down(u, w_perm, q_norm, kv_norm, layer, cos_t, sin_t, ql, kvl, mod=None):
    t, d = u.shape
    n = w_perm.shape[1]
    tm = 512
    nrm = lambda width: pl.BlockSpec((1, 1, width), lambda i: (layer, 0, 0))
    rows = lambda width: pl.BlockSpec((tm, width), lambda i: (i, 0))
    in_specs = [rows(d)]
    args = [u]
    if mod is not None:
        ada_rows, seq, ada_layer, comp_sc, comp_sh = mod
        tpb = seq // tm
        ada = lambda comp: pl.BlockSpec((1, 1, d), lambda i: (_ada_row(ada_layer, i // tpb, comp), 0, 0))
        in_specs += [ada(comp_sc), ada(comp_sh)]
        args += [ada_rows, ada_rows]
    in_specs += [pl.BlockSpec((d, n), lambda i: (0, 0)), nrm(ql), nrm(kvl), rows(LANES), rows(LANES)]
    args += [w_perm, q_norm, kv_norm, cos_t, sin_t]
    return pl.pallas_call(
        functools.partial(_mla_down_kernel, ql=ql, kvl=kvl, modulate=mod is not None),
        grid=(t // tm,),
        in_specs=in_specs,
        out_specs=[rows(ql), rows(kvl), rows(LANES)],
        out_shape=[jax.ShapeDtypeStruct((t, ql), BF16), jax.ShapeDtypeStruct((t, kvl), BF16),
                   jax.ShapeDtypeStruct((t, LANES), BF16)],
        compiler_params=_cparams(1, 48),
        name="mla_down",
    )(*args)


def _mla_up_kernel(cq_ref, ckv_ref, wq_ref, wkv_ref, c_ref, s_ref, q_ref, kv_ref, *, scale, heads):
    rc = EPILOGUE_ROWS

    def matmul(t):
        rows = slice(t * rc, (t + 1) * rc)
        return (jnp.dot(cq_ref[rows, :], wq_ref[...], preferred_element_type=F32),
                jnp.dot(ckv_ref[rows, :], wkv_ref[...], preferred_element_type=F32))

    def rotate(t, accs):
        acc, acc_kv = accs
        rows = slice(t * rc, (t + 1) * rc)
        kv_ref[rows, :] = acc_kv.astype(BF16)
        c = c_ref[rows, :]
        s = s_ref[rows, :]
        for h in range(heads):
            b0 = h * MLA_HEAD_PAD
            q_ref[rows, b0:b0 + LANES] = (acc[:, b0:b0 + LANES] * scale).astype(BF16)
            xr = acc[:, b0 + LANES:b0 + MLA_HEAD_PAD]
            q_ref[rows, b0 + LANES:b0 + MLA_HEAD_PAD] = (
                (xr * c + pltpu.roll(xr, LANES // 2, 1) * s) * scale).astype(BF16)

    _staged(cq_ref.shape[0] // rc, matmul, rotate)


def _mla_up(cq, ckv, wq_perm, wkv_stack, layer, cos_t, sin_t, scale):
    t, k = cq.shape
    n = wq_perm.shape[1]
    assert wkv_stack.shape[1:] == (ckv.shape[1], n)
    tm, tn = 1024, 1024
    rows = pl.BlockSpec((tm, LANES), lambda j, i: (i, 0))
    lat = pl.BlockSpec((tm, k), lambda j, i: (i, 0))
    out = pl.BlockSpec((tm, tn), lambda j, i: (i, j))
    return pl.pallas_call(
        functools.partial(_mla_up_kernel, scale=scale, heads=tn // MLA_HEAD_PAD),
        grid=(n // tn, t // tm),
        in_specs=[lat, lat, pl.BlockSpec((k, tn), lambda j, i: (0, j)),
                  pl.BlockSpec((None, k, tn), lambda j, i: (layer, 0, j)), rows, rows],
        out_specs=[out, out],
        out_shape=[jax.ShapeDtypeStruct((t, n), BF16), jax.ShapeDtypeStruct((t, n), BF16)],
        compiler_params=_cparams(2, 48),
        name="mla_up",
    )(cq, ckv, wq_perm, wkv_stack, cos_t, sin_t)


def _moba_qkv_kernel(u_ref, w_ref, tab_ref, o_ref, wb_ref, *, heads):
    @pl.when(pl.program_id(1) == 0)
    def _():
        wb_ref[...] = w_ref[...].astype(BF16)

    half = MOBA_ROT_DIM // 2
    rc = EPILOGUE_ROWS

    def matmul(t):
        return jnp.dot(u_ref[t * rc:(t + 1) * rc, :], wb_ref[...], preferred_element_type=F32)

    def rotate(t, acc):
        rows = slice(t * rc, (t + 1) * rc)
        c = tab_ref[rows, :LANES]
        s1 = tab_ref[rows, LANES:2 * LANES]
        s2 = tab_ref[rows, 2 * LANES:]
        for h in range(heads):
            x = acc[:, h * LANES:(h + 1) * LANES]
            r = x * c + pltpu.roll(x, half, 1) * s1 + pltpu.roll(x, LANES - half, 1) * s2
            o_ref[rows, h * LANES:(h + 1) * LANES] = r.astype(BF16)

    _staged(u_ref.shape[0] // rc, matmul, rotate)


def _moba_qkv(u, w_stack, layer, tables):
    t, k = u.shape
    n = w_stack.shape[2]
    tm, tn = 2048, 512
    tiles_per_sec = (n // 3) // tn
    return pl.pallas_call(
        functools.partial(_moba_qkv_kernel, heads=tn // LANES),
        grid=(n // tn, t // tm),
        in_specs=[pl.BlockSpec((tm, k), lambda j, i: (i, 0)),
                  pl.BlockSpec((None, k, tn), lambda j, i: (layer, 0, j)),
                  pl.BlockSpec((None, tm, 3 * LANES), lambda j, i: (j // tiles_per_sec, i, 0))],
        out_specs=pl.BlockSpec((tm, tn), lambda j, i: (i, j)),
        out_shape=jax.ShapeDtypeStruct((t, n), BF16),
        scratch_shapes=[pltpu.VMEM((k, tn), BF16)],
        compiler_params=_cparams(2, 48),
        name="moba_qkv",
    )(u, w_stack, tables)


_NT = (((1,), (1,)), ((), ()))


def _softmax_numer(s):
    m = jnp.max(s, axis=-1, keepdims=True)
    return jnp.exp2(s - m).astype(BF16)


def _fill_values_ones(vext_ref, v):
    vext_ref[:, :LANES] = v
    vext_ref[:, LANES:] = jnp.ones_like(v)


def _normalised_pv(p, vext):
    o = jnp.dot(p, vext, preferred_element_type=F32)
    return o[:, :LANES] / o[:, LANES:LANES + 1]


def _staged(n_tiles, *stages, reverse=False):
    vals = {}
    for t in range(n_tiles + len(stages) - 1):
        for k, stage in enumerate(stages):
            if 0 <= t - k < n_tiles:
                tile = n_tiles - 1 - (t - k) if reverse else t - k
                vals[tile] = stage(tile) if k == 0 else stage(tile, vals[tile])


def _mla_attn_kernel(q_ref, kv_ref, kr_ref, o_ref, kfull_ref, vext_ref, *, tq):
    seq = q_ref.shape[0]
    kfull_ref[:, :LANES] = kv_ref[:, :LANES]
    kfull_ref[:, LANES:] = kr_ref[...]
    _fill_values_ones(vext_ref, kv_ref[:, LANES:])
    row = lax.broadcasted_iota(jnp.int32, (tq, tq), 0)
    col = lax.broadcasted_iota(jnp.int32, (tq, tq), 1)
    causal = col <= row

    def scores(n):
        q = q_ref[n * tq:(n + 1) * tq, :]
        return lax.dot_general(q, kfull_ref[0:(n + 1) * tq, :], _NT, preferred_element_type=F32)

    def probs(n, s):
        diag = jnp.where(causal, s[:, n * tq:], -jnp.inf)
        return _softmax_numer(jnp.concatenate([s[:, :n * tq], diag], axis=1) if n else diag)

    def output(n, p):
        o_ref[n * tq:(n + 1) * tq, :] = _normalised_pv(p, vext_ref[0:(n + 1) * tq, :]).astype(o_ref.dtype)

    _staged(seq // tq, scores, probs, output, reverse=True)


def _mla_attention(q, kv, kr, batch, seq, heads):
    t = q.shape[0]
    return pl.pallas_call(
        functools.partial(_mla_attn_kernel, tq=256),
        grid=(batch, heads),
        in_specs=[pl.BlockSpec((seq, MLA_HEAD_PAD), lambda b, h: (b, h)),
                  pl.BlockSpec((seq, MLA_NOPE + MLA_V), lambda b, h: (b, h)),
                  pl.BlockSpec((seq, LANES), lambda b, h: (b, 0))],
        out_specs=pl.BlockSpec((seq, MLA_V), lambda b, h: (b, h)),
        out_shape=jax.ShapeDtypeStruct((t, heads * MLA_V), BF16),
        scratch_shapes=[pltpu.VMEM((seq, MLA_HEAD_PAD), BF16), pltpu.VMEM((seq, 2 * LANES), BF16)],
        compiler_params=_cparams(2, 48),
        name="mla_attn",
    )(q, kv, kr)


def _moba_attn_kernel(q_ref, k_ref, v_ref, o_ref, vext_ref, kext_ref, *, nb):
    blk = MOBA_BLOCK
    seq = k_ref.shape[0]
    _fill_values_ones(vext_ref, v_ref[...])
    kext_ref[:, :LANES] = k_ref[...]
    key_blk = lax.broadcasted_iota(jnp.int32, (seq, LANES), 0) // blk
    kext_ref[:, LANES:] = jnp.where(key_blk == lax.broadcasted_iota(jnp.int32, (seq, LANES), 1), 1.0, 0.0).astype(BF16)
    r = lax.broadcasted_iota(jnp.int32, (LANES, seq), 0)
    c = lax.broadcasted_iota(jnp.int32, (LANES, seq), 1)
    ind = jnp.where(c // blk == r, 1.0 / blk, 0.0).astype(BF16)
    km = jnp.dot(ind, k_ref[...], preferred_element_type=F32)
    kmh, kml = _split_bf16(km)
    row = lax.broadcasted_iota(jnp.int32, (blk, blk), 0)
    col = lax.broadcasted_iota(jnp.int32, (blk, blk), 1)
    causal = col <= row

    def gated_queries(n):
        q = q_ref[n * blk:(n + 1) * blk, :]
        if n <= MOBA_TOPK:
            return q
        gate_t = (lax.dot_general(kmh, q, _NT, preferred_element_type=F32)
                  + lax.dot_general(kml, q, _NT, preferred_element_type=F32))[:SUBLANES, :]
        blk_id = lax.broadcasted_iota(jnp.int32, gate_t.shape, 0)
        bias_t = jnp.where(blk_id == n, 0.0, MASKED_SCORE)
        for j in range(n):
            gj = gate_t[j:j + 1, :]
            beats = (blk_id < n) & ((gate_t > gj) | ((gate_t == gj) & (blk_id < j)))
            n_beats = jnp.sum(beats.astype(F32), axis=0, keepdims=True)
            bias_t = jnp.where((blk_id == j) & (n_beats < MOBA_TOPK), 0.0, bias_t)
        bias = jnp.concatenate([bias_t, jnp.zeros((LANES - SUBLANES, blk), F32)], axis=0).T
        return jnp.concatenate([q, bias.astype(BF16)], axis=1)

    def scores(n, q):
        keys = k_ref if q.shape[1] == LANES else kext_ref
        return lax.dot_general(q, keys[0:(n + 1) * blk, :], _NT, preferred_element_type=F32)

    def probs(n, s):
        diag = jnp.where(causal, s[:, n * blk:], -jnp.inf)
        return _softmax_numer(jnp.concatenate([s[:, :n * blk], diag], axis=1) if n else diag)

    def output(n, p):
        o_ref[n * blk:(n + 1) * blk, :] = _normalised_pv(p, vext_ref[0:(n + 1) * blk, :]).astype(o_ref.dtype)

    _staged(nb, gated_queries, scores, probs, output, reverse=True)


def _moba_attention(qkv, batch, seq, heads):
    t = qkv.shape[0]
    d = MOBA_HEAD_DIM
    return pl.pallas_call(
        functools.partial(_moba_attn_kernel, nb=seq // MOBA_BLOCK),
        grid=(batch, heads),
        in_specs=[pl.BlockSpec((seq, d), lambda b, h: (b, h)),
                  pl.BlockSpec((seq, d), lambda b, h: (b, heads + h)),
                  pl.BlockSpec((seq, d), lambda b, h: (b, 2 * heads + h))],
        out_specs=pl.BlockSpec((seq, d), lambda b, h: (b, h)),
        out_shape=jax.ShapeDtypeStruct((t, heads * d), BF16),
        scratch_shapes=[pltpu.VMEM((seq, 2 * LANES), BF16), pltpu.VMEM((seq, 2 * LANES), BF16)],
        compiler_params=_cparams(2, 48),
        name="moba_attn",
    )(qkv, qkv, qkv)


def _swiglu_partial(x, wg, wu, wd):
    g = jnp.dot(x, wg, preferred_element_type=F32)
    u = jnp.dot(x, wu, preferred_element_type=F32)
    a = (g * jax.nn.sigmoid(g) * u).astype(BF16)
    return jnp.dot(a, wd, preferred_element_type=F32)


def _ffn_kernel(u_ref, wg_ref, wu_ref, wd_ref, o_ref):
    f = pl.program_id(1)
    @pl.when(f == 0)
    def _():
        o_ref[...] = jnp.zeros_like(o_ref)

    o_ref[...] += _swiglu_partial(u_ref[...], wg_ref[...].astype(BF16), wu_ref[...].astype(BF16),
                                  wd_ref[...].astype(BF16))


def _dense_ffn(u, w_gate_up, w_down, layer):
    t, d = u.shape
    dff = w_down.shape[1]
    tm, tf = 1024, 256
    nf = dff // tf
    return pl.pallas_call(
        _ffn_kernel,
        grid=(t // tm, nf),
        in_specs=[pl.BlockSpec((tm, d), lambda i, f: (i, 0)),
                  pl.BlockSpec((None, d, tf), lambda i, f: (layer, 0, f)),
                  pl.BlockSpec((None, d, tf), lambda i, f: (layer, 0, nf + f)),
                  pl.BlockSpec((None, tf, d), lambda i, f: (layer, f, 0))],
        out_specs=pl.BlockSpec((tm, d), lambda i, f: (i, 0)),
        out_shape=jax.ShapeDtypeStruct((t, d), F32),
        compiler_params=_cparams(2, 56),
        name="dense_ffn",
    )(u, w_gate_up, w_gate_up, w_down)


def _route_metadata(idx2, n_exp):
    rb, st = MOE_ROW_BLOCK, MOE_SUPER_BLOCKS
    t = idx2.shape[0]
    a = t * TOP_K
    n_over = (a // rb + n_exp - 2) // st
    e_flat = idx2.reshape(a)
    onehot = (e_flat[:, None] == jnp.arange(n_exp, dtype=jnp.int32)[None, :]).astype(jnp.int32)
    csum = jnp.cumsum(onehot, axis=0)
    rank = jnp.sum((csum - onehot) * onehot, axis=1)
    counts = csum[-1]
    nsub = (counts + rb - 1) // rb
    sub_start = jnp.cumsum(nsub) - nsub
    dest = jnp.sum(onehot * (sub_start * rb)[None, :], axis=1) + rank
    p_rows = (a // rb + n_exp) * rb
    token_flat = jnp.arange(a, dtype=jnp.int32) // TOP_K
    row_token = jnp.zeros((p_rows,), jnp.int32).at[dest].set(token_flat)
    i32 = lambda *xs: tuple(x.astype(jnp.int32) for x in xs)
    experts = jnp.arange(n_exp, dtype=jnp.int32)
    first_row0 = jnp.concatenate([sub_start * rb, jnp.sum(nsub, keepdims=True) * rb])
    first = i32(experts, first_row0, jnp.minimum(nsub, st))
    n_it = jnp.maximum((nsub + st - 1) // st - 1, 0)
    it_end = jnp.cumsum(n_it)
    it_start = it_end - n_it
    w = jnp.arange(n_over, dtype=jnp.int32)
    e_w = jnp.sum((it_end[None, :] <= w[:, None]).astype(jnp.int32), axis=1)
    active = e_w < n_exp
    e_c = jnp.minimum(e_w, n_exp - 1)
    local = w - it_start[e_c] + 1
    over_nsub = jnp.where(active, jnp.clip(nsub[e_c] - local * st, 0, st), 0)
    over_row0 = jnp.where(active, (sub_start[e_c] + local * st) * rb, 0)
    e_last = jnp.max(jnp.where(n_it > 0, experts, 0))
    overflow = i32(jnp.where(active, e_c, e_last), over_row0, over_nsub)
    return dest.astype(jnp.int32), row_token, first, overflow, it_end[-1] > 0


ROW_DMA_UNROLL = 8


def _row_copy(src_hbm, src_row, dst, dst_row, sem, chunks):
    pitch = _row_pitch(chunks)
    return pltpu.make_async_copy(src_hbm.at[pl.ds(src_row * pitch, chunks), :],
                                 dst.at[pl.ds(dst_row * pitch, chunks), :], sem)


def _rows_wait(src_hbm, dst, sem, rows, chunks):
    n = rows * chunks
    pltpu.make_async_copy(src_hbm.at[pl.ds(0, n), :], dst.at[pl.ds(0, n), :], sem).wait()


def _dispatch_kernel(tok_ref, u_hbm, o_ref, stage_ref, sem, *, rb, chunks):
    i = pl.program_id(0)
    pitch = _row_pitch(chunks)

    def issue(step, slot):
        def body(g, c):
            for k in range(ROW_DMA_UNROLL):
                r = g * ROW_DMA_UNROLL + k
                _row_copy(u_hbm, tok_ref[step * rb + r], stage_ref.at[slot], r, sem.at[slot],
                          chunks).start(priority=k % 2)
            return c
        lax.fori_loop(0, rb // ROW_DMA_UNROLL, body, 0)

    @pl.when(i == 0)
    def _():
        issue(0, 0)

    @pl.when(i + 1 < pl.num_programs(0))
    def _():
        issue(i + 1, (i + 1) % 2)

    slot = i % 2
    _rows_wait(u_hbm, stage_ref.at[slot], sem.at[slot], rb, chunks)
    for c in range(chunks):
        o_ref[:, c * LANES:(c + 1) * LANES] = stage_ref[slot, pl.ds(c, rb, stride=pitch), :].astype(o_ref.dtype)


def _dispatch(u_lin, row_token, d):
    p_rows = row_token.shape[0]
    rb = 2 * MOE_ROW_BLOCK if p_rows % (2 * MOE_ROW_BLOCK) == 0 else MOE_ROW_BLOCK
    chunks = d // LANES
    return pl.pallas_call(
        functools.partial(_dispatch_kernel, rb=rb, chunks=chunks),
        grid_spec=pltpu.PrefetchScalarGridSpec(
            num_scalar_prefetch=1,
            grid=(p_rows // rb,),
            in_specs=[pl.BlockSpec(memory_space=pl.ANY)],
            out_specs=pl.BlockSpec((rb, d), lambda i, tok: (i, 0)),
            scratch_shapes=[pltpu.VMEM((2, rb * _row_pitch(chunks), LANES), F32),
                            pltpu.SemaphoreType.DMA((2,))]),
        out_shape=jax.ShapeDtypeStruct((p_rows, d), BF16),
        compiler_params=_cparams(1, 40),
        name="moe_dispatch",
    )(row_token, u_lin)


def _moe_ffn_kernel(*refs, rb, nf, n_items, chunks, fill_tail):
    e_ref, row0_ref, nsub_ref, xs_hbm, wg_ref, wu_ref, wd_ref = refs[:7]
    ys_hbm, x_ref, acc_ref, stage_ref, sem = refs[7 if fill_tail else 8:]
    w = pl.program_id(0)
    f = pl.program_id(1)
    nsub = nsub_ref[w]
    row0 = row0_ref[w]

    def for_range(n, fn):
        def body(r, c):
            fn(r)
            return c
        lax.fori_loop(0, n, body, 0)

    def load(r):
        return pltpu.make_async_copy(xs_hbm.at[pl.ds(pl.multiple_of(row0 + r * rb, rb), rb), :],
                                     x_ref.at[pl.ds(pl.multiple_of(r * rb, rb), rb), :], sem.at[0])

    pitch = _row_pitch(chunks)

    def store(row, slot):
        return pltpu.make_async_copy(
            stage_ref.at[slot], ys_hbm.at[pl.ds(pl.multiple_of(row * pitch, rb * pitch), rb * pitch), :],
            sem.at[1 + slot])

    @pl.when(f == 0)
    def _():
        for_range(nsub, lambda r: load(r).start())
        acc_ref[...] = jnp.zeros_like(acc_ref)
        for_range(nsub, lambda r: load(r).wait())

    @pl.when(nsub > 0)
    def _():
        def chunk(start, size):
            rows = pl.ds(pl.multiple_of(start, rb), size)
            acc_ref[rows, :] += _swiglu_partial(x_ref[rows, :], wg_ref[...].astype(BF16),
                                                wu_ref[...].astype(BF16), wd_ref[...].astype(BF16))

        for_range(nsub // 2, lambda p: chunk(p * (2 * rb), 2 * rb))

        @pl.when(nsub % 2 == 1)
        def _():
            chunk((nsub - 1) * rb, rb)

    @pl.when(f == nf - 1)
    def _():
        def emit(r):
            slot = r % 2

            @pl.when(r >= 2)
            def _():
                store(row0 + (r - 2) * rb, slot).wait()

            _store_linear(stage_ref.at[slot], acc_ref[pl.ds(pl.multiple_of(r * rb, rb), rb), :], chunks)
            store(row0 + r * rb, slot).start()

        for_range(nsub, emit)

        @pl.when(nsub >= 2)
        def _():
            store(row0, nsub % 2).wait()

        @pl.when(nsub >= 1)
        def _():
            store(row0, (nsub - 1) % 2).wait()

    if fill_tail:
        @pl.when((f == nf - 1) & (w == n_items - 1))
        def _():
            used = row0_ref[n_items]
            n_tail = (ys_hbm.shape[0] // pitch - used) // rb
            stage_ref[0] = jnp.zeros(stage_ref.shape[1:], F32)
            for_range(n_tail, lambda r: store(used + r * rb, 0).start())
            for_range(n_tail, lambda r: store(used + r * rb, 0).wait())


def _moe_ffn(xs, w_gate_up, w_down, layer, items, ys_in=None):
    item_e, item_row0, item_nsub = items
    extra = () if ys_in is None else (ys_in,)
    hbm = pl.BlockSpec(memory_space=pl.ANY)
    p_rows, d = xs.shape
    dff = w_down.shape[2]
    rb, st = MOE_ROW_BLOCK, MOE_SUPER_BLOCKS
    n_items = item_e.shape[0]
    chunks = d // LANES
    tf = 256
    nf = dff // tf

    def f_eff(f, nsub, w):
        return jnp.where(nsub[w] > 0, f, nf - 1)

    return pl.pallas_call(
        functools.partial(_moe_ffn_kernel, rb=rb, nf=nf, n_items=n_items, chunks=chunks, fill_tail=ys_in is None),
        grid_spec=pltpu.PrefetchScalarGridSpec(
            num_scalar_prefetch=3,
            grid=(n_items, nf),
            in_specs=[hbm,
                      pl.BlockSpec((None, None, d, tf), lambda w, f, e, r0, ns: (layer, e[w], 0, f_eff(f, ns, w))),
                      pl.BlockSpec((None, None, d, tf),
                                   lambda w, f, e, r0, ns: (layer, e[w], 0, nf + f_eff(f, ns, w))),
                      pl.BlockSpec((None, None, tf, d), lambda w, f, e, r0, ns: (layer, e[w], f_eff(f, ns, w), 0))]
            + [hbm] * len(extra),
            out_specs=hbm,
            scratch_shapes=[pltpu.VMEM((st * rb, d), BF16), pltpu.VMEM((st * rb, d), F32),
                            pltpu.VMEM((2, rb * _row_pitch(chunks), LANES), F32), pltpu.SemaphoreType.DMA((3,))]),
        out_shape=jax.ShapeDtypeStruct((p_rows * _row_pitch(chunks), LANES), F32),
        input_output_aliases={} if ys_in is None else {7: 0},
        compiler_params=_cparams(2, 58),
        name="moe_ffn",
    )(item_e, item_row0, item_nsub, xs, w_gate_up, w_gate_up, w_down, *extra)


def _moe_combine_ln_kernel(*refs, alpha, emit_u, tm, chunks):
    if emit_u:
        (pos_ref, ys_hbm, wt_ref, x_ref, g_ref, lg_ref, lb_ref, sc_ref, sh_ref, xo_ref, uo_ref,
         stage_ref, y_ref, sem) = refs
        sc_sh = (sc_ref[0], sh_ref[0])
    else:
        pos_ref, ys_hbm, wt_ref, x_ref, g_ref, lg_ref, lb_ref, xo_ref, stage_ref, y_ref, sem = refs
        sc_sh = None
    i = pl.program_id(0)
    pitch = _row_pitch(chunks)

    def issue(step, slot):
        def body(g, c):
            for j in range(ROW_DMA_UNROLL // TOP_K):
                t = g * (ROW_DMA_UNROLL // TOP_K) + j
                for k in range(TOP_K):
                    _row_copy(ys_hbm, pos_ref[(step * tm + t) * TOP_K + k], stage_ref.at[slot, k], t,
                              sem.at[slot, k], chunks).start(priority=k % 2)
            return c
        lax.fori_loop(0, tm * TOP_K // ROW_DMA_UNROLL, body, 0)

    @pl.when(i == 0)
    def _():
        issue(0, 0)

    @pl.when(i + 1 < pl.num_programs(0))
    def _():
        issue(i + 1, (i + 1) % 2)

    slot = i % 2
    for k in range(TOP_K):
        _rows_wait(ys_hbm, stage_ref.at[slot, k], sem.at[slot, k], tm, chunks)
    w1 = wt_ref[:, 0:1]
    w2 = wt_ref[:, 1:2]
    for c in range(chunks):
        y_ref[:, c * LANES:(c + 1) * LANES] = (stage_ref[slot, 0, pl.ds(c, tm, stride=pitch), :] * w1
                                               + stage_ref[slot, 1, pl.ds(c, tm, stride=pitch), :] * w2)
    z = alpha * x_ref[...] + (1.0 + g_ref[0]) * y_ref[...]
    xn, u = _ln_modulate(z, lg_ref[0], lb_ref[0], sc_sh)
    xo_ref[...] = xn
    if emit_u:
        uo_ref[...] = u.astype(uo_ref.dtype)


def _moe_combine_ln(ys_lin, pos, wt, x, ada_rows, ln_g, ln_b, *, alpha, seq, layer, comp_g, nxt):
    t, d = x.shape
    tm = 256
    tpb = seq // tm
    chunks = d // LANES
    row = lambda l, comp: pl.BlockSpec((1, 1, d), lambda i, p: (_ada_row(l, i // tpb, comp), 0, 0))
    tile = pl.BlockSpec((tm, d), lambda i, p: (i, 0))
    lnp = pl.BlockSpec((1, 1, d), lambda i, p: (layer, 0, 0))
    in_specs = [pl.BlockSpec(memory_space=pl.ANY), pl.BlockSpec((tm, LANES), lambda i, p: (i, 0)), tile,
                row(layer, comp_g), lnp, lnp]
    args = [ys_lin, wt, x, ada_rows, ln_g, ln_b]
    out_specs = [tile]
    out_shape = [jax.ShapeDtypeStruct((t, d), F32)]
    if nxt is not None:
        in_specs += [row(nxt[0], nxt[1]), row(nxt[0], nxt[2])]
        args += [ada_rows, ada_rows]
        out_specs.append(tile)
        out_shape.append(jax.ShapeDtypeStruct((t, d), BF16))
    outs = pl.pallas_call(
        functools.partial(_moe_combine_ln_kernel, alpha=alpha, emit_u=nxt is not None, tm=tm, chunks=chunks),
        grid_spec=pltpu.PrefetchScalarGridSpec(
            num_scalar_prefetch=1,
            grid=(t // tm,),
            in_specs=in_specs, out_specs=out_specs,
            scratch_shapes=[pltpu.VMEM((2, TOP_K, tm * _row_pitch(chunks), LANES), F32), pltpu.VMEM((tm, d), F32),
                            pltpu.SemaphoreType.DMA((2, TOP_K))]),
        out_shape=out_shape,
        compiler_params=_cparams(1, 48),
        name="moe_combine_ln",
    )(pos, *args)
    return (outs[0], outs[1]) if nxt is not None else (outs[0], None)


def _rope_cos_sin(positions, dim):
    inv_freq = ROPE_THETA ** (-jnp.arange(0, dim, 2, dtype=F32) / dim)
    ang = positions.astype(F32).reshape(-1)[:, None] * inv_freq
    return jnp.cos(ang), jnp.sin(ang)


def _mla_rope_tables(positions):
    cos, sin = _rope_cos_sin(positions, MLA_ROPE)
    z = jnp.zeros_like(cos)
    return jnp.concatenate([cos, z, cos, z], axis=1), jnp.concatenate([-sin, z, sin, z], axis=1)


def _moba_rope_tables(positions, scale):
    cos, sin = _rope_cos_sin(positions, MOBA_ROT_DIM)
    t, half = cos.shape
    rest = LANES - 2 * half
    c = jnp.concatenate([cos, cos, jnp.ones((t, rest), F32)], axis=1)
    s1 = jnp.concatenate([jnp.zeros((t, half), F32), sin, jnp.zeros((t, rest), F32)], axis=1)
    s2 = jnp.concatenate([-sin, jnp.zeros((t, half + rest), F32)], axis=1)
    rot = jnp.concatenate([c, s1, s2], axis=1)
    ident = jnp.concatenate([jnp.ones((t, LANES), F32), jnp.zeros((t, 2 * LANES), F32)], axis=1)
    return jnp.stack([rot * scale, rot, ident])


def _spread_rope_cols(w_rope):
    half = MLA_ROPE // 2
    z = jnp.zeros(w_rope.shape[:-1] + (LANES // 2 - half,), w_rope.dtype)
    return jnp.concatenate([w_rope[..., :half], z, w_rope[..., half:], z], axis=-1)


def _mla_weights(w_down, w_uq, ql, kvl):
    k = w_uq.shape[0]
    heads = w_uq.shape[1] // (MLA_NOPE + MLA_ROPE)
    wd = jnp.concatenate([w_down[:, :ql + kvl], _spread_rope_cols(w_down[:, ql + kvl:])], axis=1).astype(BF16)
    wq = w_uq.reshape(k, heads, MLA_NOPE + MLA_ROPE)
    wq = jnp.concatenate([wq[..., :MLA_NOPE], _spread_rope_cols(wq[..., MLA_NOPE:])], axis=-1)
    return wd, wq.reshape(k, heads * MLA_HEAD_PAD).astype(BF16), heads


def kernel(x, c, positions, w_ada, b_ada, ln_mix_g, ln_mix_b, ln_ffn_g, ln_ffn_b, mla_w_down, mla_q_norm,
           mla_kv_norm, mla_w_uq, mla_w_ukv, mla_w_o, moba_w_qkv, moba_w_o, ffn_w_gate_up, ffn_w_down,
           moe_w_router, moe_w_gate_up, moe_w_down):
    batch, seq, d = x.shape
    depth = w_ada.shape[0]
    t = batch * seq
    alpha = (2.0 * depth) ** 0.25
    ql = mla_q_norm.shape[1]
    kvl = mla_kv_norm.shape[1]
    n_exp = moe_w_router.shape[2]

    ada_rows = _ada_all(c, w_ada, b_ada)
    cos_mla, sin_mla = _mla_rope_tables(positions)
    moba_tables = _moba_rope_tables(positions, MOBA_HEAD_DIM ** -0.5 * LOG2_E)
    mla_w_ukv_b = mla_w_ukv.astype(BF16)
    mla_w_o_b = mla_w_o.astype(BF16)
    moba_w_o_b = moba_w_o.astype(BF16)
    ln3 = lambda p: p.reshape(depth, 1, d)
    ln_mix_g, ln_mix_b, ln_ffn_g, ln_ffn_b = ln3(ln_mix_g), ln3(ln_mix_b), ln3(ln_ffn_g), ln3(ln_ffn_b)
    q_norm3 = mla_q_norm.reshape(-1, 1, ql)
    kv_norm3 = mla_kv_norm.reshape(-1, 1, kvl)

    xf = x.reshape(t, d)
    u = None
    for l in range(depth):
        j = l // 2
        moe_layer = l % 2 == 1
        if l % 2 == 0:
            wd_p, wq_p, heads = _mla_weights(mla_w_down[j], mla_w_uq[j], ql, kvl)
            first = (xf, (ada_rows, seq, l, 1, 0)) if u is None else (u, None)
            cq, ckv, kr = _mla_down(first[0], wd_p, q_norm3, kv_norm3, j, cos_mla, sin_mla, ql, kvl, mod=first[1])
            q, kv = _mla_up(cq, ckv, wq_p, mla_w_ukv_b, j, cos_mla, sin_mla,
                            (MLA_NOPE + MLA_ROPE) ** -0.5 * LOG2_E)
            o = _mla_attention(q, kv, kr, batch, seq, heads)
            y = (o, mla_w_o_b, j)
        else:
            heads = moba_w_qkv.shape[2] // (3 * MOBA_HEAD_DIM)
            qkv = _moba_qkv(u, moba_w_qkv, j, moba_tables)
            o = _moba_attention(qkv, batch, seq, heads)
            y = (o, moba_w_o_b, j)
        ln_args = dict(alpha=alpha, seq=seq, layer=l)
        nxt = (l + 1, 1, 0) if l + 1 < depth else None
        if not moe_layer:
            xf, u = _resid_ln(xf, y, ada_rows, ln_mix_g, ln_mix_b, comp_g=2, nxt=(l, 4, 3), **ln_args)
            y = _dense_ffn(u, ffn_w_gate_up, ffn_w_down, j)
            outs = _resid_ln(xf, y, ada_rows, ln_ffn_g, ln_ffn_b, comp_g=5, nxt=nxt, **ln_args)
            xf, u = outs[0], (outs[1] if nxt is not None else None)
        else:
            w_router_pad = jnp.zeros((d, LANES), F32).at[:, :n_exp].set(moe_w_router[j])
            xf, u_lin, idx, wt = _resid_ln(xf, y, ada_rows, ln_mix_g, ln_mix_b, comp_g=2, nxt=(l, 4, 3),
                                           w_router_pad=w_router_pad, n_exp=n_exp, **ln_args)
            pos, row_token, first, overflow, any_overflow = _route_metadata(idx[:, :TOP_K], n_exp)
            xs = _dispatch(u_lin, row_token, d)
            ys_lin = _moe_ffn(xs, moe_w_gate_up, moe_w_down, j, first)
            ys_lin = lax.cond(any_overflow,
                              lambda ys: _moe_ffn(xs, moe_w_gate_up, moe_w_down, j, overflow, ys_in=ys),
                              lambda ys: ys, ys_lin)
            xf, u = _moe_combine_ln(ys_lin, pos, wt, xf, ada_rows, ln_ffn_g, ln_ffn_b, comp_g=5, nxt=nxt, **ln_args)
    return xf.reshape(batch, seq, d)
```

```python
import functools

import jax
import jax.numpy as jnp
from jax import lax
from jax.experimental import pallas as pl
from jax.experimental.pallas import tpu as pltpu

F32 = jnp.float32
BF16 = jnp.bfloat16

ROPE_THETA = 500000.0
LN_EPS = 1e-5
RMS_EPS = 1e-6
MLA_NOPE = 128
MLA_ROPE = 64
MLA_V = 128
MOBA_HEAD_DIM = 128
MOBA_ROT_DIM = 32
MOBA_BLOCK = 256
MOBA_TOPK = 3
TOP_K = 2
LOG2_E = 1.4426950408889634
MASKED_SCORE = -1e30

LANES = 128
SUBLANES = 8
MLA_HEAD_PAD = 2 * LANES

ADA_BATCH_PAD = SUBLANES
MOE_ROW_BLOCK = 256
MOE_SUPER_BLOCKS = 10
MOE_CHUNK_BLOCKS = 3
EPILOGUE_ROWS = 256
PROJ_LN_ROWS = 128


def _cparams(n_axes, vmem_mb):
    return pltpu.CompilerParams(dimension_semantics=("arbitrary",) * n_axes,
                                vmem_limit_bytes=vmem_mb * 1024 * 1024)


def _split_bf16(x):
    hi = x.astype(BF16)
    return hi, (x - hi.astype(F32)).astype(BF16)


def _ada_row(layer, batch, comp):
    return (layer * ADA_BATCH_PAD + batch) * 6 + comp


def _ada_kernel(c_ref, w_ref, b_ref, o_ref):
    c = c_ref[...]
    ca = (c * jax.nn.sigmoid(c)).astype(BF16)
    o_ref[...] = jnp.dot(ca, w_ref[...].astype(BF16), preferred_element_type=F32) + b_ref[...]


def _ada_all(c, w_ada, b_ada):
    depth, d, n6 = w_ada.shape
    b = c.shape[0]
    c_pad = jnp.zeros((ADA_BATCH_PAD, d), F32).at[:b].set(c)
    tn = 1024
    out = pl.pallas_call(
        _ada_kernel,
        grid=(depth, n6 // tn),
        in_specs=[pl.BlockSpec((ADA_BATCH_PAD, d), lambda l, j: (0, 0)),
                  pl.BlockSpec((None, d, tn), lambda l, j: (l, 0, j)),
                  pl.BlockSpec((None, 1, tn), lambda l, j: (l, 0, j))],
        out_specs=pl.BlockSpec((None, ADA_BATCH_PAD, tn), lambda l, j: (l, 0, j)),
        out_shape=jax.ShapeDtypeStruct((depth, ADA_BATCH_PAD, n6), F32),
        compiler_params=_cparams(2, 40),
        name="ada",
    )(c_pad, w_ada, b_ada.reshape(depth, 1, n6))
    return out.reshape(depth * ADA_BATCH_PAD * 6, 1, d)


def _ln_modulate(z, lg, lb, sc_sh):
    mu = jnp.mean(z, axis=-1, keepdims=True)
    zc = z - mu
    var = jnp.mean(zc * zc, axis=-1, keepdims=True)
    xn = zc * lax.rsqrt(var + LN_EPS) * lg + lb
    if sc_sh is None:
        return xn, None
    sc, sh = sc_sh
    return xn, xn * (1.0 + sc) + sh


def _row_pitch(chunks):
    return chunks + 1


def _store_linear(dst_ref, val, chunks):
    rows = val.shape[0]
    pitch = _row_pitch(chunks)
    for c in range(chunks):
        dst_ref[pl.ds(c, rows, stride=pitch), :] = val[:, c * LANES:(c + 1) * LANES]
    dst_ref[pl.ds(chunks, rows, stride=pitch), :] = jnp.zeros((rows, LANES), val.dtype)


def _top2_route(logits, n_exp):
    lane = lax.broadcasted_iota(jnp.int32, logits.shape, 1)
    lg = jnp.where(lane < n_exp, logits, -jnp.inf)
    m1 = jnp.max(lg, axis=-1, keepdims=True)
    i1 = jnp.min(jnp.where(lg == m1, lane, LANES), axis=-1, keepdims=True)
    lg2 = jnp.where(lane == i1, -jnp.inf, lg)
    m2 = jnp.max(lg2, axis=-1, keepdims=True)
    i2 = jnp.min(jnp.where(lg2 == m2, lane, LANES), axis=-1, keepdims=True)
    e = jnp.exp(m2 - m1)
    w1 = 1.0 / (1.0 + e)
    w2 = e / (1.0 + e)
    idx = jnp.where(lane == 0, i1, jnp.where(lane == 1, i2, 0))
    wt = jnp.where(lane == 0, w1, jnp.where(lane == 1, w2, 0.0))
    return idx, wt


def _resid_ln_kernel(*refs, alpha, mode, n_exp, chunks, proj):
    if proj:
        x_ref, a_ref, w_ref, g_ref, lg_ref, lb_ref = refs[:6]
        rest = refs[6:]
    else:
        x_ref, y_ref, g_ref, lg_ref, lb_ref = refs[:5]
        rest = refs[5:]
    if mode == "last":
        (xo_ref,) = rest
        sc_sh = None
    elif mode == "next":
        sc_ref, sh_ref, xo_ref, uo_ref = rest
        sc_sh = (sc_ref[0], sh_ref[0])
    else:
        sc_ref, sh_ref, wr_ref, xo_ref, uo_ref, idx_ref, wt_ref = rest
        sc_sh = (sc_ref[0], sh_ref[0])
    tm = x_ref.shape[0]
    rc = PROJ_LN_ROWS if proj else tm
    if mode == "route":
        w_cat = jnp.concatenate(_split_bf16(wr_ref[...]), axis=1)

    def project(t):
        return jnp.dot(a_ref[t * rc:(t + 1) * rc, :], w_ref[...], preferred_element_type=F32)

    def finish(t, y):
        rows = slice(t * rc, (t + 1) * rc)
        z = alpha * x_ref[rows, :] + (1.0 + g_ref[0]) * y
        xn, u = _ln_modulate(z, lg_ref[0], lb_ref[0], sc_sh)
        xo_ref[rows, :] = xn
        if mode == "next":
            uo_ref[rows, :] = u.astype(uo_ref.dtype)
        elif mode == "route":
            pitch = _row_pitch(chunks)
            _store_linear(uo_ref.at[t * rc * pitch:(t + 1) * rc * pitch, :], u, chunks)
            uh, ul = _split_bf16(u)
            r = (jnp.dot(uh, w_cat, preferred_element_type=F32) + jnp.dot(ul, w_cat, preferred_element_type=F32))
            idx, wt = _top2_route(r[:, :LANES] + r[:, LANES:], n_exp)
            idx_ref[rows, :] = idx
            wt_ref[rows, :] = wt

    if proj:
        _staged(tm // rc, project, finish)
    else:
        finish(0, y_ref[...].astype(F32))


def _resid_ln(x, y, ada_rows, ln_g, ln_b, *, alpha, seq, layer, comp_g, nxt, w_router_pad=None, n_exp=0):
    t, d = x.shape
    proj = isinstance(y, tuple)
    tm = 512 if proj else 256
    tpb = seq // tm
    chunks = d // LANES
    mode = "last" if nxt is None else ("route" if w_router_pad is not None else "next")
    row = lambda l, comp: pl.BlockSpec((1, 1, d), lambda i: (_ada_row(l, i // tpb, comp), 0, 0))
    tile = pl.BlockSpec((tm, d), lambda i: (i, 0))
    lanes = pl.BlockSpec((tm, LANES), lambda i: (i, 0))
    lnp = pl.BlockSpec((1, 1, d), lambda i: (layer, 0, 0))
    if proj:
        a, w_stack, j = y
        k = a.shape[1]
        in_specs = [tile, pl.BlockSpec((tm, k), lambda i: (i, 0)), pl.BlockSpec((None, k, d), lambda i: (j, 0, 0))]
        args = [x, a, w_stack]
    else:
        in_specs = [tile, tile]
        args = [x, y]
    in_specs += [row(layer, comp_g), lnp, lnp]
    args += [ada_rows, ln_g, ln_b]
    out_specs = [tile]
    out_shape = [jax.ShapeDtypeStruct((t, d), F32)]
    if mode != "last":
        in_specs += [row(nxt[0], nxt[1]), row(nxt[0], nxt[2])]
        args += [ada_rows, ada_rows]
    if mode == "next":
        out_specs.append(tile)
        out_shape.append(jax.ShapeDtypeStruct((t, d), BF16))
    elif mode == "route":
        pitch = _row_pitch(chunks)
        in_specs.append(pl.BlockSpec((d, LANES), lambda i: (0, 0)))
        args.append(w_router_pad)
        out_specs += [pl.BlockSpec((tm * pitch, LANES), lambda i: (i, 0)), lanes, lanes]
        out_shape += [jax.ShapeDtypeStruct((t * pitch, LANES), F32),
                      jax.ShapeDtypeStruct((t, LANES), jnp.int32), jax.ShapeDtypeStruct((t, LANES), F32)]
    return pl.pallas_call(
        functools.partial(_resid_ln_kernel, alpha=alpha, mode=mode, n_exp=n_exp, chunks=chunks, proj=proj),
        grid=(t // tm,),
        in_specs=in_specs, out_specs=out_specs, out_shape=out_shape,
        compiler_params=_cparams(1, 56 if proj else 48),
        name=("proj_ln_" if proj else "resid_ln_") + mode,
    )(*args)


def _rms(x, g):
    ms = jnp.mean(x * x, axis=-1, keepdims=True)
    return x * lax.rsqrt(ms + RMS_EPS) * g


def _mla_down_kernel(*refs, ql, kvl, modulate):
    if modulate:
        u_ref, sc_ref, sh_ref, w_ref, qn_ref, kvn_ref, c_ref, s_ref, cq_ref, ckv_ref, kr_ref = refs
    else:
        u_ref, w_ref, qn_ref, kvn_ref, c_ref, s_ref, cq_ref, ckv_ref, kr_ref = refs
    rc = EPILOGUE_ROWS

    def matmul(t):
        u = u_ref[t * rc:(t + 1) * rc, :]
        if modulate:
            u = (u * (1.0 + sc_ref[0]) + sh_ref[0]).astype(BF16)
        return jnp.dot(u, w_ref[...], preferred_element_type=F32)

    def norm_rotate(t, acc):
        rows = slice(t * rc, (t + 1) * rc)
        cq_ref[rows, :] = _rms(acc[:, :ql], qn_ref[0]).astype(BF16)
        ckv_ref[rows, :] = _rms(acc[:, ql:ql + kvl], kvn_ref[0]).astype(BF16)
        xr = acc[:, ql + kvl:]
        kr_ref[rows, :] = (xr * c_ref[rows, :] + pltpu.roll(xr, LANES // 2, 1) * s_ref[rows, :]).astype(BF16)

    _staged(u_ref.shape[0] // rc, matmul, norm_rotate)


def _mla_down(u, w_perm, q_norm, kv_norm, layer, cos_t, sin_t, ql, kvl, mod=None):
    t, d = u.shape
    n = w_perm.shape[1]
    tm = 512
    nrm = lambda width: pl.BlockSpec((1, 1, width), lambda i: (layer, 0, 0))
    rows = lambda width: pl.BlockSpec((tm, width), lambda i: (i, 0))
    in_specs = [rows(d)]
    args = [u]
    if mod is not None:
        ada_rows, seq, ada_layer, comp_sc, comp_sh = mod
        tpb = seq // tm
        ada = lambda comp: pl.BlockSpec((1, 1, d), lambda i: (_ada_row(ada_layer, i // tpb, comp), 0, 0))
        in_specs += [ada(comp_sc), ada(comp_sh)]
        args += [ada_rows, ada_rows]
    in_specs += [pl.BlockSpec((d, n), lambda i: (0, 0)), nrm(ql), nrm(kvl), rows(LANES), rows(LANES)]
    args += [w_perm, q_norm, kv_norm, cos_t, sin_t]
    return pl.pallas_call(
        functools.partial(_mla_down_kernel, ql=ql, kvl=kvl, modulate=mod is not None),
        grid=(t // tm,),
        in_specs=in_specs,
        out_specs=[rows(ql), rows(kvl), rows(LANES)],
        out_shape=[jax.ShapeDtypeStruct((t, ql), BF16), jax.ShapeDtypeStruct((t, kvl), BF16),
                   jax.ShapeDtypeStruct((t, LANES), BF16)],
        compiler_params=_cparams(1, 48),
        name="mla_down",
    )(*args)


def _mla_up_kernel(cq_ref, ckv_ref, wq_ref, wkv_ref, c_ref, s_ref, q_ref, kv_ref, *, scale, heads):
    rc = EPILOGUE_ROWS

    def matmul(t):
        rows = slice(t * rc, (t + 1) * rc)
        return (jnp.dot(cq_ref[rows, :], wq_ref[...], preferred_element_type=F32),
                jnp.dot(ckv_ref[rows, :], wkv_ref[...], preferred_element_type=F32))

    def rotate(t, accs):
        acc, acc_kv = accs
        rows = slice(t * rc, (t + 1) * rc)
        kv_ref[rows, :] = acc_kv.astype(BF16)
        c = c_ref[rows, :]
        s = s_ref[rows, :]
        for h in range(heads):
            b0 = h * MLA_HEAD_PAD
            q_ref[rows, b0:b0 + LANES] = (acc[:, b0:b0 + LANES] * scale).astype(BF16)
            xr = acc[:, b0 + LANES:b0 + MLA_HEAD_PAD]
            q_ref[rows, b0 + LANES:b0 + MLA_HEAD_PAD] = (
                (xr * c + pltpu.roll(xr, LANES // 2, 1) * s) * scale).astype(BF16)

    _staged(cq_ref.shape[0] // rc, matmul, rotate)


def _mla_up(cq, ckv, wq_perm, wkv_stack, layer, cos_t, sin_t, scale):
    t, k = cq.shape
    n = wq_perm.shape[1]
    assert wkv_stack.shape[1:] == (ckv.shape[1], n)
    tm, tn = 1024, 1024
    rows = pl.BlockSpec((tm, LANES), lambda j, i: (i, 0))
    lat = pl.BlockSpec((tm, k), lambda j, i: (i, 0))
    out = pl.BlockSpec((tm, tn), lambda j, i: (i, j))
    return pl.pallas_call(
        functools.partial(_mla_up_kernel, scale=scale, heads=tn // MLA_HEAD_PAD),
        grid=(n // tn, t // tm),
        in_specs=[lat, lat, pl.BlockSpec((k, tn), lambda j, i: (0, j)),
                  pl.BlockSpec((None, k, tn), lambda j, i: (layer, 0, j)), rows, rows],
        out_specs=[out, out],
        out_shape=[jax.ShapeDtypeStruct((t, n), BF16), jax.ShapeDtypeStruct((t, n), BF16)],
        compiler_params=_cparams(2, 48),
        name="mla_up",
    )(cq, ckv, wq_perm, wkv_stack, cos_t, sin_t)


def _moba_qkv_kernel(u_ref, w_ref, tab_ref, o_ref, wb_ref, *, heads):
    @pl.when(pl.program_id(1) == 0)
    def _():
        wb_ref[...] = w_ref[...].astype(BF16)

    half = MOBA_ROT_DIM // 2
    rc = EPILOGUE_ROWS

    def matmul(t):
        return jnp.dot(u_ref[t * rc:(t + 1) * rc, :], wb_ref[...], preferred_element_type=F32)

    def rotate(t, acc):
        rows = slice(t * rc, (t + 1) * rc)
        c = tab_ref[rows, :LANES]
        s1 = tab_ref[rows, LANES:2 * LANES]
        s2 = tab_ref[rows, 2 * LANES:]
        for h in range(heads):
            x = acc[:, h * LANES:(h + 1) * LANES]
            r = x * c + pltpu.roll(x, half, 1) * s1 + pltpu.roll(x, LANES - half, 1) * s2
            o_ref[rows, h * LANES:(h + 1) * LANES] = r.astype(BF16)

    _staged(u_ref.shape[0] // rc, matmul, rotate)


def _moba_qkv(u, w_stack, layer, tables):
    t, k = u.shape
    n = w_stack.shape[2]
    tm, tn = 2048, 512
    tiles_per_sec = (n // 3) // tn
    return pl.pallas_call(
        functools.partial(_moba_qkv_kernel, heads=tn // LANES),
        grid=(n // tn, t // tm),
        in_specs=[pl.BlockSpec((tm, k), lambda j, i: (i, 0)),
                  pl.BlockSpec((None, k, tn), lambda j, i: (layer, 0, j)),
                  pl.BlockSpec((None, tm, 3 * LANES), lambda j, i: (j // tiles_per_sec, i, 0))],
        out_specs=pl.BlockSpec((tm, tn), lambda j, i: (i, j)),
        out_shape=jax.ShapeDtypeStruct((t, n), BF16),
        scratch_shapes=[pltpu.VMEM((k, tn), BF16)],
        compiler_params=_cparams(2, 48),
        name="moba_qkv",
    )(u, w_stack, tables)


_NT = (((1,), (1,)), ((), ()))


def _softmax_numer(s):
    m = jnp.max(s, axis=-1, keepdims=True)
    return jnp.exp2(s - m).astype(BF16)


def _fill_values_ones(vext_ref, v):
    vext_ref[:, :LANES] = v
    vext_ref[:, LANES:] = jnp.ones_like(v)


def _normalised_pv(p, vext):
    o = jnp.dot(p, vext, preferred_element_type=F32)
    return o[:, :LANES] / o[:, LANES:LANES + 1]


def _staged(n_tiles, *stages, reverse=False):
    vals = {}
    for t in range(n_tiles + len(stages) - 1):
        for k, stage in enumerate(stages):
            if 0 <= t - k < n_tiles:
                tile = n_tiles - 1 - (t - k) if reverse else t - k
                vals[tile] = stage(tile) if k == 0 else stage(tile, vals[tile])


def _mla_attn_kernel(q_ref, kv_ref, kr_ref, o_ref, kfull_ref, vext_ref, *, tq):
    seq = q_ref.shape[0]
    kfull_ref[:, :LANES] = kv_ref[:, :LANES]
    kfull_ref[:, LANES:] = kr_ref[...]
    _fill_values_ones(vext_ref, kv_ref[:, LANES:])
    row = lax.broadcasted_iota(jnp.int32, (tq, tq), 0)
    col = lax.broadcasted_iota(jnp.int32, (tq, tq), 1)
    causal = col <= row

    def scores(n):
        q = q_ref[n * tq:(n + 1) * tq, :]
        return lax.dot_general(q, kfull_ref[0:(n + 1) * tq, :], _NT, preferred_element_type=F32)

    def probs(n, s):
        diag = jnp.where(causal, s[:, n * tq:], -jnp.inf)
        return _softmax_numer(jnp.concatenate([s[:, :n * tq], diag], axis=1) if n else diag)

    def output(n, p):
        o_ref[n * tq:(n + 1) * tq, :] = _normalised_pv(p, vext_ref[0:(n + 1) * tq, :]).astype(o_ref.dtype)

    _staged(seq // tq, scores, probs, output, reverse=True)


def _mla_attention(q, kv, kr, batch, seq, heads):
    t = q.shape[0]
    return pl.pallas_call(
        functools.partial(_mla_attn_kernel, tq=256),
        grid=(batch, heads),
        in_specs=[pl.BlockSpec((seq, MLA_HEAD_PAD), lambda b, h: (b, h)),
                  pl.BlockSpec((seq, MLA_NOPE + MLA_V), lambda b, h: (b, h)),
                  pl.BlockSpec((seq, LANES), lambda b, h: (b, 0))],
        out_specs=pl.BlockSpec((seq, MLA_V), lambda b, h: (b, h)),
        out_shape=jax.ShapeDtypeStruct((t, heads * MLA_V), BF16),
        scratch_shapes=[pltpu.VMEM((seq, MLA_HEAD_PAD), BF16), pltpu.VMEM((seq, 2 * LANES), BF16)],
        compiler_params=_cparams(2, 48),
        name="mla_attn",
    )(q, kv, kr)


def _moba_attn_kernel(q_ref, k_ref, v_ref, o_ref, vext_ref, kext_ref, *, nb):
    blk = MOBA_BLOCK
    seq = k_ref.shape[0]
    _fill_values_ones(vext_ref, v_ref[...])
    kext_ref[:, :LANES] = k_ref[...]
    key_blk = lax.broadcasted_iota(jnp.int32, (seq, LANES), 0) // blk
    kext_ref[:, LANES:] = jnp.where(key_blk == lax.broadcasted_iota(jnp.int32, (seq, LANES), 1), 1.0, 0.0).astype(BF16)
    r = lax.broadcasted_iota(jnp.int32, (LANES, seq), 0)
    c = lax.broadcasted_iota(jnp.int32, (LANES, seq), 1)
    ind = jnp.where(c // blk == r, 1.0 / blk, 0.0).astype(BF16)
    km = jnp.dot(ind, k_ref[...], preferred_element_type=F32)
    kmh, kml = _split_bf16(km)
    row = lax.broadcasted_iota(jnp.int32, (blk, blk), 0)
    col = lax.broadcasted_iota(jnp.int32, (blk, blk), 1)
    causal = col <= row

    def gated_queries(n):
        q = q_ref[n * blk:(n + 1) * blk, :]
        if n <= MOBA_TOPK:
            return q
        gate_t = (lax.dot_general(kmh, q, _NT, preferred_element_type=F32)
                  + lax.dot_general(kml, q, _NT, preferred_element_type=F32))[:SUBLANES, :]
        blk_id = lax.broadcasted_iota(jnp.int32, gate_t.shape, 0)
        bias_t = jnp.where(blk_id == n, 0.0, MASKED_SCORE)
        for j in range(n):
            gj = gate_t[j:j + 1, :]
            beats = (blk_id < n) & ((gate_t > gj) | ((gate_t == gj) & (blk_id < j)))
            n_beats = jnp.sum(beats.astype(F32), axis=0, keepdims=True)
            bias_t = jnp.where((blk_id == j) & (n_beats < MOBA_TOPK), 0.0, bias_t)
        bias = jnp.concatenate([bias_t, jnp.zeros((LANES - SUBLANES, blk), F32)], axis=0).T
        return jnp.concatenate([q, bias.astype(BF16)], axis=1)

    def scores(n, q):
        keys = k_ref if q.shape[1] == LANES else kext_ref
        return lax.dot_general(q, keys[0:(n + 1) * blk, :], _NT, preferred_element_type=F32)

    def probs(n, s):
        diag = jnp.where(causal, s[:, n * blk:], -jnp.inf)
        return _softmax_numer(jnp.concatenate([s[:, :n * blk], diag], axis=1) if n else diag)

    def output(n, p):
        o_ref[n * blk:(n + 1) * blk, :] = _normalised_pv(p, vext_ref[0:(n + 1) * blk, :]).astype(o_ref.dtype)

    _staged(nb, gated_queries, scores, probs, output, reverse=True)


def _moba_attention(qkv, batch, seq, heads):
    t = qkv.shape[0]
    d = MOBA_HEAD_DIM
    return pl.pallas_call(
        functools.partial(_moba_attn_kernel, nb=seq // MOBA_BLOCK),
        grid=(batch, heads),
        in_specs=[pl.BlockSpec((seq, d), lambda b, h: (b, h)),
                  pl.BlockSpec((seq, d), lambda b, h: (b, heads + h)),
                  pl.BlockSpec((seq, d), lambda b, h: (b, 2 * heads + h))],
        out_specs=pl.BlockSpec((seq, d), lambda b, h: (b, h)),
        out_shape=jax.ShapeDtypeStruct((t, heads * d), BF16),
        scratch_shapes=[pltpu.VMEM((seq, 2 * LANES), BF16), pltpu.VMEM((seq, 2 * LANES), BF16)],
        compiler_params=_cparams(2, 48),
        name="moba_attn",
    )(qkv, qkv, qkv)


def _swiglu_partial(x, wg, wu, wd):
    g = jnp.dot(x, wg, preferred_element_type=F32)
    u = jnp.dot(x, wu, preferred_element_type=F32)
    a = (g * jax.nn.sigmoid(g) * u).astype(BF16)
    return jnp.dot(a, wd, preferred_element_type=F32)


def _ffn_kernel(u_ref, wg_ref, wu_ref, wd_ref, o_ref):
    f = pl.program_id(1)
    @pl.when(f == 0)
    def _():
        o_ref[...] = jnp.zeros_like(o_ref)

    o_ref[...] += _swiglu_partial(u_ref[...], wg_ref[...].astype(BF16), wu_ref[...].astype(BF16),
                                  wd_ref[...].astype(BF16))


def _dense_ffn(u, w_gate_up, w_down, layer):
    t, d = u.shape
    dff = w_down.shape[1]
    tm, tf = 1024, 256
    nf = dff // tf
    return pl.pallas_call(
        _ffn_kernel,
        grid=(t // tm, nf),
        in_specs=[pl.BlockSpec((tm, d), lambda i, f: (i, 0)),
                  pl.BlockSpec((None, d, tf), lambda i, f: (layer, 0, f)),
                  pl.BlockSpec((None, d, tf), lambda i, f: (layer, 0, nf + f)),
                  pl.BlockSpec((None, tf, d), lambda i, f: (layer, f, 0))],
        out_specs=pl.BlockSpec((tm, d), lambda i, f: (i, 0)),
        out_shape=jax.ShapeDtypeStruct((t, d), F32),
        compiler_params=_cparams(2, 56),
        name="dense_ffn",
    )(u, w_gate_up, w_gate_up, w_down)


def _route_metadata(idx2, n_exp):
    rb, st = MOE_ROW_BLOCK, MOE_SUPER_BLOCKS
    t = idx2.shape[0]
    a = t * TOP_K
    n_over = (a // rb + n_exp - 2) // st
    e_flat = idx2.reshape(a)
    onehot = (e_flat[:, None] == jnp.arange(n_exp, dtype=jnp.int32)[None, :]).astype(jnp.int32)
    csum = jnp.cumsum(onehot, axis=0)
    rank = jnp.sum((csum - onehot) * onehot, axis=1)
    counts = csum[-1]
    nsub = (counts + rb - 1) // rb
    sub_start = jnp.cumsum(nsub) - nsub
    dest = jnp.sum(onehot * (sub_start * rb)[None, :], axis=1) + rank
    p_rows = (a // rb + n_exp) * rb
    token_flat = jnp.arange(a, dtype=jnp.int32) // TOP_K
    row_token = jnp.zeros((p_rows,), jnp.int32).at[dest].set(token_flat)
    i32 = lambda *xs: tuple(x.astype(jnp.int32) for x in xs)
    experts = jnp.arange(n_exp, dtype=jnp.int32)
    first_row0 = jnp.concatenate([sub_start * rb, jnp.sum(nsub, keepdims=True) * rb])
    first = i32(experts, first_row0, jnp.minimum(nsub, st))
    n_it = jnp.maximum((nsub + st - 1) // st - 1, 0)
    it_end = jnp.cumsum(n_it)
    it_start = it_end - n_it
    w = jnp.arange(n_over, dtype=jnp.int32)
    e_w = jnp.sum((it_end[None, :] <= w[:, None]).astype(jnp.int32), axis=1)
    active = e_w < n_exp
    e_c = jnp.minimum(e_w, n_exp - 1)
    local = w - it_start[e_c] + 1
    over_nsub = jnp.where(active, jnp.clip(nsub[e_c] - local * st, 0, st), 0)
    over_row0 = jnp.where(active, (sub_start[e_c] + local * st) * rb, 0)
    e_last = jnp.max(jnp.where(n_it > 0, experts, 0))
    overflow = i32(jnp.where(active, e_c, e_last), over_row0, over_nsub)
    return dest.astype(jnp.int32), row_token, first, overflow, it_end[-1] > 0


ROW_DMA_UNROLL = 8


def _row_copy(src_hbm, src_row, dst, dst_row, sem, chunks):
    pitch = _row_pitch(chunks)
    return pltpu.make_async_copy(src_hbm.at[pl.ds(src_row * pitch, chunks), :],
                                 dst.at[pl.ds(dst_row * pitch, chunks), :], sem)


def _rows_wait(src_hbm, dst, sem, rows, chunks):
    n = rows * chunks
    pltpu.make_async_copy(src_hbm.at[pl.ds(0, n), :], dst.at[pl.ds(0, n), :], sem).wait()


def _dispatch_kernel(tok_ref, u_hbm, o_ref, stage_ref, sem, *, rb, chunks):
    i = pl.program_id(0)
    pitch = _row_pitch(chunks)

    def issue(step, slot):
        def body(g, c):
            for k in range(ROW_DMA_UNROLL):
                r = g * ROW_DMA_UNROLL + k
                _row_copy(u_hbm, tok_ref[step * rb + r], stage_ref.at[slot], r, sem.at[slot],
                          chunks).start(priority=k % 2)
            return c
        lax.fori_loop(0, rb // ROW_DMA_UNROLL, body, 0)

    @pl.when(i == 0)
    def _():
        issue(0, 0)

    @pl.when(i + 1 < pl.num_programs(0))
    def _():
        issue(i + 1, (i + 1) % 2)

    slot = i % 2
    _rows_wait(u_hbm, stage_ref.at[slot], sem.at[slot], rb, chunks)
    for c in range(chunks):
        o_ref[:, c * LANES:(c + 1) * LANES] = stage_ref[slot, pl.ds(c, rb, stride=pitch), :].astype(o_ref.dtype)


def _dispatch(u_lin, row_token, d):
    p_rows = row_token.shape[0]
    rb = 2 * MOE_ROW_BLOCK if p_rows % (2 * MOE_ROW_BLOCK) == 0 else MOE_ROW_BLOCK
    chunks = d // LANES
    return pl.pallas_call(
        functools.partial(_dispatch_kernel, rb=rb, chunks=chunks),
        grid_spec=pltpu.PrefetchScalarGridSpec(
            num_scalar_prefetch=1,
            grid=(p_rows // rb,),
            in_specs=[pl.BlockSpec(memory_space=pl.ANY)],
            out_specs=pl.BlockSpec((rb, d), lambda i, tok: (i, 0)),
            scratch_shapes=[pltpu.VMEM((2, rb * _row_pitch(chunks), LANES), F32),
                            pltpu.SemaphoreType.DMA((2,))]),
        out_shape=jax.ShapeDtypeStruct((p_rows, d), BF16),
        compiler_params=_cparams(1, 40),
        name="moe_dispatch",
    )(row_token, u_lin)


def _moe_ffn_kernel(*refs, rb, nf, n_items, chunks, fill_tail):
    e_ref, row0_ref, nsub_ref, xs_hbm, wg_ref, wu_ref, wd_ref = refs[:7]
    ys_hbm, x_ref, acc_ref, stage_ref, sem = refs[7 if fill_tail else 8:]
    w = pl.program_id(0)
    f = pl.program_id(1)
    nsub = nsub_ref[w]
    row0 = row0_ref[w]

    def for_range(n, fn):
        def body(r, c):
            fn(r)
            return c
        lax.fori_loop(0, n, body, 0)

    def load(r):
        return pltpu.make_async_copy(xs_hbm.at[pl.ds(pl.multiple_of(row0 + r * rb, rb), rb), :],
                                     x_ref.at[pl.ds(pl.multiple_of(r * rb, rb), rb), :], sem.at[0])

    pitch = _row_pitch(chunks)

    def store(row, slot):
        return pltpu.make_async_copy(
            stage_ref.at[slot], ys_hbm.at[pl.ds(pl.multiple_of(row * pitch, rb * pitch), rb * pitch), :],
            sem.at[1 + slot])

    @pl.when(f == 0)
    def _():
        for_range(nsub, lambda r: load(r).start())
        acc_ref[...] = jnp.zeros_like(acc_ref)
        for_range(nsub, lambda r: load(r).wait())

    @pl.when(nsub > 0)
    def _():
        def chunk(start, size):
            rows = pl.ds(pl.multiple_of(start, rb), size)
            acc_ref[rows, :] += _swiglu_partial(x_ref[rows, :], wg_ref[...].astype(BF16),
                                                wu_ref[...].astype(BF16), wd_ref[...].astype(BF16))

        big = MOE_CHUNK_BLOCKS
        for_range(nsub // big, lambda p: chunk(p * (big * rb), big * rb))
        for rem in range(1, big):
            @pl.when(nsub % big == rem)
            def _(rem=rem):
                chunk((nsub - rem) * rb, rem * rb)

    @pl.when(f == nf - 1)
    def _():
        def emit(r):
            slot = r % 2

            @pl.when(r >= 2)
            def _():
                store(row0 + (r - 2) * rb, slot).wait()

            _store_linear(stage_ref.at[slot], acc_ref[pl.ds(pl.multiple_of(r * rb, rb), rb), :], chunks)
            store(row0 + r * rb, slot).start()

        for_range(nsub, emit)

        @pl.when(nsub >= 2)
        def _():
            store(row0, nsub % 2).wait()

        @pl.when(nsub >= 1)
        def _():
            store(row0, (nsub - 1) % 2).wait()

    if fill_tail:
        @pl.when((f == nf - 1) & (w == n_items - 1))
        def _():
            used = row0_ref[n_items]
            n_tail = (ys_hbm.shape[0] // pitch - used) // rb
            stage_ref[0] = jnp.zeros(stage_ref.shape[1:], F32)
            for_range(n_tail, lambda r: store(used + r * rb, 0).start())
            for_range(n_tail, lambda r: store(used + r * rb, 0).wait())


def _moe_ffn(xs, w_gate_up, w_down, layer, items, ys_in=None):
    item_e, item_row0, item_nsub = items
    extra = () if ys_in is None else (ys_in,)
    hbm = pl.BlockSpec(memory_space=pl.ANY)
    p_rows, d = xs.shape
    dff = w_down.shape[2]
    rb, st = MOE_ROW_BLOCK, MOE_SUPER_BLOCKS
    n_items = item_e.shape[0]
    chunks = d // LANES
    tf = 256
    nf = dff // tf

    def f_eff(f, nsub, w):
        return jnp.where(nsub[w] > 0, f, nf - 1)

    return pl.pallas_call(
        functools.partial(_moe_ffn_kernel, rb=rb, nf=nf, n_items=n_items, chunks=chunks, fill_tail=ys_in is None),
        grid_spec=pltpu.PrefetchScalarGridSpec(
            num_scalar_prefetch=3,
            grid=(n_items, nf),
            in_specs=[hbm,
                      pl.BlockSpec((None, None, d, tf), lambda w, f, e, r0, ns: (layer, e[w], 0, f_eff(f, ns, w))),
                      pl.BlockSpec((None, None, d, tf),
                                   lambda w, f, e, r0, ns: (layer, e[w], 0, nf + f_eff(f, ns, w))),
                      pl.BlockSpec((None, None, tf, d), lambda w, f, e, r0, ns: (layer, e[w], f_eff(f, ns, w), 0))]
            + [hbm] * len(extra),
            out_specs=hbm,
            scratch_shapes=[pltpu.VMEM((st * rb, d), BF16), pltpu.VMEM((st * rb, d), F32),
                            pltpu.VMEM((2, rb * _row_pitch(chunks), LANES), F32), pltpu.SemaphoreType.DMA((3,))]),
        out_shape=jax.ShapeDtypeStruct((p_rows * _row_pitch(chunks), LANES), F32),
        input_output_aliases={} if ys_in is None else {7: 0},
        compiler_params=_cparams(2, 58),
        name="moe_ffn",
    )(item_e, item_row0, item_nsub, xs, w_gate_up, w_gate_up, w_down, *extra)


def _moe_combine_ln_kernel(*refs, alpha, emit_u, tm, chunks):
    if emit_u:
        (pos_ref, ys_hbm, wt_ref, x_ref, g_ref, lg_ref, lb_ref, sc_ref, sh_ref, xo_ref, uo_ref,
         stage_ref, y_ref, sem) = refs
        sc_sh = (sc_ref[0], sh_ref[0])
    else:
        pos_ref, ys_hbm, wt_ref, x_ref, g_ref, lg_ref, lb_ref, xo_ref, stage_ref, y_ref, sem = refs
        sc_sh = None
    i = pl.program_id(0)
    pitch = _row_pitch(chunks)

    def issue(step, slot):
        def body(g, c):
            for j in range(ROW_DMA_UNROLL // TOP_K):
                t = g * (ROW_DMA_UNROLL // TOP_K) + j
                for k in range(TOP_K):
                    _row_copy(ys_hbm, pos_ref[(step * tm + t) * TOP_K + k], stage_ref.at[slot, k], t,
                              sem.at[slot, k], chunks).start(priority=k % 2)
            return c
        lax.fori_loop(0, tm * TOP_K // ROW_DMA_UNROLL, body, 0)

    @pl.when(i == 0)
    def _():
        issue(0, 0)

    @pl.when(i + 1 < pl.num_programs(0))
    def _():
        issue(i + 1, (i + 1) % 2)

    slot = i % 2
    for k in range(TOP_K):
        _rows_wait(ys_hbm, stage_ref.at[slot, k], sem.at[slot, k], tm, chunks)
    w1 = wt_ref[:, 0:1]
    w2 = wt_ref[:, 1:2]
    for c in range(chunks):
        y_ref[:, c * LANES:(c + 1) * LANES] = (stage_ref[slot, 0, pl.ds(c, tm, stride=pitch), :] * w1
                                               + stage_ref[slot, 1, pl.ds(c, tm, stride=pitch), :] * w2)
    z = alpha * x_ref[...] + (1.0 + g_ref[0]) * y_ref[...]
    xn, u = _ln_modulate(z, lg_ref[0], lb_ref[0], sc_sh)
    xo_ref[...] = xn
    if emit_u:
        uo_ref[...] = u.astype(uo_ref.dtype)


def _moe_combine_ln(ys_lin, pos, wt, x, ada_rows, ln_g, ln_b, *, alpha, seq, layer, comp_g, nxt):
    t, d = x.shape
    tm = 256
    tpb = seq // tm
    chunks = d // LANES
    row = lambda l, comp: pl.BlockSpec((1, 1, d), lambda i, p: (_ada_row(l, i // tpb, comp), 0, 0))
    tile = pl.BlockSpec((tm, d), lambda i, p: (i, 0))
    lnp = pl.BlockSpec((1, 1, d), lambda i, p: (layer, 0, 0))
    in_specs = [pl.BlockSpec(memory_space=pl.ANY), pl.BlockSpec((tm, LANES), lambda i, p: (i, 0)), tile,
                row(layer, comp_g), lnp, lnp]
    args = [ys_lin, wt, x, ada_rows, ln_g, ln_b]
    out_specs = [tile]
    out_shape = [jax.ShapeDtypeStruct((t, d), F32)]
    if nxt is not None:
        in_specs += [row(nxt[0], nxt[1]), row(nxt[0], nxt[2])]
        args += [ada_rows, ada_rows]
        out_specs.append(tile)
        out_shape.append(jax.ShapeDtypeStruct((t, d), BF16))
    outs = pl.pallas_call(
        functools.partial(_moe_combine_ln_kernel, alpha=alpha, emit_u=nxt is not None, tm=tm, chunks=chunks),
        grid_spec=pltpu.PrefetchScalarGridSpec(
            num_scalar_prefetch=1,
            grid=(t // tm,),
            in_specs=in_specs, out_specs=out_specs,
            scratch_shapes=[pltpu.VMEM((2, TOP_K, tm * _row_pitch(chunks), LANES), F32), pltpu.VMEM((tm, d), F32),
                            pltpu.SemaphoreType.DMA((2, TOP_K))]),
        out_shape=out_shape,
        compiler_params=_cparams(1, 48),
        name="moe_combine_ln",
    )(pos, *args)
    return (outs[0], outs[1]) if nxt is not None else (outs[0], None)


def _rope_cos_sin(positions, dim):
    inv_freq = ROPE_THETA ** (-jnp.arange(0, dim, 2, dtype=F32) / dim)
    ang = positions.astype(F32).reshape(-1)[:, None] * inv_freq
    return jnp.cos(ang), jnp.sin(ang)


def _mla_rope_tables(positions):
    cos, sin = _rope_cos_sin(positions, MLA_ROPE)
    z = jnp.zeros_like(cos)
    return jnp.concatenate([cos, z, cos, z], axis=1), jnp.concatenate([-sin, z, sin, z], axis=1)


def _moba_rope_tables(positions, scale):
    cos, sin = _rope_cos_sin(positions, MOBA_ROT_DIM)
    t, half = cos.shape
    rest = LANES - 2 * half
    c = jnp.concatenate([cos, cos, jnp.ones((t, rest), F32)], axis=1)
    s1 = jnp.concatenate([jnp.zeros((t, half), F32), sin, jnp.zeros((t, rest), F32)], axis=1)
    s2 = jnp.concatenate([-sin, jnp.zeros((t, half + rest), F32)], axis=1)
    rot = jnp.concatenate([c, s1, s2], axis=1)
    ident = jnp.concatenate([jnp.ones((t, LANES), F32), jnp.zeros((t, 2 * LANES), F32)], axis=1)
    return jnp.stack([rot * scale, rot, ident])


def _spread_rope_cols(w_rope):
    half = MLA_ROPE // 2
    z = jnp.zeros(w_rope.shape[:-1] + (LANES // 2 - half,), w_rope.dtype)
    return jnp.concatenate([w_rope[..., :half], z, w_rope[..., half:], z], axis=-1)


def _mla_weights(w_down, w_uq, ql, kvl):
    k = w_uq.shape[0]
    heads = w_uq.shape[1] // (MLA_NOPE + MLA_ROPE)
    wd = jnp.concatenate([w_down[:, :ql + kvl], _spread_rope_cols(w_down[:, ql + kvl:])], axis=1).astype(BF16)
    wq = w_uq.reshape(k, heads, MLA_NOPE + MLA_ROPE)
    wq = jnp.concatenate([wq[..., :MLA_NOPE], _spread_rope_cols(wq[..., MLA_NOPE:])], axis=-1)
    return wd, wq.reshape(k, heads * MLA_HEAD_PAD).astype(BF16), heads


def kernel(x, c, positions, w_ada, b_ada, ln_mix_g, ln_mix_b, ln_ffn_g, ln_ffn_b, mla_w_down, mla_q_norm,
           mla_kv_norm, mla_w_uq, mla_w_ukv, mla_w_o, moba_w_qkv, moba_w_o, ffn_w_gate_up, ffn_w_down,
           moe_w_router, moe_w_gate_up, moe_w_down):
    batch, seq, d = x.shape
    depth = w_ada.shape[0]
    t = batch * seq
    alpha = (2.0 * depth) ** 0.25
    ql = mla_q_norm.shape[1]
    kvl = mla_kv_norm.shape[1]
    n_exp = moe_w_router.shape[2]

    ada_rows = _ada_all(c, w_ada, b_ada)
    cos_mla, sin_mla = _mla_rope_tables(positions)
    moba_tables = _moba_rope_tables(positions, MOBA_HEAD_DIM ** -0.5 * LOG2_E)
    mla_w_ukv_b = mla_w_ukv.astype(BF16)
    mla_w_o_b = mla_w_o.astype(BF16)
    moba_w_o_b = moba_w_o.astype(BF16)
    ln3 = lambda p: p.reshape(depth, 1, d)
    ln_mix_g, ln_mix_b, ln_ffn_g, ln_ffn_b = ln3(ln_mix_g), ln3(ln_mix_b), ln3(ln_ffn_g), ln3(ln_ffn_b)
    q_norm3 = mla_q_norm.reshape(-1, 1, ql)
    kv_norm3 = mla_kv_norm.reshape(-1, 1, kvl)

    xf = x.reshape(t, d)
    u = None
    for l in range(depth):
        j = l // 2
        moe_layer = l % 2 == 1
        if l % 2 == 0:
            wd_p, wq_p, heads = _mla_weights(mla_w_down[j], mla_w_uq[j], ql, kvl)
            first = (xf, (ada_rows, seq, l, 1, 0)) if u is None else (u, None)
            cq, ckv, kr = _mla_down(first[0], wd_p, q_norm3, kv_norm3, j, cos_mla, sin_mla, ql, kvl, mod=first[1])
            q, kv = _mla_up(cq, ckv, wq_p, mla_w_ukv_b, j, cos_mla, sin_mla,
                            (MLA_NOPE + MLA_ROPE) ** -0.5 * LOG2_E)
            o = _mla_attention(q, kv, kr, batch, seq, heads)
            y = (o, mla_w_o_b, j)
        else:
            heads = moba_w_qkv.shape[2] // (3 * MOBA_HEAD_DIM)
            qkv = _moba_qkv(u, moba_w_qkv, j, moba_tables)
            o = _moba_attention(qkv, batch, seq, heads)
            y = (o, moba_w_o_b, j)
        ln_args = dict(alpha=alpha, seq=seq, layer=l)
        nxt = (l + 1, 1, 0) if l + 1 < depth else None
        if not moe_layer:
            xf, u = _resid_ln(xf, y, ada_rows, ln_mix_g, ln_mix_b, comp_g=2, nxt=(l, 4, 3), **ln_args)
            y = _dense_ffn(u, ffn_w_gate_up, ffn_w_down, j)
            outs = _resid_ln(xf, y, ada_rows, ln_ffn_g, ln_ffn_b, comp_g=5, nxt=nxt, **ln_args)
            xf, u = outs[0], (outs[1] if nxt is not None else None)
        else:
            w_router_pad = jnp.zeros((d, LANES), F32).at[:, :n_exp].set(moe_w_router[j])
            xf, u_lin, idx, wt = _resid_ln(xf, y, ada_rows, ln_mix_g, ln_mix_b, comp_g=2, nxt=(l, 4, 3),
                                           w_router_pad=w_router_pad, n_exp=n_exp, **ln_args)
            pos, row_token, first, overflow, any_overflow = _route_metadata(idx[:, :TOP_K], n_exp)
            xs = _dispatch(u_lin, row_token, d)
            ys_lin = _moe_ffn(xs, moe_w_gate_up, moe_w_down, j, first)
            ys_lin = lax.cond(any_overflow,
                              lambda ys: _moe_ffn(xs, moe_w_gate_up, moe_w_down, j, overflow, ys_in=ys),
                              lambda ys: ys, ys_lin)
            xf, u = _moe_combine_ln(ys_lin, pos, wt, xf, ada_rows, ln_ffn_g, ln_ffn_b, comp_g=5, nxt=nxt, **ln_args)
    return xf.reshape(batch, seq, d)
```
